```python
import math
import jax, jax.numpy as jnp
from jax import lax
import numpy as np

D_MODEL = 1024
BATCH = 8
SEQ = 4096
DEPTH = 4

ROPE_BASE = 10000.0
NORM_EPS = 1e-6
GN_EPS = 1e-5
N_BRANCH = 3
FFN_RES = 0.5
D_FF = 2816
N_SUBNORMS = 6
MAX_POS_OFFSET = 1024
RET_HEADS = 4
RET_DK = 128
RET_DV = 128
RET_CHUNK = 128
RET_QK_W = RET_HEADS * RET_DK
RET_V_W = RET_HEADS * RET_DV
S5_WIDTH = 512
S5_GROUP = 16
S5_STATE = 64
S5_GROUPS = S5_WIDTH // S5_GROUP
S5_DT_MIN = 0.001
S5_DT_MAX = 0.1
MLA_HEADS = 4
MLA_Q_RANK = 256
MLA_KV_RANK = 128
MLA_NOPE = 128
MLA_ROPE = 64
MLA_DV = 128
MLA_V_W = MLA_HEADS * MLA_DV
ATTN_BLOCK = 128
IN_SIZES = (RET_QK_W, RET_QK_W, RET_V_W, RET_V_W, S5_WIDTH, MLA_Q_RANK, MLA_KV_RANK, MLA_ROPE, N_BRANCH * D_MODEL)
IN_WIDTH = RET_QK_W * 2 + RET_V_W * 2 + S5_WIDTH + MLA_Q_RANK + MLA_KV_RANK + MLA_ROPE + N_BRANCH * D_MODEL

kernel_name = "hybrid_retention_s5_mla_macaron"


def rms_norm(x, g):
    xf = x.astype(jnp.float32)
    y = xf * lax.rsqrt(jnp.mean(xf * xf, axis=-1, keepdims=True) + NORM_EPS)
    return (y * g.astype(jnp.float32)).astype(x.dtype)


def swiglu(h, w_gate, w_up, w_down):
    return (jax.nn.silu(h @ w_gate) * (h @ w_up)) @ w_down


def rope_cos_sin(positions, dim):
    inv_freq = ROPE_BASE ** (-jnp.arange(0, dim, 2, dtype=jnp.float32) / dim)
    ang = positions.astype(jnp.float32)[..., None] * inv_freq
    return jnp.cos(ang), jnp.sin(ang)


def apply_rope(x, cos, sin):
    half = x.shape[-1] // 2
    x1 = x[..., :half]
    x2 = x[..., half:]
    c = cos[:, :, None, :]
    s = sin[:, :, None, :]
    return jnp.concatenate([x1 * c - x2 * s, x2 * c + x1 * s], axis=-1).astype(x.dtype)


def split_columns(p):
    outs = []
    start = 0
    for size in IN_SIZES:
        outs.append(p[..., start:start + size])
        start += size
    return outs


def retention(q, k, v, g, cos, sin, w_o):
    B, L, _ = q.shape
    C = RET_CHUNK
    nc = L // C
    f32 = jnp.float32
    q = apply_rope(q.reshape(B, L, RET_HEADS, RET_DK), cos, sin).astype(f32)
    k = apply_rope(k.reshape(B, L, RET_HEADS, RET_DK), cos, sin).astype(f32) * (RET_DK ** -0.5)
    v = v.reshape(B, L, RET_HEADS, RET_DV).astype(f32)
    qc = q.reshape(B, nc, C, RET_HEADS, RET_DK)
    kc = k.reshape(B, nc, C, RET_HEADS, RET_DK)
    vc = v.reshape(B, nc, C, RET_HEADS, RET_DV)
    log_gamma = jnp.log1p(-jnp.exp2(-5.0 - jnp.arange(RET_HEADS, dtype=f32)))
    pos = jnp.arange(C, dtype=f32)
    rel = pos[:, None] - pos[None, :]
    intra = jnp.where(rel[None] >= 0.0,
                      jnp.exp(jnp.maximum(rel, 0.0)[None] * log_gamma[:, None, None]), 0.0)
    scores = jnp.einsum('bnchk,bnmhk->bnhcm', qc, kc) * intra
    inner = jnp.einsum('bnhcm,bnmhv->bnchv', scores, vc)
    k_decay = jnp.exp((C - 1.0 - pos)[:, None] * log_gamma[None, :])
    q_decay = jnp.exp((pos + 1.0)[:, None] * log_gamma[None, :])
    chunk_kv = jnp.einsum('bnmhk,bnmhv->nbhkv', kc * k_decay[:, :, None], vc)
    chunk_decay = jnp.exp(C * log_gamma)[:, None, None]

    def step(state, kv):
        return chunk_decay * state + kv, state

    _, s_prev = lax.scan(step, jnp.zeros((B, RET_HEADS, RET_DK, RET_DV), f32), chunk_kv)
    cross = jnp.einsum('bnchk,nbhkv->bnchv', qc * q_decay[:, :, None], s_prev)
    o = (inner + cross).reshape(B, L, RET_HEADS, RET_DV)
    mu = jnp.mean(o, axis=-1, keepdims=True)
    var = jnp.mean(jnp.square(o - mu), axis=-1, keepdims=True)
    o = ((o - mu) * lax.rsqrt(var + GN_EPS)).reshape(B, L, RET_V_W).astype(g.dtype)
    return (jax.nn.silu(g) * o) @ w_o


def _complex_scan_combine(e1, e2):
    a1r, a1i, b1r, b1i = e1
    a2r, a2i, b2r, b2i = e2
    return (a2r * a1r - a2i * a1i,
            a2r * a1i + a2i * a1r,
            a2r * b1r - a2i * b1i + b2r,
            a2r * b1i + a2i * b1r + b2i)


def s5_mixer(u, a_re, a_im, log_dt, b_re, b_im, c_re, c_im, d, w_glu_a, w_glu_b):
    B, L, _ = u.shape
    f32 = jnp.float32
    a_re = a_re.astype(f32)
    a_im = a_im.astype(f32)
    dt = jnp.exp(log_dt.astype(f32))[:, None]
    mag = jnp.exp(a_re * dt)
    abar_re = mag * jnp.cos(a_im * dt)
    abar_im = mag * jnp.sin(a_im * dt)
    den = a_re * a_re + a_im * a_im
    nr = abar_re - 1.0
    f_re = (nr * a_re + abar_im * a_im) / den
    f_im = (abar_im * a_re - nr * a_im) / den
    b_re = b_re.astype(f32)
    b_im = b_im.astype(f32)
    bbar_re = f_re[..., None] * b_re - f_im[..., None] * b_im
    bbar_im = f_re[..., None] * b_im + f_im[..., None] * b_re
    ug = u.astype(f32).reshape(B, L, S5_GROUPS, S5_GROUP)
    bu_re = jnp.einsum('blgh,gph->blgp', ug, bbar_re)
    bu_im = jnp.einsum('blgh,gph->blgp', ug, bbar_im)
    shape_a = (1, L, S5_GROUPS, S5_STATE)
    elems = (jnp.broadcast_to(abar_re, shape_a), jnp.broadcast_to(abar_im, shape_a), bu_re, bu_im)
    _, _, s_re, s_im = lax.associative_scan(_complex_scan_combine, elems, axis=1)
    y = (jnp.einsum('blgp,ghp->blgh', s_re, c_re.astype(f32))
         - jnp.einsum('blgp,ghp->blgh', s_im, c_im.astype(f32))).reshape(B, L, S5_WIDTH)
    y = jax.nn.gelu(y + d.astype(f32) * u.astype(f32)).astype(u.dtype)
    return (y @ w_glu_a) * jax.nn.sigmoid(y @ w_glu_b)


def mla(c_q, c_kv, k_rope, cos, sin, q_norm, kv_norm, w_uq, w_ukv, w_o):
    B, L, _ = c_q.shape
    q = (rms_norm(c_q, q_norm) @ w_uq).reshape(B, L, MLA_HEADS, MLA_NOPE + MLA_ROPE)
    q = jnp.concatenate([q[..., :MLA_NOPE], apply_rope(q[..., MLA_NOPE:], cos, sin)], axis=-1)
    kv = (rms_norm(c_kv, kv_norm) @ w_ukv).reshape(B, L, MLA_HEADS, MLA_NOPE + MLA_DV)
    k_pe = apply_rope(k_rope[:, :, None, :], cos, sin)
    k = jnp.concatenate([kv[..., :MLA_NOPE],
                         jnp.broadcast_to(k_pe, (B, L, MLA_HEADS, MLA_ROPE)).astype(kv.dtype)], axis=-1)
    v = kv[..., MLA_NOPE:]
    scale = (MLA_NOPE + MLA_ROPE) ** -0.5
    outs = []
    for i in range(L // ATTN_BLOCK):
        q0 = i * ATTN_BLOCK
        kend = q0 + ATTN_BLOCK
        qs = q[:, q0:kend]
        s = jnp.einsum('bqhd,bkhd->bhqk', qs, k[:, :kend]).astype(jnp.float32) * scale
        qi = q0 + jnp.arange(ATTN_BLOCK)
        ki = jnp.arange(kend)
        s = jnp.where(ki[None, :] <= qi[:, None], s, -jnp.inf)
        p = jax.nn.softmax(s, axis=-1).astype(v.dtype)
        outs.append(jnp.einsum('bhqk,bkhd->bqhd', p, v[:, :kend]))
    o = jnp.concatenate(outs, axis=1).reshape(B, L, MLA_V_W)
    return o @ w_o


def setup_inputs(seed: int = 0) -> dict:
    key = jax.random.key(seed)
    ks = jax.random.split(key, 24)
    f32 = jnp.float32

    def nrm(k, shape, fan_in):
        return jax.random.normal(k, shape, f32) * (fan_in ** -0.5)

    x = jax.random.normal(ks[0], (BATCH, SEQ, D_MODEL), f32)
    offsets = jax.random.randint(ks[1], (BATCH, 1), 0, MAX_POS_OFFSET, dtype=jnp.int32)
    positions = offsets + jnp.arange(SEQ, dtype=jnp.int32)[None, :]
    norm_gains = 1.0 + 0.05 * jax.random.normal(ks[2], (DEPTH, N_SUBNORMS, D_MODEL), f32)
    ffn_w_gate = nrm(ks[3], (DEPTH, 2, D_MODEL, D_FF), D_MODEL)
    ffn_w_up = nrm(ks[4], (DEPTH, 2, D_MODEL, D_FF), D_MODEL)
    ffn_w_down = nrm(ks[5], (DEPTH, 2, D_FF, D_MODEL), D_FF)
    w_in = nrm(ks[6], (DEPTH, D_MODEL, IN_WIDTH), D_MODEL)
    ret_w_o = nrm(ks[7], (DEPTH, RET_V_W, D_MODEL), RET_V_W)
    s5_a_re = -0.5 + 0.01 * jax.random.normal(ks[8], (DEPTH, S5_GROUPS, S5_STATE), f32)
    s5_a_im = (jnp.pi * jnp.arange(S5_STATE, dtype=f32))[None, None, :] + 0.01 * jax.random.normal(ks[9], (DEPTH, S5_GROUPS, S5_STATE), f32)
    s5_log_dt = jax.random.uniform(ks[10], (DEPTH, S5_GROUPS), f32, math.log(S5_DT_MIN), math.log(S5_DT_MAX))
    s5_b_re = nrm(ks[11], (DEPTH, S5_GROUPS, S5_STATE, S5_GROUP), 2 * S5_GROUP)
    s5_b_im = nrm(ks[12], (DEPTH, S5_GROUPS, S5_STATE, S5_GROUP), 2 * S5_GROUP)
    s5_c_re = nrm(ks[13], (DEPTH, S5_GROUPS, S5_GROUP, S5_STATE), 2 * S5_STATE)
    s5_c_im = nrm(ks[14], (DEPTH, S5_GROUPS, S5_GROUP, S5_STATE), 2 * S5_STATE)
    s5_d = jax.random.normal(ks[15], (DEPTH, S5_WIDTH), f32)
    s5_glu_a = nrm(ks[16], (DEPTH, S5_WIDTH, D_MODEL), S5_WIDTH)
    s5_glu_b = nrm(ks[17], (DEPTH, S5_WIDTH, D_MODEL), S5_WIDTH)
    mla_q_norm = 1.0 + 0.05 * jax.random.normal(ks[18], (DEPTH, MLA_Q_RANK), f32)
    mla_kv_norm = 1.0 + 0.05 * jax.random.normal(ks[19], (DEPTH, MLA_KV_RANK), f32)
    mla_w_uq = nrm(ks[20], (DEPTH, MLA_Q_RANK, MLA_HEADS * (MLA_NOPE + MLA_ROPE)), MLA_Q_RANK)
    mla_w_ukv = nrm(ks[21], (DEPTH, MLA_KV_RANK, MLA_HEADS * (MLA_NOPE + MLA_DV)), MLA_KV_RANK)
    mla_w_o = nrm(ks[22], (DEPTH, MLA_V_W, D_MODEL), MLA_V_W)
    w_out = nrm(ks[23], (DEPTH, D_MODEL, D_MODEL), D_MODEL)
    return {"x": x, "positions": positions, "norm_gains": norm_gains,
            "ffn_w_gate": ffn_w_gate, "ffn_w_up": ffn_w_up, "ffn_w_down": ffn_w_down,
            "w_in": w_in, "ret_w_o": ret_w_o,
            "s5_a_re": s5_a_re, "s5_a_im": s5_a_im, "s5_log_dt": s5_log_dt,
            "s5_b_re": s5_b_re, "s5_b_im": s5_b_im, "s5_c_re": s5_c_re, "s5_c_im": s5_c_im,
            "s5_d": s5_d, "s5_glu_a": s5_glu_a, "s5_glu_b": s5_glu_b,
            "mla_q_norm": mla_q_norm, "mla_kv_norm": mla_kv_norm,
            "mla_w_uq": mla_w_uq, "mla_w_ukv": mla_w_ukv, "mla_w_o": mla_w_o,
            "w_out": w_out}


def reference(x, positions, norm_gains, ffn_w_gate, ffn_w_up, ffn_w_down, w_in, ret_w_o,
              s5_a_re, s5_a_im, s5_log_dt, s5_b_re, s5_b_im, s5_c_re, s5_c_im, s5_d,
              s5_glu_a, s5_glu_b, mla_q_norm, mla_kv_norm, mla_w_uq, mla_w_ukv, mla_w_o, w_out):
    B, L, _ = x.shape
    cos_r, sin_r = rope_cos_sin(positions, RET_DK)
    cos_m, sin_m = rope_cos_sin(positions, MLA_ROPE)
    for l in range(DEPTH):
        n = norm_gains[l]
        h = rms_norm(x, n[0])
        x = x + FFN_RES * rms_norm(swiglu(h, ffn_w_gate[l, 0], ffn_w_up[l, 0], ffn_w_down[l, 0]), n[1])
        h = rms_norm(x, n[2])
        q_r, k_r, v_r, g_r, u_s, c_q, c_kv, k_pe, gates = split_columns(h @ w_in[l])
        y_ret = retention(q_r, k_r, v_r, g_r, cos_r, sin_r, ret_w_o[l])
        y_s5 = s5_mixer(u_s, s5_a_re[l], s5_a_im[l], s5_log_dt[l], s5_b_re[l], s5_b_im[l],
                        s5_c_re[l], s5_c_im[l], s5_d[l], s5_glu_a[l], s5_glu_b[l])
        y_mla = mla(c_q, c_kv, k_pe, cos_m, sin_m, mla_q_norm[l], mla_kv_norm[l],
                    mla_w_uq[l], mla_w_ukv[l], mla_w_o[l])
        gt = jax.nn.sigmoid(gates.reshape(B, L, N_BRANCH, D_MODEL))
        merged = gt[:, :, 0] * y_ret + gt[:, :, 1] * y_s5 + gt[:, :, 2] * y_mla
        x = x + rms_norm(merged @ w_out[l], n[3])
        h = rms_norm(x, n[4])
        x = x + FFN_RES * rms_norm(swiglu(h, ffn_w_gate[l, 1], ffn_w_up[l, 1], ffn_w_down[l, 1]), n[5])
    return x
```

```python
import functools
import math

import jax
import jax.numpy as jnp
from jax import lax
from jax.experimental import pallas as pl
from jax.experimental.pallas import tpu as pltpu

F32 = jnp.float32
BF16 = jnp.bfloat16

ROPE_BASE = 10000.0
NORM_EPS = 1e-6
GN_EPS = 1e-5
FFN_RES = 0.5
RET_HEADS = 4
RET_DK = 128
RET_DV = 128
RET_CHUNK = 128
S5_GROUP = 16
S5_STATE = 64
MLA_HEADS = 4
MLA_NOPE = 128
MLA_ROPE = 64
MLA_DV = 128
MLA_HEAD_PAD = 256

LANES = 128
SUBLANES = 8
VMEM_LIMIT = 56 * 1024 * 1024

ROW_TILE = 512
FFN_CHUNKS = ((0, 1024), (1024, 2048), (2048, 2816))
RET_ROWS = 512
S5_TL = 64
S5_HALF = 1024
ATTN_TQ = 512


def _cparams(sem):
    return pltpu.CompilerParams(dimension_semantics=sem, vmem_limit_bytes=VMEM_LIMIT)


def _const_spec(shape):
    nd = len(shape)
    return pl.BlockSpec(shape, lambda *_: (0,) * nd, pipeline_mode=pl.Buffered(1))


def _rms(x, g):
    y = x * lax.rsqrt(jnp.mean(x * x, axis=-1, keepdims=True) + NORM_EPS)
    return y * g


def _dot(a, b):
    return jnp.dot(a, b, preferred_element_type=F32)


def _rope_kernel(pos_ref, fr_ref, sgr_ref, fm_ref, mc_ref, m1_ref, m2_ref,
                 cr_ref, sr_ref, cm_ref, s1_ref, s2_ref):
    pos = pos_ref[...]
    ang_r = pos * fr_ref[...]
    cr_ref[...] = jnp.cos(ang_r)
    sr_ref[...] = jnp.sin(ang_r) * sgr_ref[...]
    ang_m = pos * fm_ref[...]
    sin_m = jnp.sin(ang_m)
    cm_ref[...] = jnp.cos(ang_m) * mc_ref[...]
    s1_ref[...] = sin_m * m1_ref[...]
    s2_ref[...] = sin_m * m2_ref[...]


def _rope_tables(positions):
    B, L = positions.shape
    T = B * L
    pos = positions.astype(F32).reshape(T, 1)
    inv_r = ROPE_BASE ** (-jnp.arange(0, RET_DK, 2, dtype=F32) / RET_DK)
    inv_m = ROPE_BASE ** (-jnp.arange(0, MLA_ROPE, 2, dtype=F32) / MLA_ROPE)
    hm = MLA_ROPE // 2
    one_r = jnp.ones((RET_DK // 2,), F32)
    one_m = jnp.ones((hm,), F32)
    zero_m = jnp.zeros((hm,), F32)
    pad = jnp.zeros((LANES - MLA_ROPE,), F32)
    rows = [
        jnp.concatenate([inv_r, inv_r]),
        jnp.concatenate([-one_r, one_r]),
        jnp.concatenate([inv_m, inv_m, pad]),
        jnp.concatenate([one_m, one_m, pad]),
        jnp.concatenate([-one_m, zero_m, pad]),
        jnp.concatenate([zero_m, one_m, pad]),
    ]
    rows = [r.reshape(1, LANES) for r in rows]
    tm = 2048
    row_spec = pl.BlockSpec((1, LANES), lambda i: (0, 0))
    tab_spec = pl.BlockSpec((tm, LANES), lambda i: (i, 0))
    tab_shape = jax.ShapeDtypeStruct((T, LANES), F32)
    return pl.pallas_call(
        _rope_kernel,
        grid=(T // tm,),
        in_specs=[pl.BlockSpec((tm, 1), lambda i: (i, 0))] + [row_spec] * 6,
        out_specs=[tab_spec] * 5,
        out_shape=[tab_shape] * 5,
        compiler_params=_cparams(("parallel",)),
        name="rope_tables",
    )(pos, *rows)


def _ffn_kernel(x_ref, gpre_ref, gpost_ref, wg_ref, wu_ref, wd_ref, o_ref):
    x = x_ref[...]
    h = _rms(x, gpre_ref[...]).astype(BF16)
    y = None
    for c0, c1 in FFN_CHUNKS:
        g = _dot(h, wg_ref[:, c0:c1])
        u = _dot(h, wu_ref[:, c0:c1])
        a = (g * jax.nn.sigmoid(g) * u).astype(BF16)
        part = _dot(a, wd_ref[c0:c1, :])
        y = part if y is None else y + part
    o_ref[...] = x + FFN_RES * _rms(y, gpost_ref[...])


def _ffn(x2, gpre, gpost, wg, wu, wd):
    T, D = x2.shape
    tm = ROW_TILE
    row = pl.BlockSpec((tm, D), lambda i: (i, 0))
    return pl.pallas_call(
        _ffn_kernel,
        grid=(T // tm,),
        in_specs=[row, _const_spec(gpre.shape), _const_spec(gpost.shape),
                  _const_spec(wg.shape), _const_spec(wu.shape), _const_spec(wd.shape)],
        out_specs=row,
        out_shape=jax.ShapeDtypeStruct((T, D), F32),
        compiler_params=_cparams(("parallel",)),
        name="ffn",
    )(x2, gpre, gpost, wg, wu, wd)


def _rope_ret(v, cos, sin):
    return v * cos + pltpu.roll(v, RET_DK // 2, 1) * sin


def _rope_mla(v, cos, sin1, sin2):
    hm = MLA_ROPE // 2
    return v * cos + pltpu.roll(v, LANES - hm, 1) * sin1 + pltpu.roll(v, hm, 1) * sin2


def _proj_kernel(x_ref, g_ref, wr_ref, wu_ref, wm_ref, qn_ref, kvn_ref,
                 wuq_ref, wuk_ref, wuv_ref, cr_ref, sr_ref, cm_ref, s1_ref, s2_ref,
                 rq_ref, rk_ref, rv_ref, rg_ref, u_ref, mq_ref, mk_ref, mv_ref):
    h = _rms(x_ref[...], g_ref[...]).astype(BF16)
    cr = cr_ref[...]
    sr = sr_ref[...]
    qk_w = RET_HEADS * RET_DK
    v_w = RET_HEADS * RET_DV
    r = _dot(h, wr_ref[...])
    for hd in range(RET_HEADS):
        lo = hd * RET_DK
        q = r[:, lo:lo + RET_DK]
        k = r[:, qk_w + lo:qk_w + lo + RET_DK]
        rq_ref[:, lo:lo + RET_DK] = _rope_ret(q, cr, sr).astype(BF16)
        rk_ref[:, lo:lo + RET_DK] = (_rope_ret(k, cr, sr) * (RET_DK ** -0.5)).astype(BF16)
    rv_ref[...] = r[:, 2 * qk_w:2 * qk_w + v_w].astype(BF16)
    rg_ref[...] = r[:, 2 * qk_w + v_w:].astype(BF16)

    u_ref[...] = _dot(h, wu_ref[...])

    m = _dot(h, wm_ref[...])
    q_rank = qn_ref.shape[1]
    kv_rank = kvn_ref.shape[1]
    cq = _rms(m[:, :q_rank], qn_ref[...]).astype(BF16)
    ckv = _rms(m[:, q_rank:q_rank + kv_rank], kvn_ref[...]).astype(BF16)
    cm = cm_ref[...]
    s1 = s1_ref[...]
    s2 = s2_ref[...]
    kpe = _rope_mla(m[:, q_rank + kv_rank:], cm, s1, s2).astype(BF16)
    q = _dot(cq, wuq_ref[...])
    kn = _dot(ckv, wuk_ref[...])
    mv_ref[...] = _dot(ckv, wuv_ref[...]).astype(BF16)
    for hd in range(MLA_HEADS):
        lo = hd * MLA_HEAD_PAD
        mq_ref[:, lo:lo + MLA_NOPE] = q[:, lo:lo + MLA_NOPE].astype(BF16)
        mq_ref[:, lo + MLA_NOPE:lo + MLA_HEAD_PAD] = _rope_mla(
            q[:, lo + MLA_NOPE:lo + MLA_HEAD_PAD], cm, s1, s2).astype(BF16)
        mk_ref[:, lo:lo + MLA_NOPE] = kn[:, hd * MLA_NOPE:(hd + 1) * MLA_NOPE].astype(BF16)
        mk_ref[:, lo + MLA_NOPE:lo + MLA_HEAD_PAD] = kpe


def _proj(x2, gain, w, tabs):
    T, D = x2.shape
    tm = ROW_TILE
    row = lambda n: pl.BlockSpec((tm, n), lambda i: (i, 0))
    consts = [gain, w["wr"], w["wu"], w["wm"], w["qn"], w["kvn"], w["wuq"], w["wuk"], w["wuv"]]
    out_w = [(512, BF16), (512, BF16), (512, BF16), (512, BF16), (512, F32),
             (MLA_HEADS * MLA_HEAD_PAD, BF16), (MLA_HEADS * MLA_HEAD_PAD, BF16),
             (MLA_HEADS * MLA_DV, BF16)]
    return pl.pallas_call(
        _proj_kernel,
        grid=(T // tm,),
        in_specs=[row(D)] + [_const_spec(c.shape) for c in consts] + [row(LANES)] * 5,
        out_specs=[row(n) for n, _ in out_w],
        out_shape=[jax.ShapeDtypeStruct((T, n), dt) for n, dt in out_w],
        compiler_params=_cparams(("parallel",)),
        name="mixer_proj",
    )(x2, *consts, *tabs)


def _ret_kernel(q_ref, k_ref, v_ref, g_ref, intra_ref, qd_ref, kd_ref, cd_ref, o_ref, st_ref):
    @pl.when(pl.program_id(1) == 0)
    def _():
        st_ref[...] = jnp.zeros_like(st_ref)

    C = RET_CHUNK
    nt = (((1,), (1,)), ((), ()))
    tn = (((0,), (0,)), ((), ()))
    for hd in range(RET_HEADS):
        cols = slice(hd * RET_DK, (hd + 1) * RET_DK)
        vcols = slice(hd * RET_DV, (hd + 1) * RET_DV)
        intra = intra_ref[hd]
        qd = qd_ref[hd]
        kd = kd_ref[hd]
        cd = cd_ref[hd]
        state = st_ref[hd]
        for c in range(RET_ROWS // C):
            rows = slice(c * C, (c + 1) * C)
            q = q_ref[0, rows, cols]
            k = k_ref[0, rows, cols]
            v = v_ref[0, rows, vcols]
            scores = lax.dot_general(q, k, nt, preferred_element_type=F32) * intra
            inner = _dot(scores.astype(BF16), v)
            cross = _dot((q.astype(F32) * qd).astype(BF16), state.astype(BF16))
            kv = lax.dot_general((k.astype(F32) * kd).astype(BF16), v, tn,
                                 preferred_element_type=F32)
            state = cd * state + kv
            o = inner + cross
            mu = jnp.mean(o, axis=-1, keepdims=True)
            var = jnp.mean(jnp.square(o - mu), axis=-1, keepdims=True)
            on = (o - mu) * lax.rsqrt(var + GN_EPS)
            g = g_ref[0, rows, vcols].astype(F32)
            o_ref[0, rows, vcols] = (g * jax.nn.sigmoid(g) * on).astype(BF16)
        st_ref[hd] = state


def _ret_tables():
    C = RET_CHUNK
    log_gamma = jnp.log1p(-jnp.exp2(-5.0 - jnp.arange(RET_HEADS, dtype=F32)))
    pos = jnp.arange(C, dtype=F32)
    rel = pos[:, None] - pos[None, :]
    intra = jnp.where(rel[None] >= 0.0,
                      jnp.exp(jnp.maximum(rel, 0.0)[None] * log_gamma[:, None, None]), 0.0)
    k_decay = jnp.exp((C - 1.0 - pos)[:, None] * log_gamma[None, :])
    q_decay = jnp.exp((pos + 1.0)[:, None] * log_gamma[None, :])
    chunk_decay = jnp.exp(C * log_gamma)
    qd = jnp.broadcast_to(q_decay.T[:, :, None], (RET_HEADS, C, RET_DK))
    kd = jnp.broadcast_to(k_decay.T[:, :, None], (RET_HEADS, C, RET_DK))
    cd = jnp.broadcast_to(chunk_decay[:, None, None], (RET_HEADS, 1, RET_DV))
    return intra, qd, kd, cd


def _retention(rq, rk, rv, rg, tables, B, L):
    W = rq.shape[-1]
    shp = (B, L, W)
    blk = pl.BlockSpec((1, RET_ROWS, W), lambda b, i: (b, i, 0))
    return pl.pallas_call(
        _ret_kernel,
        grid=(B, L // RET_ROWS),
        in_specs=[blk] * 4 + [_const_spec(t.shape) for t in tables],
        out_specs=blk,
        out_shape=jax.ShapeDtypeStruct(shp, BF16),
        scratch_shapes=[pltpu.VMEM((RET_HEADS, RET_DK, RET_DV), F32)],
        compiler_params=_cparams(("parallel", "arbitrary")),
        name="retention",
    )(rq.reshape(shp), rk.reshape(shp), rv.reshape(shp), rg.reshape(shp), *tables)


def _s5_kernel(u_ref, are_ref, aim_ref, wbr_ref, wbi_ref, wcr_ref, wci_ref, d_ref, y_ref,
               sre_ref, sim_ref, u2_ref, ut_ref, bre_ref, bim_ref, yt_ref):
    B, tl, W = u_ref.shape
    nblk = wbr_ref.shape[0]
    in_w = wbr_ref.shape[1]
    st_w = wbr_ref.shape[2]

    @pl.when(pl.program_id(0) == 0)
    def _():
        sre_ref[...] = jnp.zeros_like(sre_ref)
        sim_ref[...] = jnp.zeros_like(sim_ref)

    nlb = W // LANES
    for b in range(B):
        for j in range(nlb):
            u2_ref[j, b * tl:(b + 1) * tl, :] = u_ref[b, :, j * LANES:(j + 1) * LANES]

    def gather(t, c):
        r0 = pl.multiple_of(t * B, B)
        for j in range(nlb):
            ut_ref[pl.ds(r0, B), j * LANES:(j + 1) * LANES] = u2_ref[j, pl.ds(t, B, stride=tl), :]
        return c
    lax.fori_loop(0, tl, gather, 0, unroll=8)

    for kb in range(nblk):
        lhs = ut_ref[:, kb * in_w:(kb + 1) * in_w].astype(BF16)
        bre_ref[:, kb * st_w:(kb + 1) * st_w] = _dot(lhs, wbr_ref[kb])
        bim_ref[:, kb * st_w:(kb + 1) * st_w] = _dot(lhs, wbi_ref[kb])

    n_state = sre_ref.shape[1]
    for hh in range(n_state // S5_HALF):
        lanes = slice(hh * S5_HALF, (hh + 1) * S5_HALF)
        ar = jnp.broadcast_to(are_ref[:, lanes], (B, S5_HALF))
        ai = jnp.broadcast_to(aim_ref[:, lanes], (B, S5_HALF))

        def step(t, carry, lanes=lanes, ar=ar, ai=ai):
            sr, si = carry
            r0 = pl.multiple_of(t * B, B)
            nsr = ar * sr - ai * si + bre_ref[pl.ds(r0, B), lanes]
            nsi = ar * si + ai * sr + bim_ref[pl.ds(r0, B), lanes]
            bre_ref[pl.ds(r0, B), lanes] = nsr
            bim_ref[pl.ds(r0, B), lanes] = nsi
            return nsr, nsi
        sr, si = lax.fori_loop(0, tl, step, (sre_ref[:, lanes], sim_ref[:, lanes]), unroll=4)
        sre_ref[:, lanes] = sr
        sim_ref[:, lanes] = si

    for kb in range(nblk):
        sl = slice(kb * st_w, (kb + 1) * st_w)
        ch = slice(kb * in_w, (kb + 1) * in_w)
        y = (_dot(bre_ref[:, sl].astype(BF16), wcr_ref[kb])
             - _dot(bim_ref[:, sl].astype(BF16), wci_ref[kb]))
        yt_ref[kb] = jax.nn.gelu(y + d_ref[:, ch] * ut_ref[:, ch])

    for b in range(B):
        for j in range(nlb):
            y_ref[b, :, j * LANES:(j + 1) * LANES] = yt_ref[j, pl.ds(b, tl, stride=B), :]


def _s5(u, w, B, L):
    W = u.shape[-1]
    tl = S5_TL
    n_state = w["are"].shape[1]
    blk = pl.BlockSpec((B, tl, W), lambda i: (0, i, 0))
    consts = [w["are"], w["aim"], w["wbr"], w["wbi"], w["wcr"], w["wci"], w["d"]]
    return pl.pallas_call(
        _s5_kernel,
        grid=(L // tl,),
        in_specs=[blk] + [_const_spec(c.shape) for c in consts],
        out_specs=blk,
        out_shape=jax.ShapeDtypeStruct((B, L, W), F32),
        scratch_shapes=[pltpu.VMEM((B, n_state), F32), pltpu.VMEM((B, n_state), F32),
                        pltpu.VMEM((W // LANES, B * tl, LANES), F32), pltpu.VMEM((B * tl, W), F32),
                        pltpu.VMEM((B * tl, n_state), F32), pltpu.VMEM((B * tl, n_state), F32),
                        pltpu.VMEM((W // LANES, B * tl, LANES), F32)],
        compiler_params=_cparams(("arbitrary",)),
        name="s5_scan",
    )(u.reshape(B, L, W), *consts)


def _s5_params(a_re, a_im, log_dt, b_re, b_im, c_re, c_im, d):
    G, P = a_re.shape
    H = b_re.shape[-1]
    dt = jnp.exp(log_dt)[:, None]
    mag = jnp.exp(a_re * dt)
    abar_re = mag * jnp.cos(a_im * dt)
    abar_im = mag * jnp.sin(a_im * dt)
    den = a_re * a_re + a_im * a_im
    nr = abar_re - 1.0
    f_re = (nr * a_re + abar_im * a_im) / den
    f_im = (abar_im * a_re - nr * a_im) / den
    bbar_re = f_re[..., None] * b_re - f_im[..., None] * b_im
    bbar_im = f_re[..., None] * b_im + f_im[..., None] * b_re
    gpb = LANES // H
    nblk = G // gpb
    eye = jnp.eye(gpb, dtype=F32)

    def pack_in(bb):
        bb = bb.reshape(nblk, gpb, P, H)
        m = jnp.einsum("kgph,gf->kghfp", bb, eye)
        return m.reshape(nblk, gpb * H, gpb * P).astype(BF16)

    def pack_out(cc):
        cc = cc.reshape(nblk, gpb, H, P)
        m = jnp.einsum("kghp,gf->kgpfh", cc, eye)
        return m.reshape(nblk, gpb * P, gpb * H).astype(BF16)

    return {"are": abar_re.reshape(1, G * P), "aim": abar_im.reshape(1, G * P),
            "wbr": pack_in(bbar_re), "wbi": pack_in(bbar_im),
            "wcr": pack_out(c_re), "wci": pack_out(c_im), "d": d.reshape(1, G * H)}


def _attn_kernel(q_ref, k_ref, v_ref, o_ref, m_ref, l_ref, acc_ref):
    tq = q_ref.shape[1]
    qi = pl.program_id(2)
    scale = (MLA_NOPE + MLA_ROPE) ** -0.5
    nt = (((1,), (1,)), ((), ()))
    q = q_ref[0]
    m_ref[...] = jnp.full(m_ref.shape, -1e30, F32)
    l_ref[...] = jnp.zeros_like(l_ref)
    acc_ref[...] = jnp.zeros_like(acc_ref)

    def block(j, masked):
        r0 = pl.multiple_of(j * tq, tq)
        k = k_ref[0, pl.ds(r0, tq), :]
        v = v_ref[0, pl.ds(r0, tq), :]
        s = lax.dot_general(q, k, nt, preferred_element_type=F32) * scale
        if masked:
            row = lax.broadcasted_iota(jnp.int32, s.shape, 0)
            col = lax.broadcasted_iota(jnp.int32, s.shape, 1)
            s = jnp.where(col <= row, s, -jnp.inf)
        m_old = m_ref[...]
        m_new = jnp.maximum(m_old, jnp.max(s, axis=-1, keepdims=True))
        alpha = jnp.exp(m_old - m_new)
        p = jnp.exp(s - m_new)
        l_ref[...] = alpha * l_ref[...] + jnp.sum(p, axis=-1, keepdims=True)
        acc_ref[...] = alpha * acc_ref[...] + _dot(p.astype(BF16), v)
        m_ref[...] = m_new

    def body(j, c):
        block(j, False)
        return c
    lax.fori_loop(0, qi, body, 0)
    block(qi, True)
    o_ref[0] = (acc_ref[...] / l_ref[...]).astype(BF16)


def _attention(mq, mk, mv, B, L):
    tq = ATTN_TQ
    hp = MLA_HEAD_PAD
    return pl.pallas_call(
        _attn_kernel,
        grid=(B, MLA_HEADS, L // tq),
        in_specs=[pl.BlockSpec((1, tq, hp), lambda b, h, i: (b, i, h)),
                  pl.BlockSpec((1, L, hp), lambda b, h, i: (b, 0, h)),
                  pl.BlockSpec((1, L, MLA_DV), lambda b, h, i: (b, 0, h))],
        out_specs=pl.BlockSpec((1, tq, MLA_DV), lambda b, h, i: (b, i, h)),
        out_shape=jax.ShapeDtypeStruct((B, L, MLA_HEADS * MLA_DV), BF16),
        scratch_shapes=[pltpu.VMEM((tq, 1), F32), pltpu.VMEM((tq, 1), F32),
                        pltpu.VMEM((tq, MLA_DV), F32)],
        compiler_params=_cparams(("parallel", "parallel", "arbitrary")),
        name="mla_attention",
    )(mq.reshape(B, L, -1), mk.reshape(B, L, -1), mv.reshape(B, L, -1))


def _merge_kernel(x_ref, gin_ref, gout_ref, a_ref, y_ref, o_ref_in, wg_ref, wro_ref,
                  wga_ref, wgb_ref, wmo_ref, wout_ref, out_ref):
    x = x_ref[...]
    D = x.shape[1]
    h = _rms(x, gin_ref[...]).astype(BF16)
    y_ret = _dot(a_ref[...], wro_ref[...])
    merged = jax.nn.sigmoid(_dot(h, wg_ref[:, 0:D])) * y_ret
    ys = y_ref[...].astype(BF16)
    y_s5 = _dot(ys, wga_ref[...]) * jax.nn.sigmoid(_dot(ys, wgb_ref[...]))
    merged = merged + jax.nn.sigmoid(_dot(h, wg_ref[:, D:2 * D])) * y_s5
    y_mla = _dot(o_ref_in[...], wmo_ref[...])
    merged = merged + jax.nn.sigmoid(_dot(h, wg_ref[:, 2 * D:3 * D])) * y_mla
    out = _dot(merged.astype(BF16), wout_ref[...])
    out_ref[...] = x + _rms(out, gout_ref[...])


def _merge(x2, gin, gout, a_ret, y_s5, o_mla, w):
    T, D = x2.shape
    tm = ROW_TILE
    row = lambda n: pl.BlockSpec((tm, n), lambda i: (i, 0))
    consts = [w["wgates"], w["ret_wo"], w["glu_a"], w["glu_b"], w["mla_wo"], w["w_out"]]
    return pl.pallas_call(
        _merge_kernel,
        grid=(T // tm,),
        in_specs=[row(D), _const_spec(gin.shape), _const_spec(gout.shape),
                  row(a_ret.shape[1]), row(y_s5.shape[1]), row(o_mla.shape[1])]
                 + [_const_spec(c.shape) for c in consts],
        out_specs=row(D),
        out_shape=jax.ShapeDtypeStruct((T, D), F32),
        compiler_params=_cparams(("parallel",)),
        name="gated_merge",
    )(x2, gin, gout, a_ret, y_s5, o_mla, *consts)


def _mixer_weights(w_in, q_norm, kv_norm, w_uq, w_ukv):
    D = w_in.shape[0]
    qk_w = RET_HEADS * RET_DK
    v_w = RET_HEADS * RET_DV
    q_rank = q_norm.shape[0]
    kv_rank = kv_norm.shape[0]
    s5_w = w_in.shape[1] - (2 * qk_w + 2 * v_w + q_rank + kv_rank + MLA_ROPE + 3 * D)
    o = 2 * qk_w + 2 * v_w
    wr = w_in[:, :o]
    wu = w_in[:, o:o + s5_w]
    o += s5_w
    m_w = q_rank + kv_rank + MLA_ROPE
    wm = jnp.pad(w_in[:, o:o + m_w], ((0, 0), (0, LANES - MLA_ROPE)))
    o += m_w
    wgates = w_in[:, o:]
    hd_w = MLA_NOPE + MLA_ROPE
    wuq = jnp.pad(w_uq.reshape(q_rank, MLA_HEADS, hd_w),
                  ((0, 0), (0, 0), (0, MLA_HEAD_PAD - hd_w))).reshape(q_rank, MLA_HEADS * MLA_HEAD_PAD)
    wukv = w_ukv.reshape(kv_rank, MLA_HEADS, MLA_NOPE + MLA_DV)
    wuk = wukv[:, :, :MLA_NOPE].reshape(kv_rank, MLA_HEADS * MLA_NOPE)
    wuv = wukv[:, :, MLA_NOPE:].reshape(kv_rank, MLA_HEADS * MLA_DV)
    proj = {"wr": wr.astype(BF16), "wu": wu.astype(BF16), "wm": wm.astype(BF16),
            "qn": q_norm.reshape(1, -1), "kvn": kv_norm.reshape(1, -1),
            "wuq": wuq.astype(BF16), "wuk": wuk.astype(BF16), "wuv": wuv.astype(BF16)}
    return proj, wgates.astype(BF16)


def kernel(x, positions, norm_gains, ffn_w_gate, ffn_w_up, ffn_w_down, w_in, ret_w_o,
           s5_a_re, s5_a_im, s5_log_dt, s5_b_re, s5_b_im, s5_c_re, s5_c_im, s5_d,
           s5_glu_a, s5_glu_b, mla_q_norm, mla_kv_norm, mla_w_uq, mla_w_ukv, mla_w_o, w_out):
    B, L, D = x.shape
    depth = norm_gains.shape[0]
    assert L % ROW_TILE == 0 and L % RET_ROWS == 0 and L % ATTN_TQ == 0 and L % S5_TL == 0
    assert B == SUBLANES, "the S5 scan keeps the batch on the sublane axis"

    tabs = _rope_tables(positions)
    ret_tabs = _ret_tables()
    x2 = x.reshape(B * L, D)
    for l in range(depth):
        gains = [norm_gains[l, i].reshape(1, D) for i in range(norm_gains.shape[1])]
        x2 = _ffn(x2, gains[0], gains[1], ffn_w_gate[l, 0].astype(BF16),
                  ffn_w_up[l, 0].astype(BF16), ffn_w_down[l, 0].astype(BF16))
        pw, wgates = _mixer_weights(w_in[l], mla_q_norm[l], mla_kv_norm[l], mla_w_uq[l], mla_w_ukv[l])
        rq, rk, rv, rg, u, mq, mk, mv = _proj(x2, gains[2], pw, tabs)
        a_ret = _retention(rq, rk, rv, rg, ret_tabs, B, L)
        s5w = _s5_params(s5_a_re[l], s5_a_im[l], s5_log_dt[l], s5_b_re[l], s5_b_im[l],
                         s5_c_re[l], s5_c_im[l], s5_d[l])
        y_s5 = _s5(u, s5w, B, L)
        o_mla = _attention(mq, mk, mv, B, L)
        mw = {"wgates": wgates, "ret_wo": ret_w_o[l].astype(BF16),
              "glu_a": s5_glu_a[l].astype(BF16), "glu_b": s5_glu_b[l].astype(BF16),
              "mla_wo": mla_w_o[l].astype(BF16), "w_out": w_out[l].astype(BF16)}
        x2 = _merge(x2, gains[2], gains[3], a_ret.reshape(B * L, -1), y_s5.reshape(B * L, -1),
                    o_mla.reshape(B * L, -1), mw)
        x2 = _ffn(x2, gains[4], gains[5], ffn_w_gate[l, 1].astype(BF16),
                  ffn_w_up[l, 1].astype(BF16), ffn_w_down[l, 1].astype(BF16))
    return x2.reshape(B, L, D)
```

```python
import functools
import math

import jax
import jax.numpy as jnp
from jax import lax
from jax.experimental import pallas as pl
from jax.experimental.pallas import tpu as pltpu

F32 = jnp.float32
BF16 = jnp.bfloat16

ROPE_BASE = 10000.0
NORM_EPS = 1e-6
GN_EPS = 1e-5
FFN_RES = 0.5
RET_HEADS = 4
RET_DK = 128
RET_DV = 128
RET_CHUNK = 128
S5_GROUP = 16
S5_STATE = 64
MLA_HEADS = 4
MLA_NOPE = 128
MLA_ROPE = 64
MLA_DV = 128
MLA_HEAD_PAD = 256

LANES = 128
SUBLANES = 8
VMEM_LIMIT = 56 * 1024 * 1024

ROW_TILE = 512
FFN_CHUNKS = ((0, 1024), (1024, 2048), (2048, 2816))
RET_ROWS = 512
S5_TL = 64
S5_HALF = 1024
ATTN_TQ = 512
ATTN_HEADS_PER_STEP = 2


def _cparams(sem):
    return pltpu.CompilerParams(dimension_semantics=sem, vmem_limit_bytes=VMEM_LIMIT)


def _const_spec(shape):
    nd = len(shape)
    return pl.BlockSpec(shape, lambda *_: (0,) * nd, pipeline_mode=pl.Buffered(1))


def _rms(x, g):
    y = x * lax.rsqrt(jnp.mean(x * x, axis=-1, keepdims=True) + NORM_EPS)
    return y * g


def _dot(a, b):
    return jnp.dot(a, b, preferred_element_type=F32)


def _rope_kernel(pos_ref, fr_ref, sgr_ref, fm_ref, mc_ref, m1_ref, m2_ref,
                 cr_ref, sr_ref, cm_ref, s1_ref, s2_ref):
    pos = pos_ref[...]
    ang_r = pos * fr_ref[...]
    cr_ref[...] = jnp.cos(ang_r)
    sr_ref[...] = jnp.sin(ang_r) * sgr_ref[...]
    ang_m = pos * fm_ref[...]
    sin_m = jnp.sin(ang_m)
    cm_ref[...] = jnp.cos(ang_m) * mc_ref[...]
    s1_ref[...] = sin_m * m1_ref[...]
    s2_ref[...] = sin_m * m2_ref[...]


def _rope_tables(positions):
    B, L = positions.shape
    T = B * L
    pos = positions.astype(F32).reshape(T, 1)
    inv_r = ROPE_BASE ** (-jnp.arange(0, RET_DK, 2, dtype=F32) / RET_DK)
    inv_m = ROPE_BASE ** (-jnp.arange(0, MLA_ROPE, 2, dtype=F32) / MLA_ROPE)
    hm = MLA_ROPE // 2
    one_r = jnp.ones((RET_DK // 2,), F32)
    one_m = jnp.ones((hm,), F32)
    zero_m = jnp.zeros((hm,), F32)
    pad = jnp.zeros((LANES - MLA_ROPE,), F32)
    rows = [
        jnp.concatenate([inv_r, inv_r]),
        jnp.concatenate([-one_r, one_r]),
        jnp.concatenate([inv_m, inv_m, pad]),
        jnp.concatenate([one_m, one_m, pad]),
        jnp.concatenate([-one_m, zero_m, pad]),
        jnp.concatenate([zero_m, one_m, pad]),
    ]
    rows = [r.reshape(1, LANES) for r in rows]
    tm = 2048
    row_spec = pl.BlockSpec((1, LANES), lambda i: (0, 0))
    tab_spec = pl.BlockSpec((tm, LANES), lambda i: (i, 0))
    tab_shape = jax.ShapeDtypeStruct((T, LANES), F32)
    return pl.pallas_call(
        _rope_kernel,
        grid=(T // tm,),
        in_specs=[pl.BlockSpec((tm, 1), lambda i: (i, 0))] + [row_spec] * 6,
        out_specs=[tab_spec] * 5,
        out_shape=[tab_shape] * 5,
        compiler_params=_cparams(("parallel",)),
        name="rope_tables",
    )(pos, *rows)


def _ffn_kernel(x_ref, gpre_ref, gpost_ref, wg_ref, wu_ref, wd_ref, o_ref):
    x = x_ref[...]
    h = _rms(x, gpre_ref[...]).astype(BF16)
    y = None
    for c0, c1 in FFN_CHUNKS:
        g = _dot(h, wg_ref[:, c0:c1])
        u = _dot(h, wu_ref[:, c0:c1])
        a = (g * jax.nn.sigmoid(g) * u).astype(BF16)
        part = _dot(a, wd_ref[c0:c1, :])
        y = part if y is None else y + part
    o_ref[...] = x + FFN_RES * _rms(y, gpost_ref[...])


def _ffn(x2, gpre, gpost, wg, wu, wd):
    T, D = x2.shape
    tm = ROW_TILE
    row = pl.BlockSpec((tm, D), lambda i: (i, 0))
    return pl.pallas_call(
        _ffn_kernel,
        grid=(T // tm,),
        in_specs=[row, _const_spec(gpre.shape), _const_spec(gpost.shape),
                  _const_spec(wg.shape), _const_spec(wu.shape), _const_spec(wd.shape)],
        out_specs=row,
        out_shape=jax.ShapeDtypeStruct((T, D), F32),
        compiler_params=_cparams(("parallel",)),
        name="ffn",
    )(x2, gpre, gpost, wg, wu, wd)


def _rope_ret(v, cos, sin):
    return v * cos + pltpu.roll(v, RET_DK // 2, 1) * sin


def _rope_mla(v, cos, sin1, sin2):
    hm = MLA_ROPE // 2
    return v * cos + pltpu.roll(v, LANES - hm, 1) * sin1 + pltpu.roll(v, hm, 1) * sin2


def _proj_kernel(x_ref, g_ref, wr_ref, wu_ref, wm_ref, qn_ref, kvn_ref,
                 wuq_ref, wuk_ref, wuv_ref, cr_ref, sr_ref, cm_ref, s1_ref, s2_ref,
                 rq_ref, rk_ref, rv_ref, rg_ref, u_ref, mq_ref, mk_ref, mv_ref):
    h = _rms(x_ref[...], g_ref[...]).astype(BF16)
    cr = cr_ref[...]
    sr = sr_ref[...]
    qk_w = RET_HEADS * RET_DK
    v_w = RET_HEADS * RET_DV
    r = _dot(h, wr_ref[...])
    for hd in range(RET_HEADS):
        lo = hd * RET_DK
        q = r[:, lo:lo + RET_DK]
        k = r[:, qk_w + lo:qk_w + lo + RET_DK]
        rq_ref[:, lo:lo + RET_DK] = _rope_ret(q, cr, sr).astype(BF16)
        rk_ref[:, lo:lo + RET_DK] = (_rope_ret(k, cr, sr) * (RET_DK ** -0.5)).astype(BF16)
    rv_ref[...] = r[:, 2 * qk_w:2 * qk_w + v_w].astype(BF16)
    rg_ref[...] = r[:, 2 * qk_w + v_w:].astype(BF16)

    u_ref[...] = _dot(h, wu_ref[...])

    m = _dot(h, wm_ref[...])
    q_rank = qn_ref.shape[1]
    kv_rank = kvn_ref.shape[1]
    cq = _rms(m[:, :q_rank], qn_ref[...]).astype(BF16)
    ckv = _rms(m[:, q_rank:q_rank + kv_rank], kvn_ref[...]).astype(BF16)
    cm = cm_ref[...]
    s1 = s1_ref[...]
    s2 = s2_ref[...]
    kpe = _rope_mla(m[:, q_rank + kv_rank:], cm, s1, s2).astype(BF16)
    q = _dot(cq, wuq_ref[...])
    kn = _dot(ckv, wuk_ref[...])
    mv_ref[...] = _dot(ckv, wuv_ref[...]).astype(BF16)
    for hd in range(MLA_HEADS):
        lo = hd * MLA_HEAD_PAD
        mq_ref[:, lo:lo + MLA_NOPE] = q[:, lo:lo + MLA_NOPE].astype(BF16)
        mq_ref[:, lo + MLA_NOPE:lo + MLA_HEAD_PAD] = _rope_mla(
            q[:, lo + MLA_NOPE:lo + MLA_HEAD_PAD], cm, s1, s2).astype(BF16)
        mk_ref[:, lo:lo + MLA_NOPE] = kn[:, hd * MLA_NOPE:(hd + 1) * MLA_NOPE].astype(BF16)
        mk_ref[:, lo + MLA_NOPE:lo + MLA_HEAD_PAD] = kpe


def _proj(x2, gain, w, tabs):
    T, D = x2.shape
    tm = ROW_TILE
    row = lambda n: pl.BlockSpec((tm, n), lambda i: (i, 0))
    consts = [gain, w["wr"], w["wu"], w["wm"], w["qn"], w["kvn"], w["wuq"], w["wuk"], w["wuv"]]
    out_w = [(512, BF16), (512, BF16), (512, BF16), (512, BF16), (512, F32),
             (MLA_HEADS * MLA_HEAD_PAD, BF16), (MLA_HEADS * MLA_HEAD_PAD, BF16),
             (MLA_HEADS * MLA_DV, BF16)]
    return pl.pallas_call(
        _proj_kernel,
        grid=(T // tm,),
        in_specs=[row(D)] + [_const_spec(c.shape) for c in consts] + [row(LANES)] * 5,
        out_specs=[row(n) for n, _ in out_w],
        out_shape=[jax.ShapeDtypeStruct((T, n), dt) for n, dt in out_w],
        compiler_params=_cparams(("parallel",)),
        name="mixer_proj",
    )(x2, *consts, *tabs)


def _ret_kernel(q_ref, k_ref, v_ref, g_ref, intra_ref, qd_ref, kd_ref, cd_ref, o_ref, st_ref):
    @pl.when(pl.program_id(1) == 0)
    def _():
        st_ref[...] = jnp.zeros_like(st_ref)

    C = RET_CHUNK
    nt = (((1,), (1,)), ((), ()))
    tn = (((0,), (0,)), ((), ()))
    for hd in range(RET_HEADS):
        cols = slice(hd * RET_DK, (hd + 1) * RET_DK)
        vcols = slice(hd * RET_DV, (hd + 1) * RET_DV)
        intra = intra_ref[hd]
        qd = qd_ref[hd]
        kd = kd_ref[hd]
        cd = cd_ref[hd]
        state = st_ref[hd]
        for c in range(RET_ROWS // C):
            rows = slice(c * C, (c + 1) * C)
            q = q_ref[0, rows, cols]
            k = k_ref[0, rows, cols]
            v = v_ref[0, rows, vcols]
            scores = lax.dot_general(q, k, nt, preferred_element_type=F32) * intra
            inner = _dot(scores.astype(BF16), v)
            cross = _dot((q.astype(F32) * qd).astype(BF16), state.astype(BF16))
            kv = lax.dot_general((k.astype(F32) * kd).astype(BF16), v, tn,
                                 preferred_element_type=F32)
            state = cd * state + kv
            o = inner + cross
            mu = jnp.mean(o, axis=-1, keepdims=True)
            var = jnp.mean(jnp.square(o - mu), axis=-1, keepdims=True)
            on = (o - mu) * lax.rsqrt(var + GN_EPS)
            g = g_ref[0, rows, vcols].astype(F32)
            o_ref[0, rows, vcols] = (g * jax.nn.sigmoid(g) * on).astype(BF16)
        st_ref[hd] = state


def _ret_tables():
    C = RET_CHUNK
    log_gamma = jnp.log1p(-jnp.exp2(-5.0 - jnp.arange(RET_HEADS, dtype=F32)))
    pos = jnp.arange(C, dtype=F32)
    rel = pos[:, None] - pos[None, :]
    intra = jnp.where(rel[None] >= 0.0,
                      jnp.exp(jnp.maximum(rel, 0.0)[None] * log_gamma[:, None, None]), 0.0)
    k_decay = jnp.exp((C - 1.0 - pos)[:, None] * log_gamma[None, :])
    q_decay = jnp.exp((pos + 1.0)[:, None] * log_gamma[None, :])
    chunk_decay = jnp.exp(C * log_gamma)
    qd = jnp.broadcast_to(q_decay.T[:, :, None], (RET_HEADS, C, RET_DK))
    kd = jnp.broadcast_to(k_decay.T[:, :, None], (RET_HEADS, C, RET_DK))
    cd = jnp.broadcast_to(chunk_decay[:, None, None], (RET_HEADS, 1, RET_DV))
    return intra, qd, kd, cd


def _retention(rq, rk, rv, rg, tables, B, L):
    W = rq.shape[-1]
    shp = (B, L, W)
    blk = pl.BlockSpec((1, RET_ROWS, W), lambda b, i: (b, i, 0))
    return pl.pallas_call(
        _ret_kernel,
        grid=(B, L // RET_ROWS),
        in_specs=[blk] * 4 + [_const_spec(t.shape) for t in tables],
        out_specs=blk,
        out_shape=jax.ShapeDtypeStruct(shp, BF16),
        scratch_shapes=[pltpu.VMEM((RET_HEADS, RET_DK, RET_DV), F32)],
        compiler_params=_cparams(("parallel", "arbitrary")),
        name="retention",
    )(rq.reshape(shp), rk.reshape(shp), rv.reshape(shp), rg.reshape(shp), *tables)


def _s5_kernel(u_ref, are_ref, aim_ref, wbr_ref, wbi_ref, wcr_ref, wci_ref, d_ref, y_ref,
               sre_ref, sim_ref, u2_ref, ut_ref, bre_ref, bim_ref, yt_ref):
    B, tl, W = u_ref.shape
    nblk = wbr_ref.shape[0]
    in_w = wbr_ref.shape[1]
    st_w = wbr_ref.shape[2]

    @pl.when(pl.program_id(0) == 0)
    def _():
        sre_ref[...] = jnp.zeros_like(sre_ref)
        sim_ref[...] = jnp.zeros_like(sim_ref)

    nlb = W // LANES
    for b in range(B):
        for j in range(nlb):
            u2_ref[j, b * tl:(b + 1) * tl, :] = u_ref[b, :, j * LANES:(j + 1) * LANES]

    def gather(t, c):
        r0 = pl.multiple_of(t * B, B)
        for j in range(nlb):
            ut_ref[pl.ds(r0, B), j * LANES:(j + 1) * LANES] = u2_ref[j, pl.ds(t, B, stride=tl), :]
        return c
    lax.fori_loop(0, tl, gather, 0, unroll=8)

    for kb in range(nblk):
        lhs = ut_ref[:, kb * in_w:(kb + 1) * in_w].astype(BF16)
        bre_ref[:, kb * st_w:(kb + 1) * st_w] = _dot(lhs, wbr_ref[kb])
        bim_ref[:, kb * st_w:(kb + 1) * st_w] = _dot(lhs, wbi_ref[kb])

    n_state = sre_ref.shape[1]
    for hh in range(n_state // S5_HALF):
        lanes = slice(hh * S5_HALF, (hh + 1) * S5_HALF)
        ar = jnp.broadcast_to(are_ref[:, lanes], (B, S5_HALF))
        ai = jnp.broadcast_to(aim_ref[:, lanes], (B, S5_HALF))

        def step(t, carry, lanes=lanes, ar=ar, ai=ai):
            sr, si = carry
            r0 = pl.multiple_of(t * B, B)
            nsr = ar * sr - ai * si + bre_ref[pl.ds(r0, B), lanes]
            nsi = ar * si + ai * sr + bim_ref[pl.ds(r0, B), lanes]
            bre_ref[pl.ds(r0, B), lanes] = nsr
            bim_ref[pl.ds(r0, B), lanes] = nsi
            return nsr, nsi
        sr, si = lax.fori_loop(0, tl, step, (sre_ref[:, lanes], sim_ref[:, lanes]), unroll=4)
        sre_ref[:, lanes] = sr
        sim_ref[:, lanes] = si

    for kb in range(nblk):
        sl = slice(kb * st_w, (kb + 1) * st_w)
        ch = slice(kb * in_w, (kb + 1) * in_w)
        y = (_dot(bre_ref[:, sl].astype(BF16), wcr_ref[kb])
             - _dot(bim_ref[:, sl].astype(BF16), wci_ref[kb]))
        yt_ref[kb] = jax.nn.gelu(y + d_ref[:, ch] * ut_ref[:, ch])

    for b in range(B):
        for j in range(nlb):
            y_ref[b, :, j * LANES:(j + 1) * LANES] = yt_ref[j, pl.ds(b, tl, stride=B), :]


def _s5(u, w, B, L):
    W = u.shape[-1]
    tl = S5_TL
    n_state = w["are"].shape[1]
    blk = pl.BlockSpec((B, tl, W), lambda i: (0, i, 0))
    consts = [w["are"], w["aim"], w["wbr"], w["wbi"], w["wcr"], w["wci"], w["d"]]
    return pl.pallas_call(
        _s5_kernel,
        grid=(L // tl,),
        in_specs=[blk] + [_const_spec(c.shape) for c in consts],
        out_specs=blk,
        out_shape=jax.ShapeDtypeStruct((B, L, W), F32),
        scratch_shapes=[pltpu.VMEM((B, n_state), F32), pltpu.VMEM((B, n_state), F32),
                        pltpu.VMEM((W // LANES, B * tl, LANES), F32), pltpu.VMEM((B * tl, W), F32),
                        pltpu.VMEM((B * tl, n_state), F32), pltpu.VMEM((B * tl, n_state), F32),
                        pltpu.VMEM((W // LANES, B * tl, LANES), F32)],
        compiler_params=_cparams(("arbitrary",)),
        name="s5_scan",
    )(u.reshape(B, L, W), *consts)


def _s5_params(a_re, a_im, log_dt, b_re, b_im, c_re, c_im, d):
    G, P = a_re.shape
    H = b_re.shape[-1]
    dt = jnp.exp(log_dt)[:, None]
    mag = jnp.exp(a_re * dt)
    abar_re = mag * jnp.cos(a_im * dt)
    abar_im = mag * jnp.sin(a_im * dt)
    den = a_re * a_re + a_im * a_im
    nr = abar_re - 1.0
    f_re = (nr * a_re + abar_im * a_im) / den
    f_im = (abar_im * a_re - nr * a_im) / den
    bbar_re = f_re[..., None] * b_re - f_im[..., None] * b_im
    bbar_im = f_re[..., None] * b_im + f_im[..., None] * b_re
    gpb = LANES // H
    nblk = G // gpb
    eye = jnp.eye(gpb, dtype=F32)

    def pack_in(bb):
        bb = bb.reshape(nblk, gpb, P, H)
        m = jnp.einsum("kgph,gf->kghfp", bb, eye)
        return m.reshape(nblk, gpb * H, gpb * P).astype(BF16)

    def pack_out(cc):
        cc = cc.reshape(nblk, gpb, H, P)
        m = jnp.einsum("kghp,gf->kgpfh", cc, eye)
        return m.reshape(nblk, gpb * P, gpb * H).astype(BF16)

    return {"are": abar_re.reshape(1, G * P), "aim": abar_im.reshape(1, G * P),
            "wbr": pack_in(bbar_re), "wbi": pack_in(bbar_im),
            "wcr": pack_out(c_re), "wci": pack_out(c_im), "d": d.reshape(1, G * H)}


def _attn_kernel(q_ref, k_ref, v_ref, o_ref, s_ref, m_ref, mprev_ref, l_ref, acc_ref):
    tq = q_ref.shape[1]
    assert m_ref.shape[0] == 2
    qi = pl.program_id(2)
    c = (MLA_NOPE + MLA_ROPE) ** -0.5 * math.log2(math.e)
    nt = (((1,), (1,)), ((), ()))
    tn = (((0,), (0,)), ((), ()))
    m_ref[...] = jnp.full(m_ref.shape, -1e30, F32)
    l_ref[...] = jnp.zeros_like(l_ref)
    acc_ref[...] = jnp.zeros_like(acc_ref)

    def scores(j, h, masked):
        r0 = pl.multiple_of(j * tq, tq)
        qk = slice(h * MLA_HEAD_PAD, (h + 1) * MLA_HEAD_PAD)
        st = lax.dot_general(k_ref[0, pl.ds(r0, tq), qk], q_ref[0, :, qk], nt,
                             preferred_element_type=F32)
        if masked:
            kidx = lax.broadcasted_iota(jnp.int32, st.shape, 0)
            qidx = lax.broadcasted_iota(jnp.int32, st.shape, 1)
            st = jnp.where(kidx <= qidx, st, -jnp.inf)
        s_ref[h] = st
        m_old = m_ref[h]
        mprev_ref[h] = m_old
        m_ref[h] = jnp.maximum(m_old, jnp.max(st, axis=0, keepdims=True))

    def accumulate(j, h):
        r0 = pl.multiple_of(j * tq, tq)
        v = v_ref[0, pl.ds(r0, tq), h * MLA_DV:(h + 1) * MLA_DV]
        m_new = m_ref[h]
        alpha = jnp.exp2((mprev_ref[h] - m_new) * c)
        pt = jnp.exp2((s_ref[h] - m_new) * c)
        l_ref[h] = alpha * l_ref[h] + jnp.sum(pt, axis=0, keepdims=True)
        acc_ref[h] = alpha * acc_ref[h] + lax.dot_general(
            v, pt.astype(BF16), tn, preferred_element_type=F32)

    scores(qi, 0, True)
    scores(qi, 1, True)
    accumulate(qi, 0)

    def body(j, carry):
        scores(j, 0, False)
        accumulate(jnp.where(j == 0, qi, j - 1), 1)
        scores(j, 1, False)
        accumulate(j, 0)
        return carry
    lax.fori_loop(0, qi, body, 0)
    accumulate(jnp.maximum(qi - 1, 0), 1)
    for h in range(2):
        o_ref[0, :, h * MLA_DV:(h + 1) * MLA_DV] = (acc_ref[h] / l_ref[h]).T.astype(BF16)


def _attention(mq, mk, mv, B, L):
    tq = ATTN_TQ
    nh = ATTN_HEADS_PER_STEP
    qw = nh * MLA_HEAD_PAD
    vw = nh * MLA_DV
    return pl.pallas_call(
        _attn_kernel,
        grid=(B, MLA_HEADS // nh, L // tq),
        in_specs=[pl.BlockSpec((1, tq, qw), lambda b, h, i: (b, i, h)),
                  pl.BlockSpec((1, L, qw), lambda b, h, i: (b, 0, h)),
                  pl.BlockSpec((1, L, vw), lambda b, h, i: (b, 0, h))],
        out_specs=pl.BlockSpec((1, tq, vw), lambda b, h, i: (b, i, h)),
        out_shape=jax.ShapeDtypeStruct((B, L, MLA_HEADS * MLA_DV), BF16),
        scratch_shapes=[pltpu.VMEM((nh, tq, tq), F32), pltpu.VMEM((nh, 1, tq), F32),
                        pltpu.VMEM((nh, 1, tq), F32), pltpu.VMEM((nh, 1, tq), F32),
                        pltpu.VMEM((nh, MLA_DV, tq), F32)],
        compiler_params=_cparams(("parallel", "parallel", "arbitrary")),
        name="mla_attention",
    )(mq.reshape(B, L, -1), mk.reshape(B, L, -1), mv.reshape(B, L, -1))


def _merge_kernel(x_ref, gin_ref, gout_ref, a_ref, y_ref, o_ref_in, wg_ref, wro_ref,
                  wga_ref, wgb_ref, wmo_ref, wout_ref, out_ref):
    x = x_ref[...]
    D = x.shape[1]
    h = _rms(x, gin_ref[...]).astype(BF16)
    y_ret = _dot(a_ref[...], wro_ref[...])
    merged = jax.nn.sigmoid(_dot(h, wg_ref[:, 0:D])) * y_ret
    ys = y_ref[...].astype(BF16)
    y_s5 = _dot(ys, wga_ref[...]) * jax.nn.sigmoid(_dot(ys, wgb_ref[...]))
    merged = merged + jax.nn.sigmoid(_dot(h, wg_ref[:, D:2 * D])) * y_s5
    y_mla = _dot(o_ref_in[...], wmo_ref[...])
    merged = merged + jax.nn.sigmoid(_dot(h, wg_ref[:, 2 * D:3 * D])) * y_mla
    out = _dot(merged.astype(BF16), wout_ref[...])
    out_ref[...] = x + _rms(out, gout_ref[...])


def _merge(x2, gin, gout, a_ret, y_s5, o_mla, w):
    T, D = x2.shape
    tm = ROW_TILE
    row = lambda n: pl.BlockSpec((tm, n), lambda i: (i, 0))
    consts = [w["wgates"], w["ret_wo"], w["glu_a"], w["glu_b"], w["mla_wo"], w["w_out"]]
    return pl.pallas_call(
        _merge_kernel,
        grid=(T // tm,),
        in_specs=[row(D), _const_spec(gin.shape), _const_spec(gout.shape),
                  row(a_ret.shape[1]), row(y_s5.shape[1]), row(o_mla.shape[1])]
                 + [_const_spec(c.shape) for c in consts],
        out_specs=row(D),
        out_shape=jax.ShapeDtypeStruct((T, D), F32),
        compiler_params=_cparams(("parallel",)),
        name="gated_merge",
    )(x2, gin, gout, a_ret, y_s5, o_mla, *consts)


def _mixer_weights(w_in, q_norm, kv_norm, w_uq, w_ukv):
    D = w_in.shape[0]
    qk_w = RET_HEADS * RET_DK
    v_w = RET_HEADS * RET_DV
    q_rank = q_norm.shape[0]
    kv_rank = kv_norm.shape[0]
    s5_w = w_in.shape[1] - (2 * qk_w + 2 * v_w + q_rank + kv_rank + MLA_ROPE + 3 * D)
    o = 2 * qk_w + 2 * v_w
    wr = w_in[:, :o]
    wu = w_in[:, o:o + s5_w]
    o += s5_w
    m_w = q_rank + kv_rank + MLA_ROPE
    wm = jnp.pad(w_in[:, o:o + m_w], ((0, 0), (0, LANES - MLA_ROPE)))
    o += m_w
    wgates = w_in[:, o:]
    hd_w = MLA_NOPE + MLA_ROPE
    wuq = jnp.pad(w_uq.reshape(q_rank, MLA_HEADS, hd_w),
                  ((0, 0), (0, 0), (0, MLA_HEAD_PAD - hd_w))).reshape(q_rank, MLA_HEADS * MLA_HEAD_PAD)
    wukv = w_ukv.reshape(kv_rank, MLA_HEADS, MLA_NOPE + MLA_DV)
    wuk = wukv[:, :, :MLA_NOPE].reshape(kv_rank, MLA_HEADS * MLA_NOPE)
    wuv = wukv[:, :, MLA_NOPE:].reshape(kv_rank, MLA_HEADS * MLA_DV)
    proj = {"wr": wr.astype(BF16), "wu": wu.astype(BF16), "wm": wm.astype(BF16),
            "qn": q_norm.reshape(1, -1), "kvn": kv_norm.reshape(1, -1),
            "wuq": wuq.astype(BF16), "wuk": wuk.astype(BF16), "wuv": wuv.astype(BF16)}
    return proj, wgates.astype(BF16)


def kernel(x, positions, norm_gains, ffn_w_gate, ffn_w_up, ffn_w_down, w_in, ret_w_o,
           s5_a_re, s5_a_im, s5_log_dt, s5_b_re, s5_b_im, s5_c_re, s5_c_im, s5_d,
           s5_glu_a, s5_glu_b, mla_q_norm, mla_kv_norm, mla_w_uq, mla_w_ukv, mla_w_o, w_out):
    B, L, D = x.shape
    depth = norm_gains.shape[0]
    assert L % ROW_TILE == 0 and L % RET_ROWS == 0 and L % ATTN_TQ == 0 and L % S5_TL == 0
    assert B == SUBLANES, "the S5 scan keeps the batch on the sublane axis"

    tabs = _rope_tables(positions)
    ret_tabs = _ret_tables()
    x2 = x.reshape(B * L, D)
    for l in range(depth):
        gains = [norm_gains[l, i].reshape(1, D) for i in range(norm_gains.shape[1])]
        x2 = _ffn(x2, gains[0], gains[1], ffn_w_gate[l, 0].astype(BF16),
                  ffn_w_up[l, 0].astype(BF16), ffn_w_down[l, 0].astype(BF16))
        pw, wgates = _mixer_weights(w_in[l], mla_q_norm[l], mla_kv_norm[l], mla_w_uq[l], mla_w_ukv[l])
        rq, rk, rv, rg, u, mq, mk, mv = _proj(x2, gains[2], pw, tabs)
        a_ret = _retention(rq, rk, rv, rg, ret_tabs, B, L)
        s5w = _s5_params(s5_a_re[l], s5_a_im[l], s5_log_dt[l], s5_b_re[l], s5_b_im[l],
                         s5_c_re[l], s5_c_im[l], s5_d[l])
        y_s5 = _s5(u, s5w, B, L)
        o_mla = _attention(mq, mk, mv, B, L)
        mw = {"wgates": wgates, "ret_wo": ret_w_o[l].astype(BF16),
              "glu_a": s5_glu_a[l].astype(BF16), "glu_b": s5_glu_b[l].astype(BF16),
              "mla_wo": mla_w_o[l].astype(BF16), "w_out": w_out[l].astype(BF16)}
        x2 = _merge(x2, gains[2], gains[3], a_ret.reshape(B * L, -1), y_s5.reshape(B * L, -1),
                    o_mla.reshape(B * L, -1), mw)
        x2 = _ffn(x2, gains[4], gains[5], ffn_w_gate[l, 1].astype(BF16),
                  ffn_w_up[l, 1].astype(BF16), ffn_w_down[l, 1].astype(BF16))
    return x2.reshape(B, L, D)
```

```python
import math

import jax
import jax.numpy as jnp
from jax import lax
from jax.experimental import pallas as pl
from jax.experimental.pallas import tpu as pltpu

F32 = jnp.float32
BF16 = jnp.bfloat16

ROPE_BASE = 10000.0
NORM_EPS = 1e-6
GN_EPS = 1e-5
FFN_RES = 0.5
RET_HEADS = 4
RET_DK = 128
RET_DV = 128
RET_CHUNK = 128
MLA_HEADS = 4
MLA_NOPE = 128
MLA_ROPE = 64
MLA_DV = 128
MLA_HEAD_PAD = 256

LANES = 128
SUBLANES = 8
VMEM_LIMIT = 56 * 1024 * 1024

ROW_TILE = 512
FFN_CHUNKS = ((0, 1024), (1024, 2048), (2048, 2816))
RET_ROWS = 512
S5_TL = 64
S5_HALF = 1024
ATTN_TQ = 512
ATTN_HEADS_PER_STEP = 4


def _cparams(sem):
    return pltpu.CompilerParams(dimension_semantics=sem, vmem_limit_bytes=VMEM_LIMIT)


def _const_spec(shape):
    nd = len(shape)
    return pl.BlockSpec(shape, lambda *_: (0,) * nd, pipeline_mode=pl.Buffered(1))


def _slab_spec(arr, lead, block=None, col=0):
    tail = tuple(arr.shape[len(lead):]) if block is None else tuple(block)
    idx = tuple(lead) + (0,) * (len(tail) - 1) + (col,)
    return pl.BlockSpec((None,) * len(lead) + tail, lambda *_: idx,
                        pipeline_mode=pl.Buffered(1))


def _rms(x, g):
    y = x * lax.rsqrt(jnp.mean(x * x, axis=-1, keepdims=True) + NORM_EPS)
    return y * g


def _dot(a, b):
    return jnp.dot(a, b, preferred_element_type=F32)


def _rope_kernel(pos_ref, f_ref, a_ref, na_ref, b_ref, p_ref, np_ref, q_ref,
                 cr_ref, sr_ref, cm_ref, s1_ref, s2_ref):
    ang = pos_ref[...] * f_ref[...]
    c = jnp.cos(ang)
    s = jnp.sin(ang)
    hr = RET_DK // 2
    c_hi = pltpu.roll(c, hr, 1)
    s_hi = pltpu.roll(s, hr, 1)
    c_m2 = pltpu.roll(c, hr + MLA_ROPE // 2, 1)
    s_m2 = pltpu.roll(s, hr + MLA_ROPE // 2, 1)
    cr_ref[...] = c * a_ref[...] + c_hi * b_ref[...]
    sr_ref[...] = s * na_ref[...] + s_hi * b_ref[...]
    cm_ref[...] = c_hi * p_ref[...] + c_m2 * q_ref[...]
    s1_ref[...] = s_hi * np_ref[...]
    s2_ref[...] = s_m2 * q_ref[...]


def _rope_tables(positions):
    B, L = positions.shape
    T = B * L
    pos = positions.astype(F32).reshape(T, 1)
    inv_r = ROPE_BASE ** (-jnp.arange(0, RET_DK, 2, dtype=F32) / RET_DK)
    inv_m = ROPE_BASE ** (-jnp.arange(0, MLA_ROPE, 2, dtype=F32) / MLA_ROPE)
    hr = RET_DK // 2
    hm = MLA_ROPE // 2
    lane = jnp.arange(LANES)
    freq = jnp.concatenate([inv_r, inv_m, jnp.zeros((LANES - hr - hm,), F32)])
    a = (lane < hr).astype(F32)
    b = (lane >= hr).astype(F32)
    p = (lane < hm).astype(F32)
    q = ((lane >= hm) & (lane < 2 * hm)).astype(F32)
    rows = [r.reshape(1, LANES) for r in (freq, a, -a, b, p, -p, q)]
    tm = 2048
    row_spec = pl.BlockSpec((1, LANES), lambda i: (0, 0))
    tab_spec = pl.BlockSpec((tm, LANES), lambda i: (i, 0))
    tab_shape = jax.ShapeDtypeStruct((T, LANES), F32)
    return pl.pallas_call(
        _rope_kernel,
        grid=(T // tm,),
        in_specs=[pl.BlockSpec((tm, 1), lambda i: (i, 0))] + [row_spec] * len(rows),
        out_specs=[tab_spec] * 5,
        out_shape=[tab_shape] * 5,
        compiler_params=_cparams(("parallel",)),
        name="rope_tables",
    )(pos, *rows)


def _ffn_kernel(x_ref, gpre_ref, gpost_ref, wg_ref, wu_ref, wd_ref, o_ref):
    x = x_ref[...]
    h = _rms(x, gpre_ref[...]).astype(BF16)
    y = None
    pending = None
    for c0, c1 in FFN_CHUNKS + ((None, None),):
        if c0 is not None:
            g = _dot(h, wg_ref[:, c0:c1])
            u = _dot(h, wu_ref[:, c0:c1])
        if pending is not None:
            a, p0, p1 = pending
            part = _dot(a, wd_ref[p0:p1, :])
            y = part if y is None else y + part
        if c0 is not None:
            pending = ((g * jax.nn.sigmoid(g) * u).astype(BF16), c0, c1)
    o_ref[...] = x + FFN_RES * _rms(y, gpost_ref[...])


def _ffn(x2, p, l, k):
    T, D = x2.shape
    tm = ROW_TILE
    row = pl.BlockSpec((tm, D), lambda i: (i, 0))
    n_sub = p["gains"].shape[0] // p["ffn_wg"].shape[0]
    gi = l * n_sub + 4 * k
    return pl.pallas_call(
        _ffn_kernel,
        grid=(T // tm,),
        in_specs=[row, _slab_spec(p["gains"], (gi,)), _slab_spec(p["gains"], (gi + 1,)),
                  _slab_spec(p["ffn_wg"], (l, k)), _slab_spec(p["ffn_wu"], (l, k)),
                  _slab_spec(p["ffn_wd"], (l, k))],
        out_specs=row,
        out_shape=jax.ShapeDtypeStruct((T, D), F32),
        compiler_params=_cparams(("parallel",)),
        name="ffn",
    )(x2, p["gains"], p["gains"], p["ffn_wg"], p["ffn_wu"], p["ffn_wd"])


def _rope_ret(v, cos, sin):
    return v * cos + pltpu.roll(v, RET_DK // 2, 1) * sin


def _rope_mla(v, cos, sin1, sin2):
    hm = MLA_ROPE // 2
    return v * cos + pltpu.roll(v, LANES - hm, 1) * sin1 + pltpu.roll(v, hm, 1) * sin2


def _proj_kernel(x_ref, g_ref, wr_ref, wu_ref, wm_ref, qn_ref, kvn_ref,
                 wuq_ref, wuk_ref, wuv_ref, cr_ref, sr_ref, cm_ref, s1_ref, s2_ref,
                 rq_ref, rk_ref, rv_ref, rg_ref, u_ref, mq_ref, mk_ref, mv_ref):
    h = _rms(x_ref[...], g_ref[...]).astype(BF16)
    cr = cr_ref[...]
    sr = sr_ref[...]
    qk_w = RET_HEADS * RET_DK
    v_w = RET_HEADS * RET_DV
    r = _dot(h, wr_ref[...])
    for hd in range(RET_HEADS):
        lo = hd * RET_DK
        q = r[:, lo:lo + RET_DK]
        k = r[:, qk_w + lo:qk_w + lo + RET_DK]
        rq_ref[:, lo:lo + RET_DK] = _rope_ret(q, cr, sr).astype(BF16)
        rk_ref[:, lo:lo + RET_DK] = (_rope_ret(k, cr, sr) * (RET_DK ** -0.5)).astype(BF16)
    rv_ref[...] = r[:, 2 * qk_w:2 * qk_w + v_w].astype(BF16)
    rg_ref[...] = r[:, 2 * qk_w + v_w:].astype(BF16)

    u_ref[...] = _dot(h, wu_ref[...])

    m = _dot(h, wm_ref[...])
    q_rank = qn_ref.shape[1]
    kv_rank = kvn_ref.shape[1]
    cq = _rms(m[:, :q_rank], qn_ref[...]).astype(BF16)
    ckv = _rms(m[:, q_rank:q_rank + kv_rank], kvn_ref[...]).astype(BF16)
    cm = cm_ref[...]
    s1 = s1_ref[...]
    s2 = s2_ref[...]
    kpe = _rope_mla(m[:, q_rank + kv_rank:], cm, s1, s2).astype(BF16)
    q = _dot(cq, wuq_ref[...])
    kn = _dot(ckv, wuk_ref[...])
    mv_ref[...] = _dot(ckv, wuv_ref[...]).astype(BF16)
    for hd in range(MLA_HEADS):
        lo = hd * MLA_HEAD_PAD
        mq_ref[:, lo:lo + MLA_NOPE] = q[:, lo:lo + MLA_NOPE].astype(BF16)
        mq_ref[:, lo + MLA_NOPE:lo + MLA_HEAD_PAD] = _rope_mla(
            q[:, lo + MLA_NOPE:lo + MLA_HEAD_PAD], cm, s1, s2).astype(BF16)
        mk_ref[:, lo:lo + MLA_NOPE] = kn[:, hd * MLA_NOPE:(hd + 1) * MLA_NOPE].astype(BF16)
        mk_ref[:, lo + MLA_NOPE:lo + MLA_HEAD_PAD] = kpe


def _proj(x2, p, l, tabs):
    T, D = x2.shape
    tm = ROW_TILE
    row = lambda n: pl.BlockSpec((tm, n), lambda i: (i, 0))
    n_sub = p["gains"].shape[0] // p["w_in"].shape[0]
    ret_w = 2 * RET_HEADS * RET_DK + 2 * RET_HEADS * RET_DV
    s5_w = p["s5_d"].shape[-1]
    assert ret_w % s5_w == 0
    consts = [(p["gains"], _slab_spec(p["gains"], (l * n_sub + 2,))),
              (p["w_in"], _slab_spec(p["w_in"], (l,), block=(D, ret_w), col=0)),
              (p["w_in"], _slab_spec(p["w_in"], (l,), block=(D, s5_w), col=ret_w // s5_w)),
              (p["wm"], _slab_spec(p["wm"], (l,))),
              (p["qn"], _slab_spec(p["qn"], (l,))), (p["kvn"], _slab_spec(p["kvn"], (l,))),
              (p["wuq"], _slab_spec(p["wuq"], (l,))), (p["wuk"], _slab_spec(p["wuk"], (l,))),
              (p["wuv"], _slab_spec(p["wuv"], (l,)))]
    out_w = [(RET_HEADS * RET_DK, BF16), (RET_HEADS * RET_DK, BF16),
             (RET_HEADS * RET_DV, BF16), (RET_HEADS * RET_DV, BF16), (s5_w, F32),
             (MLA_HEADS * MLA_HEAD_PAD, BF16), (MLA_HEADS * MLA_HEAD_PAD, BF16),
             (MLA_HEADS * MLA_DV, BF16)]
    return pl.pallas_call(
        _proj_kernel,
        grid=(T // tm,),
        in_specs=[row(D)] + [s for _, s in consts] + [row(LANES)] * 5,
        out_specs=[row(n) for n, _ in out_w],
        out_shape=[jax.ShapeDtypeStruct((T, n), dt) for n, dt in out_w],
        compiler_params=_cparams(("parallel",)),
        name="mixer_proj",
    )(x2, *[a for a, _ in consts], *tabs)


def _ret_kernel(q_ref, k_ref, v_ref, g_ref, intra_ref, qd_ref, kd_ref, cd_ref, o_ref, st_ref):
    @pl.when(pl.program_id(1) == 0)
    def _():
        st_ref[...] = jnp.zeros_like(st_ref)

    C = RET_CHUNK
    nt = (((1,), (1,)), ((), ()))
    tn = (((0,), (0,)), ((), ()))
    for hd in range(RET_HEADS):
        cols = slice(hd * RET_DK, (hd + 1) * RET_DK)
        vcols = slice(hd * RET_DV, (hd + 1) * RET_DV)
        intra = intra_ref[hd]
        qd = qd_ref[hd]
        kd = kd_ref[hd]
        cd = cd_ref[hd]
        state = st_ref[hd]
        for c in range(RET_ROWS // C):
            rows = slice(c * C, (c + 1) * C)
            q = q_ref[0, rows, cols]
            k = k_ref[0, rows, cols]
            v = v_ref[0, rows, vcols]
            scores = lax.dot_general(q, k, nt, preferred_element_type=F32) * intra
            inner = _dot(scores.astype(BF16), v)
            cross = _dot((q.astype(F32) * qd).astype(BF16), state.astype(BF16))
            kv = lax.dot_general((k.astype(F32) * kd).astype(BF16), v, tn,
                                 preferred_element_type=F32)
            state = cd * state + kv
            o = inner + cross
            mu = jnp.mean(o, axis=-1, keepdims=True)
            var = jnp.mean(jnp.square(o - mu), axis=-1, keepdims=True)
            on = (o - mu) * lax.rsqrt(var + GN_EPS)
            g = g_ref[0, rows, vcols].astype(F32)
            o_ref[0, rows, vcols] = (g * jax.nn.sigmoid(g) * on).astype(BF16)
        st_ref[hd] = state


def _ret_tables():
    C = RET_CHUNK
    log_gamma = jnp.log1p(-jnp.exp2(-5.0 - jnp.arange(RET_HEADS, dtype=F32)))
    pos = jnp.arange(C, dtype=F32)
    rel = pos[:, None] - pos[None, :]
    intra = jnp.where(rel[None] >= 0.0,
                      jnp.exp(jnp.maximum(rel, 0.0)[None] * log_gamma[:, None, None]), 0.0)
    k_decay = jnp.exp((C - 1.0 - pos)[:, None] * log_gamma[None, :])
    q_decay = jnp.exp((pos + 1.0)[:, None] * log_gamma[None, :])
    chunk_decay = jnp.exp(C * log_gamma)
    qd = jnp.broadcast_to(q_decay.T[:, :, None], (RET_HEADS, C, RET_DK))
    kd = jnp.broadcast_to(k_decay.T[:, :, None], (RET_HEADS, C, RET_DK))
    cd = jnp.broadcast_to(chunk_decay[:, None, None], (RET_HEADS, 1, RET_DV))
    return intra, qd, kd, cd


def _retention(rq, rk, rv, rg, tables, B, L):
    W = rq.shape[-1]
    shp = (B, L, W)
    blk = pl.BlockSpec((1, RET_ROWS, W), lambda b, i: (b, i, 0))
    return pl.pallas_call(
        _ret_kernel,
        grid=(B, L // RET_ROWS),
        in_specs=[blk] * 4 + [_const_spec(t.shape) for t in tables],
        out_specs=blk,
        out_shape=jax.ShapeDtypeStruct(shp, BF16),
        scratch_shapes=[pltpu.VMEM((RET_HEADS, RET_DK, RET_DV), F32)],
        compiler_params=_cparams(("parallel", "arbitrary")),
        name="retention",
    )(rq.reshape(shp), rk.reshape(shp), rv.reshape(shp), rg.reshape(shp), *tables)


def _s5_zoh_kernel(are_ref, aim_ref, ldt_ref, bre_ref, bim_ref,
                   abr_ref, abi_ref, bbr_ref, bbi_ref):
    a_re = are_ref[...]
    a_im = aim_ref[...]
    dt = jnp.exp(ldt_ref[...])
    mag = jnp.exp(a_re * dt)
    abar_re = mag * jnp.cos(a_im * dt)
    abar_im = mag * jnp.sin(a_im * dt)
    den = a_re * a_re + a_im * a_im
    nr = abar_re - 1.0
    f_re = (nr * a_re + abar_im * a_im) / den
    f_im = (abar_im * a_re - nr * a_im) / den
    b_re = bre_ref[...]
    b_im = bim_ref[...]
    abr_ref[...] = abar_re
    abi_ref[...] = abar_im
    bbr_ref[...] = f_re * b_re - f_im * b_im
    bbi_ref[...] = f_re * b_im + f_im * b_re


def _s5_params(a_re, a_im, log_dt, b_re, b_im, c_re, c_im, d):
    Dp, G, P = a_re.shape
    H = b_re.shape[-1]
    R = Dp * G * H
    rep = lambda a: jnp.broadcast_to(a[:, :, None, :], (Dp, G, H, a.shape[-1])).reshape(R, -1)
    bt = lambda b: jnp.swapaxes(b, 2, 3).reshape(R, P)
    args = [rep(a_re), rep(a_im), rep(log_dt[:, :, None]), bt(b_re), bt(b_im)]
    out = jax.ShapeDtypeStruct((R, P), F32)
    abr, abi, bbr, bbi = pl.pallas_call(
        _s5_zoh_kernel, out_shape=[out] * 4, name="s5_discretise",
        compiler_params=pltpu.CompilerParams(vmem_limit_bytes=VMEM_LIMIT),
    )(*args)
    abar_re = abr.reshape(Dp, G, H, P)[:, :, 0, :].reshape(Dp, 1, G * P)
    abar_im = abi.reshape(Dp, G, H, P)[:, :, 0, :].reshape(Dp, 1, G * P)
    gpb = LANES // H
    nblk = G // gpb
    eye = jnp.eye(gpb, dtype=F32)

    def pack_in(bb):
        bb = bb.reshape(Dp, nblk, gpb, H, P)
        m = jnp.einsum("dkghp,gf->dkghfp", bb, eye)
        return m.reshape(Dp, nblk, gpb * H, gpb * P).astype(BF16)

    def pack_out(cc):
        cc = cc.reshape(Dp, nblk, gpb, H, P)
        m = jnp.einsum("dkghp,gf->dkgpfh", cc, eye)
        return m.reshape(Dp, nblk, gpb * P, gpb * H).astype(BF16)

    return {"s5_are": abar_re, "s5_aim": abar_im,
            "s5_wbr": pack_in(bbr), "s5_wbi": pack_in(bbi),
            "s5_wcr": pack_out(c_re), "s5_wci": pack_out(c_im),
            "s5_d": d.reshape(Dp, 1, G * H)}


def _s5_kernel(u_ref, are_ref, aim_ref, wbr_ref, wbi_ref, wcr_ref, wci_ref, d_ref, y_ref,
               sre_ref, sim_ref, u2_ref, ut_ref, bre_ref, bim_ref, yt_ref):
    B, tl, W = u_ref.shape
    nblk = wbr_ref.shape[0]
    in_w = wbr_ref.shape[1]
    st_w = wbr_ref.shape[2]

    @pl.when(pl.program_id(0) == 0)
    def _():
        sre_ref[...] = jnp.zeros_like(sre_ref)
        sim_ref[...] = jnp.zeros_like(sim_ref)

    nlb = W // LANES
    for b in range(B):
        for j in range(nlb):
            u2_ref[j, b * tl:(b + 1) * tl, :] = u_ref[b, :, j * LANES:(j + 1) * LANES]

    def gather(t, c):
        r0 = pl.multiple_of(t * B, B)
        for j in range(nlb):
            ut_ref[pl.ds(r0, B), j * LANES:(j + 1) * LANES] = u2_ref[j, pl.ds(t, B, stride=tl), :]
        return c
    lax.fori_loop(0, tl, gather, 0, unroll=8)

    for kb in range(nblk):
        lhs = ut_ref[:, kb * in_w:(kb + 1) * in_w].astype(BF16)
        bre_ref[:, kb * st_w:(kb + 1) * st_w] = _dot(lhs, wbr_ref[kb])
        bim_ref[:, kb * st_w:(kb + 1) * st_w] = _dot(lhs, wbi_ref[kb])

    n_state = sre_ref.shape[1]
    for hh in range(n_state // S5_HALF):
        lanes = slice(hh * S5_HALF, (hh + 1) * S5_HALF)
        ar = jnp.broadcast_to(are_ref[:, lanes], (B, S5_HALF))
        ai = jnp.broadcast_to(aim_ref[:, lanes], (B, S5_HALF))

        def step(t, carry, lanes=lanes, ar=ar, ai=ai):
            sr, si = carry
            r0 = pl.multiple_of(t * B, B)
            nsr = ar * sr - ai * si + bre_ref[pl.ds(r0, B), lanes]
            nsi = ar * si + ai * sr + bim_ref[pl.ds(r0, B), lanes]
            bre_ref[pl.ds(r0, B), lanes] = nsr
            bim_ref[pl.ds(r0, B), lanes] = nsi
            return nsr, nsi
        sr, si = lax.fori_loop(0, tl, step, (sre_ref[:, lanes], sim_ref[:, lanes]), unroll=4)
        sre_ref[:, lanes] = sr
        sim_ref[:, lanes] = si

    for kb in range(nblk):
        sl = slice(kb * st_w, (kb + 1) * st_w)
        ch = slice(kb * in_w, (kb + 1) * in_w)
        y = (_dot(bre_ref[:, sl].astype(BF16), wcr_ref[kb])
             - _dot(bim_ref[:, sl].astype(BF16), wci_ref[kb]))
        yt_ref[kb] = jax.nn.gelu(y + d_ref[:, ch] * ut_ref[:, ch])

    for b in range(B):
        for j in range(nlb):
            y_ref[b, :, j * LANES:(j + 1) * LANES] = yt_ref[j, pl.ds(b, tl, stride=B), :]


def _s5(u, p, l, B, L):
    W = u.shape[-1]
    tl = S5_TL
    n_state = p["s5_are"].shape[-1]
    blk = pl.BlockSpec((B, tl, W), lambda i: (0, i, 0))
    names = ["s5_are", "s5_aim", "s5_wbr", "s5_wbi", "s5_wcr", "s5_wci", "s5_d"]
    return pl.pallas_call(
        _s5_kernel,
        grid=(L // tl,),
        in_specs=[blk] + [_slab_spec(p[n], (l,)) for n in names],
        out_specs=blk,
        out_shape=jax.ShapeDtypeStruct((B, L, W), F32),
        scratch_shapes=[pltpu.VMEM((B, n_state), F32), pltpu.VMEM((B, n_state), F32),
                        pltpu.VMEM((W // LANES, B * tl, LANES), F32), pltpu.VMEM((B * tl, W), F32),
                        pltpu.VMEM((B * tl, n_state), F32), pltpu.VMEM((B * tl, n_state), F32),
                        pltpu.VMEM((W // LANES, B * tl, LANES), F32)],
        compiler_params=_cparams(("arbitrary",)),
        name="s5_scan",
    )(u.reshape(B, L, W), *[p[n] for n in names])


def _attn_kernel(q_ref, k_ref, v_ref, o_ref, s_ref, m_ref, mprev_ref, l_ref, acc_ref):
    tq = q_ref.shape[1]
    nh = m_ref.shape[0]
    qi = pl.program_id(2)
    c = (MLA_NOPE + MLA_ROPE) ** -0.5 * math.log2(math.e)
    nt = (((1,), (1,)), ((), ()))
    tn = (((0,), (0,)), ((), ()))
    m_ref[...] = jnp.full(m_ref.shape, -1e30, F32)
    l_ref[...] = jnp.zeros_like(l_ref)
    acc_ref[...] = jnp.zeros_like(acc_ref)

    def scores(j, h, masked):
        r0 = pl.multiple_of(j * tq, tq)
        qk = slice(h * MLA_HEAD_PAD, (h + 1) * MLA_HEAD_PAD)
        st = lax.dot_general(k_ref[0, pl.ds(r0, tq), qk], q_ref[0, :, qk], nt,
                             preferred_element_type=F32)
        if masked:
            kidx = lax.broadcasted_iota(jnp.int32, st.shape, 0)
            qidx = lax.broadcasted_iota(jnp.int32, st.shape, 1)
            st = jnp.where(kidx <= qidx, st, -jnp.inf)
        s_ref[h] = st
        m_old = m_ref[h]
        mprev_ref[h] = m_old
        m_ref[h] = jnp.maximum(m_old, jnp.max(st, axis=0, keepdims=True))

    def accumulate(j, h):
        r0 = pl.multiple_of(j * tq, tq)
        v = v_ref[0, pl.ds(r0, tq), h * MLA_DV:(h + 1) * MLA_DV]
        m_new = m_ref[h]
        alpha = jnp.exp2((mprev_ref[h] - m_new) * c)
        pt = jnp.exp2((s_ref[h] - m_new) * c)
        l_ref[h] = alpha * l_ref[h] + jnp.sum(pt, axis=0, keepdims=True)
        acc_ref[h] = alpha * acc_ref[h] + lax.dot_general(
            v, pt.astype(BF16), tn, preferred_element_type=F32)

    scores(qi, 0, True)
    for h in range(1, nh):
        scores(qi, h, True)
        accumulate(qi, h - 1)

    def body(j, carry):
        scores(j, 0, False)
        accumulate(jnp.where(j == 0, qi, j - 1), nh - 1)
        for h in range(1, nh):
            scores(j, h, False)
            accumulate(j, h - 1)
        return carry
    lax.fori_loop(0, qi, body, 0)
    accumulate(jnp.maximum(qi - 1, 0), nh - 1)
    for h in range(nh):
        o_ref[0, :, h * MLA_DV:(h + 1) * MLA_DV] = (acc_ref[h] / l_ref[h]).T.astype(BF16)


def _attention(mq, mk, mv, B, L):
    tq = ATTN_TQ
    nh = ATTN_HEADS_PER_STEP
    qw = nh * MLA_HEAD_PAD
    vw = nh * MLA_DV
    return pl.pallas_call(
        _attn_kernel,
        grid=(B, MLA_HEADS // nh, L // tq),
        in_specs=[pl.BlockSpec((1, tq, qw), lambda b, h, i: (b, i, h)),
                  pl.BlockSpec((1, L, qw), lambda b, h, i: (b, 0, h)),
                  pl.BlockSpec((1, L, vw), lambda b, h, i: (b, 0, h))],
        out_specs=pl.BlockSpec((1, tq, vw), lambda b, h, i: (b, i, h)),
        out_shape=jax.ShapeDtypeStruct((B, L, MLA_HEADS * MLA_DV), BF16),
        scratch_shapes=[pltpu.VMEM((nh, tq, tq), F32), pltpu.VMEM((nh, 1, tq), F32),
                        pltpu.VMEM((nh, 1, tq), F32), pltpu.VMEM((nh, 1, tq), F32),
                        pltpu.VMEM((nh, MLA_DV, tq), F32)],
        compiler_params=_cparams(("parallel", "parallel", "arbitrary")),
        name="mla_attention",
    )(mq.reshape(B, L, -1), mk.reshape(B, L, -1), mv.reshape(B, L, -1))


def _merge_kernel(x_ref, gin_ref, gout_ref, a_ref, y_ref, o_ref_in, wg_ref, wro_ref,
                  wga_ref, wgb_ref, wmo_ref, wout_ref, out_ref):
    x = x_ref[...]
    D = x.shape[1]
    h = _rms(x, gin_ref[...]).astype(BF16)
    y_ret = _dot(a_ref[...], wro_ref[...])
    merged = jax.nn.sigmoid(_dot(h, wg_ref[:, 0:D])) * y_ret
    ys = y_ref[...].astype(BF16)
    y_s5 = _dot(ys, wga_ref[...]) * jax.nn.sigmoid(_dot(ys, wgb_ref[...]))
    merged = merged + jax.nn.sigmoid(_dot(h, wg_ref[:, D:2 * D])) * y_s5
    y_mla = _dot(o_ref_in[...], wmo_ref[...])
    merged = merged + jax.nn.sigmoid(_dot(h, wg_ref[:, 2 * D:3 * D])) * y_mla
    out = _dot(merged.astype(BF16), wout_ref[...])
    out_ref[...] = x + _rms(out, gout_ref[...])


def _merge(x2, a_ret, y_s5, o_mla, p, l):
    T, D = x2.shape
    tm = ROW_TILE
    row = lambda n: pl.BlockSpec((tm, n), lambda i: (i, 0))
    n_sub = p["gains"].shape[0] // p["wgates"].shape[0]
    names = ["wgates", "ret_wo", "glu_a", "glu_b", "mla_wo", "w_out"]
    return pl.pallas_call(
        _merge_kernel,
        grid=(T // tm,),
        in_specs=[row(D), _slab_spec(p["gains"], (l * n_sub + 2,)),
                  _slab_spec(p["gains"], (l * n_sub + 3,)),
                  row(a_ret.shape[1]), row(y_s5.shape[1]), row(o_mla.shape[1])]
                 + [_slab_spec(p[n], (l,)) for n in names],
        out_specs=row(D),
        out_shape=jax.ShapeDtypeStruct((T, D), F32),
        compiler_params=_cparams(("parallel",)),
        name="gated_merge",
    )(x2, p["gains"], p["gains"], a_ret, y_s5, o_mla, *[p[n] for n in names])


def _prepare(norm_gains, ffn_w_gate, ffn_w_up, ffn_w_down, w_in, ret_w_o,
             s5_a_re, s5_a_im, s5_log_dt, s5_b_re, s5_b_im, s5_c_re, s5_c_im, s5_d,
             s5_glu_a, s5_glu_b, mla_q_norm, mla_kv_norm, mla_w_uq, mla_w_ukv, mla_w_o, w_out):
    depth, n_sub, D = norm_gains.shape
    q_rank = mla_q_norm.shape[-1]
    kv_rank = mla_kv_norm.shape[-1]
    s5_w = s5_d.shape[-1]
    ret_w = 2 * RET_HEADS * RET_DK + 2 * RET_HEADS * RET_DV
    m_lo = ret_w + s5_w
    m_w = q_rank + kv_rank + MLA_ROPE
    assert w_in.shape[-1] == m_lo + m_w + 3 * D
    hd_w = MLA_NOPE + MLA_ROPE
    wuq = jnp.pad(mla_w_uq.reshape(depth, q_rank, MLA_HEADS, hd_w),
                  ((0, 0), (0, 0), (0, 0), (0, MLA_HEAD_PAD - hd_w)))
    wukv = mla_w_ukv.reshape(depth, kv_rank, MLA_HEADS, MLA_NOPE + MLA_DV)
    p = {
        "gains": norm_gains.reshape(depth * n_sub, 1, D),
        "ffn_wg": ffn_w_gate.astype(BF16), "ffn_wu": ffn_w_up.astype(BF16),
        "ffn_wd": ffn_w_down.astype(BF16),
        "w_in": w_in.astype(BF16),
        "wm": jnp.pad(w_in[:, :, m_lo:m_lo + m_w],
                      ((0, 0), (0, 0), (0, LANES - MLA_ROPE))).astype(BF16),
        "wgates": w_in[:, :, m_lo + m_w:].astype(BF16),
        "qn": mla_q_norm.reshape(depth, 1, q_rank), "kvn": mla_kv_norm.reshape(depth, 1, kv_rank),
        "wuq": wuq.reshape(depth, q_rank, MLA_HEADS * MLA_HEAD_PAD).astype(BF16),
        "wuk": wukv[..., :MLA_NOPE].reshape(depth, kv_rank, MLA_HEADS * MLA_NOPE).astype(BF16),
        "wuv": wukv[..., MLA_NOPE:].reshape(depth, kv_rank, MLA_HEADS * MLA_DV).astype(BF16),
        "ret_wo": ret_w_o.astype(BF16), "glu_a": s5_glu_a.astype(BF16),
        "glu_b": s5_glu_b.astype(BF16), "mla_wo": mla_w_o.astype(BF16),
        "w_out": w_out.astype(BF16),
    }
    p.update(_s5_params(s5_a_re, s5_a_im, s5_log_dt, s5_b_re, s5_b_im, s5_c_re, s5_c_im, s5_d))
    return p


def kernel(x, positions, norm_gains, ffn_w_gate, ffn_w_up, ffn_w_down, w_in, ret_w_o,
           s5_a_re, s5_a_im, s5_log_dt, s5_b_re, s5_b_im, s5_c_re, s5_c_im, s5_d,
           s5_glu_a, s5_glu_b, mla_q_norm, mla_kv_norm, mla_w_uq, mla_w_ukv, mla_w_o, w_out):
    B, L, D = x.shape
    depth = norm_gains.shape[0]
    assert L % ROW_TILE == 0 and L % RET_ROWS == 0 and L % ATTN_TQ == 0 and L % S5_TL == 0
    assert B == SUBLANES, "the S5 scan keeps the batch on the sublane axis"

    p = _prepare(norm_gains, ffn_w_gate, ffn_w_up, ffn_w_down, w_in, ret_w_o,
                 s5_a_re, s5_a_im, s5_log_dt, s5_b_re, s5_b_im, s5_c_re, s5_c_im, s5_d,
                 s5_glu_a, s5_glu_b, mla_q_norm, mla_kv_norm, mla_w_uq, mla_w_ukv, mla_w_o, w_out)
    tabs = _rope_tables(positions)
    ret_tabs = _ret_tables()
    x2 = x.reshape(B * L, D)
    for l in range(depth):
        x2 = _ffn(x2, p, l, 0)
        rq, rk, rv, rg, u, mq, mk, mv = _proj(x2, p, l, tabs)
        a_ret = _retention(rq, rk, rv, rg, ret_tabs, B, L)
        y_s5 = _s5(u, p, l, B, L)
        o_mla = _attention(mq, mk, mv, B, L)
        x2 = _merge(x2, a_ret.reshape(B * L, -1), y_s5.reshape(B * L, -1),
                    o_mla.reshape(B * L, -1), p, l)
        x2 = _ffn(x2, p, l, 1)
    return x2.reshape(B, L, D)
```

```python
import math

import jax
import jax.numpy as jnp
from jax import lax
from jax.experimental import pallas as pl
from jax.experimental.pallas import tpu as pltpu

F32 = jnp.float32
BF16 = jnp.bfloat16

ROPE_BASE = 10000.0
NORM_EPS = 1e-6
GN_EPS = 1e-5
FFN_RES = 0.5
RET_HEADS = 4
RET_DK = 128
RET_DV = 128
RET_CHUNK = 128
MLA_HEADS = 4
MLA_NOPE = 128
MLA_ROPE = 64
MLA_DV = 128
MLA_HEAD_PAD = 256

LANES = 128
SUBLANES = 8
VMEM_LIMIT = 56 * 1024 * 1024

ROW_TILE = 512
FFN_CHUNKS = ((0, 1024), (1024, 2048), (2048, 2816))
RET_ROWS = 512
S5_TL = 64
S5_HALF = 1024
ATTN_TQ = 512
ATTN_HEADS_PER_STEP = 4


def _cparams(sem):
    return pltpu.CompilerParams(dimension_semantics=sem, vmem_limit_bytes=VMEM_LIMIT)


def _const_spec(shape):
    nd = len(shape)
    return pl.BlockSpec(shape, lambda *_: (0,) * nd, pipeline_mode=pl.Buffered(1))


def _slab_spec(arr, lead, block=None, col=0):
    tail = tuple(arr.shape[len(lead):]) if block is None else tuple(block)
    idx = tuple(lead) + (0,) * (len(tail) - 1) + (col,)
    return pl.BlockSpec((None,) * len(lead) + tail, lambda *_: idx,
                        pipeline_mode=pl.Buffered(1))


def _rms(x, g):
    y = x * lax.rsqrt(jnp.mean(x * x, axis=-1, keepdims=True) + NORM_EPS)
    return y * g


def _dot(a, b):
    return jnp.dot(a, b, preferred_element_type=F32)


def _rope_kernel(pos_ref, f_ref, a_ref, na_ref, b_ref, p_ref, np_ref, q_ref,
                 cr_ref, sr_ref, cm_ref, s1_ref, s2_ref):
    ang = pos_ref[...] * f_ref[...]
    c = jnp.cos(ang)
    s = jnp.sin(ang)
    hr = RET_DK // 2
    c_hi = pltpu.roll(c, hr, 1)
    s_hi = pltpu.roll(s, hr, 1)
    c_m2 = pltpu.roll(c, hr + MLA_ROPE // 2, 1)
    s_m2 = pltpu.roll(s, hr + MLA_ROPE // 2, 1)
    cr_ref[...] = c * a_ref[...] + c_hi * b_ref[...]
    sr_ref[...] = s * na_ref[...] + s_hi * b_ref[...]
    cm_ref[...] = c_hi * p_ref[...] + c_m2 * q_ref[...]
    s1_ref[...] = s_hi * np_ref[...]
    s2_ref[...] = s_m2 * q_ref[...]


def _rope_tables(positions):
    B, L = positions.shape
    T = B * L
    pos = positions.astype(F32).reshape(T, 1)
    inv_r = ROPE_BASE ** (-jnp.arange(0, RET_DK, 2, dtype=F32) / RET_DK)
    inv_m = ROPE_BASE ** (-jnp.arange(0, MLA_ROPE, 2, dtype=F32) / MLA_ROPE)
    hr = RET_DK // 2
    hm = MLA_ROPE // 2
    lane = jnp.arange(LANES)
    freq = jnp.concatenate([inv_r, inv_m, jnp.zeros((LANES - hr - hm,), F32)])
    a = (lane < hr).astype(F32)
    b = (lane >= hr).astype(F32)
    p = (lane < hm).astype(F32)
    q = ((lane >= hm) & (lane < 2 * hm)).astype(F32)
    rows = [r.reshape(1, LANES) for r in (freq, a, -a, b, p, -p, q)]
    tm = 2048
    row_spec = pl.BlockSpec((1, LANES), lambda i: (0, 0))
    tab_spec = pl.BlockSpec((tm, LANES), lambda i: (i, 0))
    tab_shape = jax.ShapeDtypeStruct((T, LANES), F32)
    return pl.pallas_call(
        _rope_kernel,
        grid=(T // tm,),
        in_specs=[pl.BlockSpec((tm, 1), lambda i: (i, 0))] + [row_spec] * len(rows),
        out_specs=[tab_spec] * 5,
        out_shape=[tab_shape] * 5,
        compiler_params=_cparams(("parallel",)),
        name="rope_tables",
    )(pos, *rows)


def _ffn_kernel(x_ref, xn_ref, gpre_ref, gpost_ref, wg_ref, wu_ref, wd_ref, o_ref, ha_ref):
    half = x_ref.shape[0] // 2
    gpre = gpre_ref[...]
    gpost = gpost_ref[...]

    @pl.when(pl.program_id(0) == 0)
    def _():
        ha_ref[...] = _rms(x_ref[:half, :], gpre).astype(BF16)

    def swiglu(h):
        y = None
        pending = None
        for c0, c1 in FFN_CHUNKS + ((None, None),):
            if c0 is not None:
                g = _dot(h, wg_ref[:, c0:c1])
                u = _dot(h, wu_ref[:, c0:c1])
            if pending is not None:
                a, p0, p1 = pending
                part = _dot(a, wd_ref[p0:p1, :])
                y = part if y is None else y + part
            if c0 is not None:
                pending = ((g * jax.nn.sigmoid(g) * u).astype(BF16), c0, c1)
        return y

    ya = swiglu(ha_ref[...])
    yb = swiglu(_rms(x_ref[half:, :], gpre).astype(BF16))
    o_ref[:half, :] = x_ref[:half, :] + FFN_RES * _rms(ya, gpost)
    ha_ref[...] = _rms(xn_ref[...], gpre).astype(BF16)
    o_ref[half:, :] = x_ref[half:, :] + FFN_RES * _rms(yb, gpost)


def _ffn(x2, p, l, k):
    T, D = x2.shape
    half = ROW_TILE
    n = T // (2 * half)
    row = pl.BlockSpec((2 * half, D), lambda i: (i, 0))
    nxt = pl.BlockSpec((half, D), lambda i: (jnp.minimum(2 * i + 2, 2 * n - 2), 0))
    n_sub = p["gains"].shape[0] // p["ffn_wg"].shape[0]
    gi = l * n_sub + 4 * k
    return pl.pallas_call(
        _ffn_kernel,
        grid=(n,),
        in_specs=[row, nxt, _slab_spec(p["gains"], (gi,)), _slab_spec(p["gains"], (gi + 1,)),
                  _slab_spec(p["ffn_wg"], (l, k)), _slab_spec(p["ffn_wu"], (l, k)),
                  _slab_spec(p["ffn_wd"], (l, k))],
        out_specs=row,
        out_shape=jax.ShapeDtypeStruct((T, D), F32),
        scratch_shapes=[pltpu.VMEM((half, D), BF16)],
        compiler_params=_cparams(("arbitrary",)),
        name="ffn",
    )(x2, x2, p["gains"], p["gains"], p["ffn_wg"], p["ffn_wu"], p["ffn_wd"])


def _rope_ret(v, cos, sin):
    return v * cos + pltpu.roll(v, RET_DK // 2, 1) * sin


def _rope_mla(v, cos, sin1, sin2):
    hm = MLA_ROPE // 2
    return v * cos + pltpu.roll(v, LANES - hm, 1) * sin1 + pltpu.roll(v, hm, 1) * sin2


def _proj_kernel(x_ref, xn_ref, g_ref, wr_ref, wu_ref, wm_ref, qn_ref, kvn_ref,
                 wuq_ref, wuk_ref, wuv_ref, cr_ref, sr_ref, cm_ref, s1_ref, s2_ref,
                 rq_ref, rk_ref, rv_ref, rg_ref, u_ref, mq_ref, mk_ref, mv_ref, ha_ref):
    half = x_ref.shape[0] // 2
    gain = g_ref[...]
    qk_w = RET_HEADS * RET_DK
    v_w = RET_HEADS * RET_DV
    q_rank = qn_ref.shape[1]
    kv_rank = kvn_ref.shape[1]

    @pl.when(pl.program_id(0) == 0)
    def _():
        ha_ref[...] = _rms(x_ref[:half, :], gain).astype(BF16)

    def project(h):
        m = _dot(h, wm_ref[...])
        r = _dot(h, wr_ref[...])
        u = _dot(h, wu_ref[...])
        return m, r, u

    def mla_up(m):
        cq = _rms(m[:, :q_rank], qn_ref[...]).astype(BF16)
        ckv = _rms(m[:, q_rank:q_rank + kv_rank], kvn_ref[...]).astype(BF16)
        q = _dot(cq, wuq_ref[...])
        kn = _dot(ckv, wuk_ref[...])
        v = _dot(ckv, wuv_ref[...])
        return q, kn, v

    def store(rows, m, r, u, q, kn, v):
        cr = cr_ref[rows, :]
        sr = sr_ref[rows, :]
        for hd in range(RET_HEADS):
            lo = hd * RET_DK
            rq_ref[rows, lo:lo + RET_DK] = _rope_ret(r[:, lo:lo + RET_DK], cr, sr).astype(BF16)
            rk_ref[rows, lo:lo + RET_DK] = (_rope_ret(r[:, qk_w + lo:qk_w + lo + RET_DK], cr, sr)
                                            * (RET_DK ** -0.5)).astype(BF16)
        rv_ref[rows, :] = r[:, 2 * qk_w:2 * qk_w + v_w].astype(BF16)
        rg_ref[rows, :] = r[:, 2 * qk_w + v_w:].astype(BF16)
        u_ref[rows, :] = u
        cm = cm_ref[rows, :]
        s1 = s1_ref[rows, :]
        s2 = s2_ref[rows, :]
        kpe = _rope_mla(m[:, q_rank + kv_rank:], cm, s1, s2).astype(BF16)
        mv_ref[rows, :] = v.astype(BF16)
        for hd in range(MLA_HEADS):
            lo = hd * MLA_HEAD_PAD
            mq_ref[rows, lo:lo + MLA_NOPE] = q[:, lo:lo + MLA_NOPE].astype(BF16)
            mq_ref[rows, lo + MLA_NOPE:lo + MLA_HEAD_PAD] = _rope_mla(
                q[:, lo + MLA_NOPE:lo + MLA_HEAD_PAD], cm, s1, s2).astype(BF16)
            mk_ref[rows, lo:lo + MLA_NOPE] = kn[:, hd * MLA_NOPE:(hd + 1) * MLA_NOPE].astype(BF16)
            mk_ref[rows, lo + MLA_NOPE:lo + MLA_HEAD_PAD] = kpe

    pa = project(ha_ref[...])
    pb = project(_rms(x_ref[half:, :], gain).astype(BF16))
    ua = mla_up(pa[0])
    ub = mla_up(pb[0])
    store(slice(0, half), *pa, *ua)
    ha_ref[...] = _rms(xn_ref[...], gain).astype(BF16)
    store(slice(half, 2 * half), *pb, *ub)


def _proj(x2, p, l, tabs):
    T, D = x2.shape
    tm = ROW_TILE
    n = T // tm
    row = lambda n: pl.BlockSpec((tm, n), lambda i: (i, 0))
    nxt = pl.BlockSpec((tm // 2, D), lambda i: (jnp.minimum(2 * i + 2, 2 * n - 2), 0))
    n_sub = p["gains"].shape[0] // p["w_in"].shape[0]
    ret_w = 2 * RET_HEADS * RET_DK + 2 * RET_HEADS * RET_DV
    s5_w = p["s5_d"].shape[-1]
    assert ret_w % s5_w == 0
    consts = [(p["gains"], _slab_spec(p["gains"], (l * n_sub + 2,))),
              (p["w_in"], _slab_spec(p["w_in"], (l,), block=(D, ret_w), col=0)),
              (p["w_in"], _slab_spec(p["w_in"], (l,), block=(D, s5_w), col=ret_w // s5_w)),
              (p["wm"], _slab_spec(p["wm"], (l,))),
              (p["qn"], _slab_spec(p["qn"], (l,))), (p["kvn"], _slab_spec(p["kvn"], (l,))),
              (p["wuq"], _slab_spec(p["wuq"], (l,))), (p["wuk"], _slab_spec(p["wuk"], (l,))),
              (p["wuv"], _slab_spec(p["wuv"], (l,)))]
    out_w = [(RET_HEADS * RET_DK, BF16), (RET_HEADS * RET_DK, BF16),
             (RET_HEADS * RET_DV, BF16), (RET_HEADS * RET_DV, BF16), (s5_w, F32),
             (MLA_HEADS * MLA_HEAD_PAD, BF16), (MLA_HEADS * MLA_HEAD_PAD, BF16),
             (MLA_HEADS * MLA_DV, BF16)]
    return pl.pallas_call(
        _proj_kernel,
        grid=(n,),
        in_specs=[row(D), nxt] + [s for _, s in consts] + [row(LANES)] * 5,
        out_specs=[row(w) for w, _ in out_w],
        out_shape=[jax.ShapeDtypeStruct((T, w), dt) for w, dt in out_w],
        scratch_shapes=[pltpu.VMEM((tm // 2, D), BF16)],
        compiler_params=_cparams(("arbitrary",)),
        name="mixer_proj",
    )(x2, x2, *[a for a, _ in consts], *tabs)


def _ret_kernel(q_ref, k_ref, v_ref, g_ref, intra_ref, qd_ref, kd_ref, cd_ref, o_ref, st_ref):
    @pl.when(pl.program_id(1) == 0)
    def _():
        st_ref[...] = jnp.zeros_like(st_ref)

    C = RET_CHUNK
    nt = (((1,), (1,)), ((), ()))
    tn = (((0,), (0,)), ((), ()))
    for hd in range(RET_HEADS):
        cols = slice(hd * RET_DK, (hd + 1) * RET_DK)
        vcols = slice(hd * RET_DV, (hd + 1) * RET_DV)
        intra = intra_ref[hd]
        qd = qd_ref[hd]
        kd = kd_ref[hd]
        cd = cd_ref[hd]
        state = st_ref[hd]
        for c in range(RET_ROWS // C):
            rows = slice(c * C, (c + 1) * C)
            q = q_ref[0, rows, cols]
            k = k_ref[0, rows, cols]
            v = v_ref[0, rows, vcols]
            scores = lax.dot_general(q, k, nt, preferred_element_type=F32) * intra
            inner = _dot(scores.astype(BF16), v)
            cross = _dot((q.astype(F32) * qd).astype(BF16), state.astype(BF16))
            kv = lax.dot_general((k.astype(F32) * kd).astype(BF16), v, tn,
                                 preferred_element_type=F32)
            state = cd * state + kv
            o = inner + cross
            mu = jnp.mean(o, axis=-1, keepdims=True)
            var = jnp.mean(jnp.square(o - mu), axis=-1, keepdims=True)
            on = (o - mu) * lax.rsqrt(var + GN_EPS)
            g = g_ref[0, rows, vcols].astype(F32)
            o_ref[0, rows, vcols] = (g * jax.nn.sigmoid(g) * on).astype(BF16)
        st_ref[hd] = state


def _ret_tables():
    C = RET_CHUNK
    log_gamma = jnp.log1p(-jnp.exp2(-5.0 - jnp.arange(RET_HEADS, dtype=F32)))
    pos = jnp.arange(C, dtype=F32)
    rel = pos[:, None] - pos[None, :]
    intra = jnp.where(rel[None] >= 0.0,
                      jnp.exp(jnp.maximum(rel, 0.0)[None] * log_gamma[:, None, None]), 0.0)
    k_decay = jnp.exp((C - 1.0 - pos)[:, None] * log_gamma[None, :])
    q_decay = jnp.exp((pos + 1.0)[:, None] * log_gamma[None, :])
    chunk_decay = jnp.exp(C * log_gamma)
    qd = jnp.broadcast_to(q_decay.T[:, :, None], (RET_HEADS, C, RET_DK))
    kd = jnp.broadcast_to(k_decay.T[:, :, None], (RET_HEADS, C, RET_DK))
    cd = jnp.broadcast_to(chunk_decay[:, None, None], (RET_HEADS, 1, RET_DV))
    return intra, qd, kd, cd


def _retention(rq, rk, rv, rg, tables, B, L):
    W = rq.shape[-1]
    shp = (B, L, W)
    blk = pl.BlockSpec((1, RET_ROWS, W), lambda b, i: (b, i, 0))
    return pl.pallas_call(
        _ret_kernel,
        grid=(B, L // RET_ROWS),
        in_specs=[blk] * 4 + [_const_spec(t.shape) for t in tables],
        out_specs=blk,
        out_shape=jax.ShapeDtypeStruct(shp, BF16),
        scratch_shapes=[pltpu.VMEM((RET_HEADS, RET_DK, RET_DV), F32)],
        compiler_params=_cparams(("parallel", "arbitrary")),
        name="retention",
    )(rq.reshape(shp), rk.reshape(shp), rv.reshape(shp), rg.reshape(shp), *tables)


def _s5_zoh_kernel(are_ref, aim_ref, ldt_ref, bre_ref, bim_ref,
                   abr_ref, abi_ref, bbr_ref, bbi_ref):
    a_re = are_ref[...]
    a_im = aim_ref[...]
    dt = jnp.exp(ldt_ref[...])
    mag = jnp.exp(a_re * dt)
    abar_re = mag * jnp.cos(a_im * dt)
    abar_im = mag * jnp.sin(a_im * dt)
    den = a_re * a_re + a_im * a_im
    nr = abar_re - 1.0
    f_re = (nr * a_re + abar_im * a_im) / den
    f_im = (abar_im * a_re - nr * a_im) / den
    b_re = bre_ref[...]
    b_im = bim_ref[...]
    abr_ref[...] = abar_re
    abi_ref[...] = abar_im
    bbr_ref[...] = f_re * b_re - f_im * b_im
    bbi_ref[...] = f_re * b_im + f_im * b_re


def _s5_params(a_re, a_im, log_dt, b_re, b_im, c_re, c_im, d):
    Dp, G, P = a_re.shape
    H = b_re.shape[-1]
    R = Dp * G * H
    rep = lambda a: jnp.broadcast_to(a[:, :, None, :], (Dp, G, H, a.shape[-1])).reshape(R, -1)
    bt = lambda b: jnp.swapaxes(b, 2, 3).reshape(R, P)
    args = [rep(a_re), rep(a_im), rep(log_dt[:, :, None]), bt(b_re), bt(b_im)]
    out = jax.ShapeDtypeStruct((R, P), F32)
    abr, abi, bbr, bbi = pl.pallas_call(
        _s5_zoh_kernel, out_shape=[out] * 4, name="s5_discretise",
        compiler_params=pltpu.CompilerParams(vmem_limit_bytes=VMEM_LIMIT),
    )(*args)
    abar_re = abr.reshape(Dp, G, H, P)[:, :, 0, :].reshape(Dp, 1, G * P)
    abar_im = abi.reshape(Dp, G, H, P)[:, :, 0, :].reshape(Dp, 1, G * P)
    gpb = LANES // H
    nblk = G // gpb
    eye = jnp.eye(gpb, dtype=F32)

    def pack_in(bb):
        bb = bb.reshape(Dp, nblk, gpb, H, P)
        m = jnp.einsum("dkghp,gf->dkghfp", bb, eye)
        return m.reshape(Dp, nblk, gpb * H, gpb * P).astype(BF16)

    def pack_out(cc):
        cc = cc.reshape(Dp, nblk, gpb, H, P)
        m = jnp.einsum("dkghp,gf->dkgpfh", cc, eye)
        return m.reshape(Dp, nblk, gpb * P, gpb * H).astype(BF16)

    return {"s5_are": abar_re, "s5_aim": abar_im,
            "s5_wbr": pack_in(bbr), "s5_wbi": pack_in(bbi),
            "s5_wcr": pack_out(c_re), "s5_wci": pack_out(c_im),
            "s5_d": d.reshape(Dp, 1, G * H)}


def _s5_kernel(u_ref, are_ref, aim_ref, wbr_ref, wbi_ref, wcr_ref, wci_ref, d_ref, y_ref,
               sre_ref, sim_ref, u2_ref, ut_ref, bre_ref, bim_ref, yt_ref):
    B, tl, W = u_ref.shape
    nblk = wbr_ref.shape[0]
    in_w = wbr_ref.shape[1]
    st_w = wbr_ref.shape[2]

    @pl.when(pl.program_id(0) == 0)
    def _():
        sre_ref[...] = jnp.zeros_like(sre_ref)
        sim_ref[...] = jnp.zeros_like(sim_ref)

    nlb = W // LANES
    for b in range(B):
        for j in range(nlb):
            u2_ref[j, b * tl:(b + 1) * tl, :] = u_ref[b, :, j * LANES:(j + 1) * LANES]

    def gather(t, c):
        r0 = pl.multiple_of(t * B, B)
        for j in range(nlb):
            ut_ref[pl.ds(r0, B), j * LANES:(j + 1) * LANES] = u2_ref[j, pl.ds(t, B, stride=tl), :]
        return c
    lax.fori_loop(0, tl, gather, 0, unroll=8)

    for kb in range(nblk):
        lhs = ut_ref[:, kb * in_w:(kb + 1) * in_w].astype(BF16)
        bre_ref[:, kb * st_w:(kb + 1) * st_w] = _dot(lhs, wbr_ref[kb])
        bim_ref[:, kb * st_w:(kb + 1) * st_w] = _dot(lhs, wbi_ref[kb])

    n_state = sre_ref.shape[1]
    for hh in range(n_state // S5_HALF):
        lanes = slice(hh * S5_HALF, (hh + 1) * S5_HALF)
        ar = jnp.broadcast_to(are_ref[:, lanes], (B, S5_HALF))
        ai = jnp.broadcast_to(aim_ref[:, lanes], (B, S5_HALF))

        def step(t, carry, lanes=lanes, ar=ar, ai=ai):
            sr, si = carry
            r0 = pl.multiple_of(t * B, B)
            nsr = ar * sr - ai * si + bre_ref[pl.ds(r0, B), lanes]
            nsi = ar * si + ai * sr + bim_ref[pl.ds(r0, B), lanes]
            bre_ref[pl.ds(r0, B), lanes] = nsr
            bim_ref[pl.ds(r0, B), lanes] = nsi
            return nsr, nsi
        sr, si = lax.fori_loop(0, tl, step, (sre_ref[:, lanes], sim_ref[:, lanes]), unroll=4)
        sre_ref[:, lanes] = sr
        sim_ref[:, lanes] = si

    for kb in range(nblk):
        sl = slice(kb * st_w, (kb + 1) * st_w)
        ch = slice(kb * in_w, (kb + 1) * in_w)
        y = (_dot(bre_ref[:, sl].astype(BF16), wcr_ref[kb])
             - _dot(bim_ref[:, sl].astype(BF16), wci_ref[kb]))
        yt_ref[kb] = jax.nn.gelu(y + d_ref[:, ch] * ut_ref[:, ch])

    for b in range(B):
        for j in range(nlb):
            y_ref[b, :, j * LANES:(j + 1) * LANES] = yt_ref[j, pl.ds(b, tl, stride=B), :]


def _s5(u, p, l, B, L):
    W = u.shape[-1]
    tl = S5_TL
    n_state = p["s5_are"].shape[-1]
    blk = pl.BlockSpec((B, tl, W), lambda i: (0, i, 0))
    names = ["s5_are", "s5_aim", "s5_wbr", "s5_wbi", "s5_wcr", "s5_wci", "s5_d"]
    return pl.pallas_call(
        _s5_kernel,
        grid=(L // tl,),
        in_specs=[blk] + [_slab_spec(p[n], (l,)) for n in names],
        out_specs=blk,
        out_shape=jax.ShapeDtypeStruct((B, L, W), F32),
        scratch_shapes=[pltpu.VMEM((B, n_state), F32), pltpu.VMEM((B, n_state), F32),
                        pltpu.VMEM((W // LANES, B * tl, LANES), F32), pltpu.VMEM((B * tl, W), F32),
                        pltpu.VMEM((B * tl, n_state), F32), pltpu.VMEM((B * tl, n_state), F32),
                        pltpu.VMEM((W // LANES, B * tl, LANES), F32)],
        compiler_params=_cparams(("arbitrary",)),
        name="s5_scan",
    )(u.reshape(B, L, W), *[p[n] for n in names])


def _attn_kernel(q_ref, k_ref, v_ref, o_ref, s_ref, m_ref, mprev_ref, l_ref, acc_ref, vt_ref):
    tq = q_ref.shape[1]
    nh = m_ref.shape[0]
    qi = pl.program_id(2)
    c = (MLA_NOPE + MLA_ROPE) ** -0.5 * math.log2(math.e)
    nt = (((1,), (1,)), ((), ()))
    m_ref[...] = jnp.full(m_ref.shape, -1e30, F32)
    l_ref[...] = jnp.zeros_like(l_ref)
    acc_ref[...] = jnp.zeros_like(acc_ref)

    def scores(j, h, masked):
        r0 = pl.multiple_of(j * tq, tq)
        qk = slice(h * MLA_HEAD_PAD, (h + 1) * MLA_HEAD_PAD)
        st = lax.dot_general(k_ref[0, pl.ds(r0, tq), qk], q_ref[0, :, qk], nt,
                             preferred_element_type=F32)
        if masked:
            kidx = lax.broadcasted_iota(jnp.int32, st.shape, 0)
            qidx = lax.broadcasted_iota(jnp.int32, st.shape, 1)
            st = jnp.where(kidx <= qidx, st, -jnp.inf)
        s_ref[h] = st
        m_old = m_ref[h]
        mprev_ref[h] = m_old
        m_ref[h] = jnp.maximum(m_old, jnp.max(st, axis=0, keepdims=True))

    @pl.when(qi == 0)
    def _():
        for h in range(nh):
            for blk in range(v_ref.shape[1] // tq):
                rows = slice(blk * tq, (blk + 1) * tq)
                vt_ref[h, :, rows] = v_ref[0, rows, h * MLA_DV:(h + 1) * MLA_DV].T

    def accumulate(j, h):
        r0 = pl.multiple_of(j * tq, tq)
        m_new = m_ref[h]
        alpha = jnp.exp2((mprev_ref[h] - m_new) * c)
        pt = jnp.exp2((s_ref[h] - m_new) * c)
        l_ref[h] = alpha * l_ref[h] + jnp.sum(pt, axis=0, keepdims=True)
        acc_ref[h] = alpha * acc_ref[h] + _dot(vt_ref[h, :, pl.ds(r0, tq)],
                                               pt.astype(BF16))

    scores(qi, 0, True)
    for h in range(1, nh):
        scores(qi, h, True)
        accumulate(qi, h - 1)

    def body(j, carry):
        scores(j, 0, False)
        accumulate(jnp.where(j == 0, qi, j - 1), nh - 1)
        for h in range(1, nh):
            scores(j, h, False)
            accumulate(j, h - 1)
        return carry
    lax.fori_loop(0, qi, body, 0)
    accumulate(jnp.maximum(qi - 1, 0), nh - 1)
    for h in range(nh):
        o_ref[0, :, h * MLA_DV:(h + 1) * MLA_DV] = (acc_ref[h] / l_ref[h]).T.astype(BF16)


def _attention(mq, mk, mv, B, L):
    tq = ATTN_TQ
    nh = ATTN_HEADS_PER_STEP
    qw = nh * MLA_HEAD_PAD
    vw = nh * MLA_DV
    return pl.pallas_call(
        _attn_kernel,
        grid=(B, MLA_HEADS // nh, L // tq),
        in_specs=[pl.BlockSpec((1, tq, qw), lambda b, h, i: (b, i, h)),
                  pl.BlockSpec((1, L, qw), lambda b, h, i: (b, 0, h)),
                  pl.BlockSpec((1, L, vw), lambda b, h, i: (b, 0, h))],
        out_specs=pl.BlockSpec((1, tq, vw), lambda b, h, i: (b, i, h)),
        out_shape=jax.ShapeDtypeStruct((B, L, MLA_HEADS * MLA_DV), BF16),
        scratch_shapes=[pltpu.VMEM((nh, tq, tq), F32), pltpu.VMEM((nh, 1, tq), F32),
                        pltpu.VMEM((nh, 1, tq), F32), pltpu.VMEM((nh, 1, tq), F32),
                        pltpu.VMEM((nh, MLA_DV, tq), F32), pltpu.VMEM((nh, MLA_DV, L), BF16)],
        compiler_params=_cparams(("parallel", "parallel", "arbitrary")),
        name="mla_attention",
    )(mq.reshape(B, L, -1), mk.reshape(B, L, -1), mv.reshape(B, L, -1))


def _merge_kernel(x_ref, xn_ref, gin_ref, gout_ref, a_ref, y_ref, o_ref_in, wg_ref, wro_ref,
                  wga_ref, wgb_ref, wmo_ref, wout_ref, out_ref, ha_ref):
    half = x_ref.shape[0] // 2
    D = x_ref.shape[1]
    gin = gin_ref[...]
    gout = gout_ref[...]

    @pl.when(pl.program_id(0) == 0)
    def _():
        ha_ref[...] = _rms(x_ref[:half, :], gin).astype(BF16)

    def branches(h, rows):
        ys = y_ref[rows, :].astype(BF16)
        return (_dot(h, wg_ref[:, 0:D]), _dot(a_ref[rows, :], wro_ref[...]),
                _dot(h, wg_ref[:, D:2 * D]), _dot(ys, wga_ref[...]), _dot(ys, wgb_ref[...]),
                _dot(h, wg_ref[:, 2 * D:3 * D]), _dot(o_ref_in[rows, :], wmo_ref[...]))

    def combine(g0, y_ret, g1, ya, yb, g2, y_mla):
        merged = jax.nn.sigmoid(g0) * y_ret
        merged = merged + jax.nn.sigmoid(g1) * (ya * jax.nn.sigmoid(yb))
        merged = merged + jax.nn.sigmoid(g2) * y_mla
        return merged.astype(BF16)

    rows_a = slice(0, half)
    rows_b = slice(half, 2 * half)
    ba = branches(ha_ref[...], rows_a)
    bb = branches(_rms(x_ref[rows_b, :], gin).astype(BF16), rows_b)
    oa = _dot(combine(*ba), wout_ref[...])
    ob = _dot(combine(*bb), wout_ref[...])
    out_ref[rows_a, :] = x_ref[rows_a, :] + _rms(oa, gout)
    ha_ref[...] = _rms(xn_ref[...], gin).astype(BF16)
    out_ref[rows_b, :] = x_ref[rows_b, :] + _rms(ob, gout)


def _merge(x2, a_ret, y_s5, o_mla, p, l):
    T, D = x2.shape
    tm = ROW_TILE
    n = T // tm
    row = lambda w: pl.BlockSpec((tm, w), lambda i: (i, 0))
    nxt = pl.BlockSpec((tm // 2, D), lambda i: (jnp.minimum(2 * i + 2, 2 * n - 2), 0))
    n_sub = p["gains"].shape[0] // p["wgates"].shape[0]
    names = ["wgates", "ret_wo", "glu_a", "glu_b", "mla_wo", "w_out"]
    return pl.pallas_call(
        _merge_kernel,
        grid=(n,),
        in_specs=[row(D), nxt, _slab_spec(p["gains"], (l * n_sub + 2,)),
                  _slab_spec(p["gains"], (l * n_sub + 3,)),
                  row(a_ret.shape[1]), row(y_s5.shape[1]), row(o_mla.shape[1])]
                 + [_slab_spec(p[n], (l,)) for n in names],
        out_specs=row(D),
        out_shape=jax.ShapeDtypeStruct((T, D), F32),
        scratch_shapes=[pltpu.VMEM((tm // 2, D), BF16)],
        compiler_params=_cparams(("arbitrary",)),
        name="gated_merge",
    )(x2, x2, p["gains"], p["gains"], a_ret, y_s5, o_mla, *[p[n] for n in names])


def _prepare(norm_gains, ffn_w_gate, ffn_w_up, ffn_w_down, w_in, ret_w_o,
             s5_a_re, s5_a_im, s5_log_dt, s5_b_re, s5_b_im, s5_c_re, s5_c_im, s5_d,
             s5_glu_a, s5_glu_b, mla_q_norm, mla_kv_norm, mla_w_uq, mla_w_ukv, mla_w_o, w_out):
    depth, n_sub, D = norm_gains.shape
    q_rank = mla_q_norm.shape[-1]
    kv_rank = mla_kv_norm.shape[-1]
    s5_w = s5_d.shape[-1]
    ret_w = 2 * RET_HEADS * RET_DK + 2 * RET_HEADS * RET_DV
    m_lo = ret_w + s5_w
    m_w = q_rank + kv_rank + MLA_ROPE
    assert w_in.shape[-1] == m_lo + m_w + 3 * D
    hd_w = MLA_NOPE + MLA_ROPE
    wuq = jnp.pad(mla_w_uq.reshape(depth, q_rank, MLA_HEADS, hd_w),
                  ((0, 0), (0, 0), (0, 0), (0, MLA_HEAD_PAD - hd_w)))
    wukv = mla_w_ukv.reshape(depth, kv_rank, MLA_HEADS, MLA_NOPE + MLA_DV)
    p = {
        "gains": norm_gains.reshape(depth * n_sub, 1, D),
        "ffn_wg": ffn_w_gate.astype(BF16), "ffn_wu": ffn_w_up.astype(BF16),
        "ffn_wd": ffn_w_down.astype(BF16),
        "w_in": w_in.astype(BF16),
        "wm": jnp.pad(w_in[:, :, m_lo:m_lo + m_w],
                      ((0, 0), (0, 0), (0, LANES - MLA_ROPE))).astype(BF16),
        "wgates": w_in[:, :, m_lo + m_w:].astype(BF16),
        "qn": mla_q_norm.reshape(depth, 1, q_rank), "kvn": mla_kv_norm.reshape(depth, 1, kv_rank),
        "wuq": wuq.reshape(depth, q_rank, MLA_HEADS * MLA_HEAD_PAD).astype(BF16),
        "wuk": wukv[..., :MLA_NOPE].reshape(depth, kv_rank, MLA_HEADS * MLA_NOPE).astype(BF16),
        "wuv": wukv[..., MLA_NOPE:].reshape(depth, kv_rank, MLA_HEADS * MLA_DV).astype(BF16),
        "ret_wo": ret_w_o.astype(BF16), "glu_a": s5_glu_a.astype(BF16),
        "glu_b": s5_glu_b.astype(BF16), "mla_wo": mla_w_o.astype(BF16),
        "w_out": w_out.astype(BF16),
    }
    p.update(_s5_params(s5_a_re, s5_a_im, s5_log_dt, s5_b_re, s5_b_im, s5_c_re, s5_c_im, s5_d))
    return p


def kernel(x, positions, norm_gains, ffn_w_gate, ffn_w_up, ffn_w_down, w_in, ret_w_o,
           s5_a_re, s5_a_im, s5_log_dt, s5_b_re, s5_b_im, s5_c_re, s5_c_im, s5_d,
           s5_glu_a, s5_glu_b, mla_q_norm, mla_kv_norm, mla_w_uq, mla_w_ukv, mla_w_o, w_out):
    B, L, D = x.shape
    depth = norm_gains.shape[0]
    assert L % ROW_TILE == 0 and L % RET_ROWS == 0 and L % ATTN_TQ == 0 and L % S5_TL == 0
    assert B == SUBLANES, "the S5 scan keeps the batch on the sublane axis"

    p = _prepare(norm_gains, ffn_w_gate, ffn_w_up, ffn_w_down, w_in, ret_w_o,
                 s5_a_re, s5_a_im, s5_log_dt, s5_b_re, s5_b_im, s5_c_re, s5_c_im, s5_d,
                 s5_glu_a, s5_glu_b, mla_q_norm, mla_kv_norm, mla_w_uq, mla_w_ukv, mla_w_o, w_out)
    tabs = _rope_tables(positions)
    ret_tabs = _ret_tables()
    x2 = x.reshape(B * L, D)
    for l in range(depth):
        x2 = _ffn(x2, p, l, 0)
        rq, rk, rv, rg, u, mq, mk, mv = _proj(x2, p, l, tabs)
        a_ret = _retention(rq, rk, rv, rg, ret_tabs, B, L)
        y_s5 = _s5(u, p, l, B, L)
        o_mla = _attention(mq, mk, mv, B, L)
        x2 = _merge(x2, a_ret.reshape(B * L, -1), y_s5.reshape(B * L, -1),
                    o_mla.reshape(B * L, -1), p, l)
        x2 = _ffn(x2, p, l, 1)
    return x2.reshape(B, L, D)
```

```python
import math

import jax
import jax.numpy as jnp
from jax import lax
from jax.experimental import pallas as pl
from jax.experimental.pallas import tpu as pltpu

F32 = jnp.float32
BF16 = jnp.bfloat16

ROPE_BASE = 10000.0
NORM_EPS = 1e-6
GN_EPS = 1e-5
FFN_RES = 0.5
RET_HEADS = 4
RET_DK = 128
RET_DV = 128
RET_CHUNK = 128
MLA_HEADS = 4
MLA_NOPE = 128
MLA_ROPE = 64
MLA_DV = 128
MLA_HEAD_PAD = 256

LANES = 128
SUBLANES = 8
VMEM_LIMIT = 56 * 1024 * 1024

ROW_TILE = 512
FFN_CHUNKS = ((0, 1024), (1024, 2048), (2048, 2816))
RET_ROWS = 512
S5_TL = 64
S5_HALF = 1024
ATTN_TQ = 512
ATTN_HEADS_PER_STEP = 4


def _cparams(sem):
    return pltpu.CompilerParams(dimension_semantics=sem, vmem_limit_bytes=VMEM_LIMIT)


def _const_spec(shape):
    nd = len(shape)
    return pl.BlockSpec(shape, lambda *_: (0,) * nd, pipeline_mode=pl.Buffered(1))


def _slab_spec(arr, lead, block=None, col=0):
    tail = tuple(arr.shape[len(lead):]) if block is None else tuple(block)
    idx = tuple(lead) + (0,) * (len(tail) - 1) + (col,)
    return pl.BlockSpec((None,) * len(lead) + tail, lambda *_: idx,
                        pipeline_mode=pl.Buffered(1))


def _rms(x, g):
    y = x * lax.rsqrt(jnp.mean(x * x, axis=-1, keepdims=True) + NORM_EPS)
    return y * g


def _dot(a, b):
    return jnp.dot(a, b, preferred_element_type=F32)


def _rope_kernel(pos_ref, f_ref, a_ref, na_ref, b_ref, p_ref, np_ref, q_ref,
                 cr_ref, sr_ref, cm_ref, s1_ref, s2_ref):
    ang = pos_ref[...] * f_ref[...]
    c = jnp.cos(ang)
    s = jnp.sin(ang)
    hr = RET_DK // 2
    c_hi = pltpu.roll(c, hr, 1)
    s_hi = pltpu.roll(s, hr, 1)
    c_m2 = pltpu.roll(c, hr + MLA_ROPE // 2, 1)
    s_m2 = pltpu.roll(s, hr + MLA_ROPE // 2, 1)
    cr_ref[...] = c * a_ref[...] + c_hi * b_ref[...]
    sr_ref[...] = s * na_ref[...] + s_hi * b_ref[...]
    cm_ref[...] = c_hi * p_ref[...] + c_m2 * q_ref[...]
    s1_ref[...] = s_hi * np_ref[...]
    s2_ref[...] = s_m2 * q_ref[...]


def _rope_tables(positions):
    B, L = positions.shape
    T = B * L
    pos = positions.astype(F32).reshape(T, 1)
    inv_r = ROPE_BASE ** (-jnp.arange(0, RET_DK, 2, dtype=F32) / RET_DK)
    inv_m = ROPE_BASE ** (-jnp.arange(0, MLA_ROPE, 2, dtype=F32) / MLA_ROPE)
    hr = RET_DK // 2
    hm = MLA_ROPE // 2
    lane = jnp.arange(LANES)
    freq = jnp.concatenate([inv_r, inv_m, jnp.zeros((LANES - hr - hm,), F32)])
    a = (lane < hr).astype(F32)
    b = (lane >= hr).astype(F32)
    p = (lane < hm).astype(F32)
    q = ((lane >= hm) & (lane < 2 * hm)).astype(F32)
    rows = [r.reshape(1, LANES) for r in (freq, a, -a, b, p, -p, q)]
    tm = 2048
    row_spec = pl.BlockSpec((1, LANES), lambda i: (0, 0))
    tab_spec = pl.BlockSpec((tm, LANES), lambda i: (i, 0))
    tab_shape = jax.ShapeDtypeStruct((T, LANES), F32)
    return pl.pallas_call(
        _rope_kernel,
        grid=(T // tm,),
        in_specs=[pl.BlockSpec((tm, 1), lambda i: (i, 0))] + [row_spec] * len(rows),
        out_specs=[tab_spec] * 5,
        out_shape=[tab_shape] * 5,
        compiler_params=_cparams(("parallel",)),
        name="rope_tables",
    )(pos, *rows)


def _ffn_kernel(x_ref, xn_ref, gpre_ref, gpost_ref, wg_ref, wu_ref, wd_ref, o_ref, ha_ref):
    half = x_ref.shape[0] // 2
    gpre = gpre_ref[...]
    gpost = gpost_ref[...]

    @pl.when(pl.program_id(0) == 0)
    def _():
        ha_ref[...] = _rms(x_ref[:half, :], gpre).astype(BF16)

    def swiglu(h):
        y = None
        pending = None
        for c0, c1 in FFN_CHUNKS + ((None, None),):
            if c0 is not None:
                g = _dot(h, wg_ref[:, c0:c1])
                u = _dot(h, wu_ref[:, c0:c1])
            if pending is not None:
                a, p0, p1 = pending
                part = _dot(a, wd_ref[p0:p1, :])
                y = part if y is None else y + part
            if c0 is not None:
                pending = ((g * jax.nn.sigmoid(g) * u).astype(BF16), c0, c1)
        return y

    ya = swiglu(ha_ref[...])
    yb = swiglu(_rms(x_ref[half:, :], gpre).astype(BF16))
    o_ref[:half, :] = x_ref[:half, :] + FFN_RES * _rms(ya, gpost)
    ha_ref[...] = _rms(xn_ref[...], gpre).astype(BF16)
    o_ref[half:, :] = x_ref[half:, :] + FFN_RES * _rms(yb, gpost)


def _ffn(x2, p, l, k):
    T, D = x2.shape
    half = ROW_TILE
    n = T // (2 * half)
    row = pl.BlockSpec((2 * half, D), lambda i: (i, 0))
    nxt = pl.BlockSpec((half, D), lambda i: (jnp.minimum(2 * i + 2, 2 * n - 2), 0))
    n_sub = p["gains"].shape[0] // p["ffn_wg"].shape[0]
    gi = l * n_sub + 4 * k
    return pl.pallas_call(
        _ffn_kernel,
        grid=(n,),
        in_specs=[row, nxt, _slab_spec(p["gains"], (gi,)), _slab_spec(p["gains"], (gi + 1,)),
                  _slab_spec(p["ffn_wg"], (l, k)), _slab_spec(p["ffn_wu"], (l, k)),
                  _slab_spec(p["ffn_wd"], (l, k))],
        out_specs=row,
        out_shape=jax.ShapeDtypeStruct((T, D), F32),
        scratch_shapes=[pltpu.VMEM((half, D), BF16)],
        compiler_params=_cparams(("arbitrary",)),
        name="ffn",
    )(x2, x2, p["gains"], p["gains"], p["ffn_wg"], p["ffn_wu"], p["ffn_wd"])


def _rope_ret(v, cos, sin):
    return v * cos + pltpu.roll(v, RET_DK // 2, 1) * sin


def _rope_mla(v, cos, sin1, sin2):
    hm = MLA_ROPE // 2
    return v * cos + pltpu.roll(v, LANES - hm, 1) * sin1 + pltpu.roll(v, hm, 1) * sin2


def _proj_kernel(x_ref, xn_ref, g_ref, wr_ref, wu_ref, wm_ref, qn_ref, kvn_ref,
                 wuq_ref, wuk_ref, wuv_ref, cr_ref, sr_ref, cm_ref, s1_ref, s2_ref,
                 rq_ref, rk_ref, rv_ref, rg_ref, u_ref, mq_ref, mk_ref, mv_ref, ha_ref):
    half = x_ref.shape[0] // 2
    gain = g_ref[...]
    qk_w = RET_HEADS * RET_DK
    v_w = RET_HEADS * RET_DV
    q_rank = qn_ref.shape[1]
    kv_rank = kvn_ref.shape[1]

    @pl.when(pl.program_id(0) == 0)
    def _():
        ha_ref[...] = _rms(x_ref[:half, :], gain).astype(BF16)

    def project(h):
        m = _dot(h, wm_ref[...])
        r = _dot(h, wr_ref[...])
        u = _dot(h, wu_ref[...])
        return m, r, u

    def mla_up(m):
        cq = _rms(m[:, :q_rank], qn_ref[...]).astype(BF16)
        ckv = _rms(m[:, q_rank:q_rank + kv_rank], kvn_ref[...]).astype(BF16)
        q = _dot(cq, wuq_ref[...])
        kn = _dot(ckv, wuk_ref[...])
        v = _dot(ckv, wuv_ref[...])
        return q, kn, v

    def store(rows, m, r, u, q, kn, v):
        cr = cr_ref[rows, :]
        sr = sr_ref[rows, :]
        for hd in range(RET_HEADS):
            lo = hd * RET_DK
            rq_ref[rows, lo:lo + RET_DK] = _rope_ret(r[:, lo:lo + RET_DK], cr, sr).astype(BF16)
            rk_ref[rows, lo:lo + RET_DK] = (_rope_ret(r[:, qk_w + lo:qk_w + lo + RET_DK], cr, sr)
                                            * (RET_DK ** -0.5)).astype(BF16)
        rv_ref[rows, :] = r[:, 2 * qk_w:2 * qk_w + v_w].astype(BF16)
        rg_ref[rows, :] = r[:, 2 * qk_w + v_w:].astype(BF16)
        u_ref[rows, :] = u
        cm = cm_ref[rows, :]
        s1 = s1_ref[rows, :]
        s2 = s2_ref[rows, :]
        kpe = _rope_mla(m[:, q_rank + kv_rank:], cm, s1, s2).astype(BF16)
        mv_ref[rows, :] = v.astype(BF16)
        for hd in range(MLA_HEADS):
            lo = hd * MLA_HEAD_PAD
            mq_ref[rows, lo:lo + MLA_NOPE] = q[:, lo:lo + MLA_NOPE].astype(BF16)
            mq_ref[rows, lo + MLA_NOPE:lo + MLA_HEAD_PAD] = _rope_mla(
                q[:, lo + MLA_NOPE:lo + MLA_HEAD_PAD], cm, s1, s2).astype(BF16)
            mk_ref[rows, lo:lo + MLA_NOPE] = kn[:, hd * MLA_NOPE:(hd + 1) * MLA_NOPE].astype(BF16)
            mk_ref[rows, lo + MLA_NOPE:lo + MLA_HEAD_PAD] = kpe

    pa = project(ha_ref[...])
    pb = project(_rms(x_ref[half:, :], gain).astype(BF16))
    ua = mla_up(pa[0])
    ub = mla_up(pb[0])
    store(slice(0, half), *pa, *ua)
    ha_ref[...] = _rms(xn_ref[...], gain).astype(BF16)
    store(slice(half, 2 * half), *pb, *ub)


def _proj(x2, p, l, tabs):
    T, D = x2.shape
    tm = ROW_TILE
    n = T // tm
    row = lambda n: pl.BlockSpec((tm, n), lambda i: (i, 0))
    nxt = pl.BlockSpec((tm // 2, D), lambda i: (jnp.minimum(2 * i + 2, 2 * n - 2), 0))
    n_sub = p["gains"].shape[0] // p["w_in"].shape[0]
    ret_w = 2 * RET_HEADS * RET_DK + 2 * RET_HEADS * RET_DV
    s5_w = p["s5_d"].shape[-1]
    assert ret_w % s5_w == 0
    consts = [(p["gains"], _slab_spec(p["gains"], (l * n_sub + 2,))),
              (p["w_in"], _slab_spec(p["w_in"], (l,), block=(D, ret_w), col=0)),
              (p["w_in"], _slab_spec(p["w_in"], (l,), block=(D, s5_w), col=ret_w // s5_w)),
              (p["wm"], _slab_spec(p["wm"], (l,))),
              (p["qn"], _slab_spec(p["qn"], (l,))), (p["kvn"], _slab_spec(p["kvn"], (l,))),
              (p["wuq"], _slab_spec(p["wuq"], (l,))), (p["wuk"], _slab_spec(p["wuk"], (l,))),
              (p["wuv"], _slab_spec(p["wuv"], (l,)))]
    out_w = [(RET_HEADS * RET_DK, BF16), (RET_HEADS * RET_DK, BF16),
             (RET_HEADS * RET_DV, BF16), (RET_HEADS * RET_DV, BF16), (s5_w, F32),
             (MLA_HEADS * MLA_HEAD_PAD, BF16), (MLA_HEADS * MLA_HEAD_PAD, BF16),
             (MLA_HEADS * MLA_DV, BF16)]
    return pl.pallas_call(
        _proj_kernel,
        grid=(n,),
        in_specs=[row(D), nxt] + [s for _, s in consts] + [row(LANES)] * 5,
        out_specs=[row(w) for w, _ in out_w],
        out_shape=[jax.ShapeDtypeStruct((T, w), dt) for w, dt in out_w],
        scratch_shapes=[pltpu.VMEM((tm // 2, D), BF16)],
        compiler_params=_cparams(("arbitrary",)),
        name="mixer_proj",
    )(x2, x2, *[a for a, _ in consts], *tabs)


def _ret_kernel(q_ref, k_ref, v_ref, g_ref, intra_ref, qd_ref, kd_ref, cd_ref, o_ref, st_ref):
    @pl.when(pl.program_id(1) == 0)
    def _():
        st_ref[...] = jnp.zeros_like(st_ref)

    C = RET_CHUNK
    nt = (((1,), (1,)), ((), ()))
    tn = (((0,), (0,)), ((), ()))
    for hd in range(RET_HEADS):
        cols = slice(hd * RET_DK, (hd + 1) * RET_DK)
        vcols = slice(hd * RET_DV, (hd + 1) * RET_DV)
        intra = intra_ref[hd]
        qd = qd_ref[hd]
        kd = kd_ref[hd]
        cd = cd_ref[hd]
        state = st_ref[hd]
        for c in range(RET_ROWS // C):
            rows = slice(c * C, (c + 1) * C)
            q = q_ref[0, rows, cols]
            k = k_ref[0, rows, cols]
            v = v_ref[0, rows, vcols]
            scores = lax.dot_general(q, k, nt, preferred_element_type=F32) * intra
            inner = _dot(scores.astype(BF16), v)
            cross = _dot((q.astype(F32) * qd).astype(BF16), state.astype(BF16))
            kv = lax.dot_general((k.astype(F32) * kd).astype(BF16), v, tn,
                                 preferred_element_type=F32)
            state = cd * state + kv
            o = inner + cross
            mu = jnp.mean(o, axis=-1, keepdims=True)
            var = jnp.mean(jnp.square(o - mu), axis=-1, keepdims=True)
            on = (o - mu) * lax.rsqrt(var + GN_EPS)
            g = g_ref[0, rows, vcols].astype(F32)
            o_ref[0, rows, vcols] = (g * jax.nn.sigmoid(g) * on).astype(BF16)
        st_ref[hd] = state


def _ret_tables():
    C = RET_CHUNK
    log_gamma = jnp.log1p(-jnp.exp2(-5.0 - jnp.arange(RET_HEADS, dtype=F32)))
    pos = jnp.arange(C, dtype=F32)
    rel = pos[:, None] - pos[None, :]
    intra = jnp.where(rel[None] >= 0.0,
                      jnp.exp(jnp.maximum(rel, 0.0)[None] * log_gamma[:, None, None]), 0.0)
    k_decay = jnp.exp((C - 1.0 - pos)[:, None] * log_gamma[None, :])
    q_decay = jnp.exp((pos + 1.0)[:, None] * log_gamma[None, :])
    chunk_decay = jnp.exp(C * log_gamma)
    qd = jnp.broadcast_to(q_decay.T[:, :, None], (RET_HEADS, C, RET_DK))
    kd = jnp.broadcast_to(k_decay.T[:, :, None], (RET_HEADS, C, RET_DK))
    cd = jnp.broadcast_to(chunk_decay[:, None, None], (RET_HEADS, 1, RET_DV))
    return intra, qd, kd, cd


def _retention(rq, rk, rv, rg, tables, B, L):
    W = rq.shape[-1]
    shp = (B, L, W)
    blk = pl.BlockSpec((1, RET_ROWS, W), lambda b, i: (b, i, 0))
    return pl.pallas_call(
        _ret_kernel,
        grid=(B, L // RET_ROWS),
        in_specs=[blk] * 4 + [_const_spec(t.shape) for t in tables],
        out_specs=blk,
        out_shape=jax.ShapeDtypeStruct(shp, BF16),
        scratch_shapes=[pltpu.VMEM((RET_HEADS, RET_DK, RET_DV), F32)],
        compiler_params=_cparams(("parallel", "arbitrary")),
        name="retention",
    )(rq.reshape(shp), rk.reshape(shp), rv.reshape(shp), rg.reshape(shp), *tables)


def _s5_zoh_kernel(are_ref, aim_ref, ldt_ref, bre_ref, bim_ref,
                   abr_ref, abi_ref, bbr_ref, bbi_ref):
    a_re = are_ref[...]
    a_im = aim_ref[...]
    dt = jnp.exp(ldt_ref[...])
    mag = jnp.exp(a_re * dt)
    abar_re = mag * jnp.cos(a_im * dt)
    abar_im = mag * jnp.sin(a_im * dt)
    den = a_re * a_re + a_im * a_im
    nr = abar_re - 1.0
    f_re = (nr * a_re + abar_im * a_im) / den
    f_im = (abar_im * a_re - nr * a_im) / den
    b_re = bre_ref[...]
    b_im = bim_ref[...]
    abr_ref[...] = abar_re
    abi_ref[...] = abar_im
    bbr_ref[...] = f_re * b_re - f_im * b_im
    bbi_ref[...] = f_re * b_im + f_im * b_re


def _s5_params(a_re, a_im, log_dt, b_re, b_im, c_re, c_im, d):
    Dp, G, P = a_re.shape
    H = b_re.shape[-1]
    R = Dp * G * H
    rep = lambda a: jnp.broadcast_to(a[:, :, None, :], (Dp, G, H, a.shape[-1])).reshape(R, -1)
    bt = lambda b: jnp.swapaxes(b, 2, 3).reshape(R, P)
    args = [rep(a_re), rep(a_im), rep(log_dt[:, :, None]), bt(b_re), bt(b_im)]
    out = jax.ShapeDtypeStruct((R, P), F32)
    abr, abi, bbr, bbi = pl.pallas_call(
        _s5_zoh_kernel, out_shape=[out] * 4, name="s5_discretise",
        compiler_params=pltpu.CompilerParams(vmem_limit_bytes=VMEM_LIMIT),
    )(*args)
    abar_re = abr.reshape(Dp, G, H, P)[:, :, 0, :].reshape(Dp, 1, G * P)
    abar_im = abi.reshape(Dp, G, H, P)[:, :, 0, :].reshape(Dp, 1, G * P)
    gpb = LANES // H
    nblk = G // gpb
    eye = jnp.eye(gpb, dtype=F32)

    def pack_in(bb):
        bb = bb.reshape(Dp, nblk, gpb, H, P)
        m = jnp.einsum("dkghp,gf->dkghfp", bb, eye)
        return m.reshape(Dp, nblk, gpb * H, gpb * P).astype(BF16)

    def pack_out(cc):
        cc = cc.reshape(Dp, nblk, gpb, H, P)
        m = jnp.einsum("dkghp,gf->dkgpfh", cc, eye)
        return m.reshape(Dp, nblk, gpb * P, gpb * H).astype(BF16)

    return {"s5_are": abar_re, "s5_aim": abar_im,
            "s5_wbr": pack_in(bbr), "s5_wbi": pack_in(bbi),
            "s5_wcr": pack_out(c_re), "s5_wci": pack_out(c_im),
            "s5_d": d.reshape(Dp, 1, G * H)}


def _s5_kernel(u_ref, are_ref, aim_ref, wbr_ref, wbi_ref, wcr_ref, wci_ref, d_ref, y_ref,
               sre_ref, sim_ref, u2_ref, ut_ref, bre_ref, bim_ref, yt_ref):
    B, tl, W = u_ref.shape
    nblk = wbr_ref.shape[0]
    in_w = wbr_ref.shape[1]
    st_w = wbr_ref.shape[2]

    @pl.when(pl.program_id(0) == 0)
    def _():
        sre_ref[...] = jnp.zeros_like(sre_ref)
        sim_ref[...] = jnp.zeros_like(sim_ref)

    assert in_w == LANES and nblk * LANES == W

    def gather(kb):
        ch = slice(kb * LANES, (kb + 1) * LANES)
        for b in range(B):
            u2_ref[kb, b * tl:(b + 1) * tl, :] = u_ref[b, :, ch]
        for t in range(tl):
            ut_ref[t * B:(t + 1) * B, ch] = u2_ref[kb, pl.ds(t, B, stride=tl), :]

    def emit(kb):
        for b in range(B):
            y_ref[b, :, kb * LANES:(kb + 1) * LANES] = yt_ref[kb, pl.ds(b, tl, stride=B), :]

    def drive(kb):
        sl = slice(kb * st_w, (kb + 1) * st_w)
        lhs = ut_ref[:, kb * in_w:(kb + 1) * in_w].astype(BF16)
        bre_ref[:, sl] = _dot(lhs, wbr_ref[kb])
        bim_ref[:, sl] = _dot(lhs, wbi_ref[kb])

    def scan(kb):
        sl = slice(kb * st_w, (kb + 1) * st_w)
        ar = jnp.broadcast_to(are_ref[:, sl], (B, st_w))
        ai = jnp.broadcast_to(aim_ref[:, sl], (B, st_w))
        sr = sre_ref[:, sl]
        si = sim_ref[:, sl]
        for t in range(tl):
            rows = slice(t * B, (t + 1) * B)
            sr, si = (ar * sr - ai * si + bre_ref[rows, sl],
                      ar * si + ai * sr + bim_ref[rows, sl])
            bre_ref[rows, sl] = sr
            bim_ref[rows, sl] = si
        sre_ref[:, sl] = sr
        sim_ref[:, sl] = si

    def project(kb):
        sl = slice(kb * st_w, (kb + 1) * st_w)
        ch = slice(kb * in_w, (kb + 1) * in_w)
        y = (_dot(bre_ref[:, sl].astype(BF16), wcr_ref[kb])
             - _dot(bim_ref[:, sl].astype(BF16), wci_ref[kb]))
        yt_ref[kb] = jax.nn.gelu(y + d_ref[:, ch] * ut_ref[:, ch])

    gather(0)
    drive(0)
    for kb in range(nblk):
        if kb + 1 < nblk:
            gather(kb + 1)
            drive(kb + 1)
        scan(kb)
        if kb >= 1:
            project(kb - 1)
            emit(kb - 1)
    project(nblk - 1)
    emit(nblk - 1)


def _s5(u, p, l, B, L):
    W = u.shape[-1]
    tl = S5_TL
    n_state = p["s5_are"].shape[-1]
    blk = pl.BlockSpec((B, tl, W), lambda i: (0, i, 0))
    names = ["s5_are", "s5_aim", "s5_wbr", "s5_wbi", "s5_wcr", "s5_wci", "s5_d"]
    return pl.pallas_call(
        _s5_kernel,
        grid=(L // tl,),
        in_specs=[blk] + [_slab_spec(p[n], (l,)) for n in names],
        out_specs=blk,
        out_shape=jax.ShapeDtypeStruct((B, L, W), F32),
        scratch_shapes=[pltpu.VMEM((B, n_state), F32), pltpu.VMEM((B, n_state), F32),
                        pltpu.VMEM((W // LANES, B * tl, LANES), F32), pltpu.VMEM((B * tl, W), F32),
                        pltpu.VMEM((B * tl, n_state), F32), pltpu.VMEM((B * tl, n_state), F32),
                        pltpu.VMEM((W // LANES, B * tl, LANES), F32)],
        compiler_params=_cparams(("arbitrary",)),
        name="s5_scan",
    )(u.reshape(B, L, W), *[p[n] for n in names])


def _attn_kernel(q_ref, k_ref, v_ref, o_ref, s_ref, m_ref, mprev_ref, l_ref, acc_ref, vt_ref,
                 qt_ref):
    tq = q_ref.shape[1]
    nh = m_ref.shape[0]
    qi = pl.program_id(2)
    c = (MLA_NOPE + MLA_ROPE) ** -0.5 * math.log2(math.e)
    m_ref[...] = jnp.full(m_ref.shape, -1e30, F32)
    l_ref[...] = jnp.zeros_like(l_ref)
    acc_ref[...] = jnp.zeros_like(acc_ref)
    for h in range(nh):
        qt_ref[h] = q_ref[0, :, h * MLA_HEAD_PAD:(h + 1) * MLA_HEAD_PAD].T

    def scores(j, h, masked):
        r0 = pl.multiple_of(j * tq, tq)
        qk = slice(h * MLA_HEAD_PAD, (h + 1) * MLA_HEAD_PAD)
        st = _dot(k_ref[0, pl.ds(r0, tq), qk], qt_ref[h])
        if masked:
            kidx = lax.broadcasted_iota(jnp.int32, st.shape, 0)
            qidx = lax.broadcasted_iota(jnp.int32, st.shape, 1)
            st = jnp.where(kidx <= qidx, st, -jnp.inf)
        s_ref[h] = st
        m_old = m_ref[h]
        mprev_ref[h] = m_old
        m_ref[h] = jnp.maximum(m_old, jnp.max(st, axis=0, keepdims=True))

    @pl.when(qi == 0)
    def _():
        for h in range(nh):
            for blk in range(v_ref.shape[1] // tq):
                rows = slice(blk * tq, (blk + 1) * tq)
                vt_ref[h, :, rows] = v_ref[0, rows, h * MLA_DV:(h + 1) * MLA_DV].T

    def accumulate(j, h):
        r0 = pl.multiple_of(j * tq, tq)
        m_new = m_ref[h]
        alpha = jnp.exp2((mprev_ref[h] - m_new) * c)
        pt = jnp.exp2((s_ref[h] - m_new) * c)
        l_ref[h] = alpha * l_ref[h] + jnp.sum(pt, axis=0, keepdims=True)
        acc_ref[h] = alpha * acc_ref[h] + _dot(vt_ref[h, :, pl.ds(r0, tq)],
                                               pt.astype(BF16))

    scores(qi, 0, True)
    for h in range(1, nh):
        scores(qi, h, True)
        accumulate(qi, h - 1)

    def body(j, carry):
        scores(j, 0, False)
        accumulate(jnp.where(j == 0, qi, j - 1), nh - 1)
        for h in range(1, nh):
            scores(j, h, False)
            accumulate(j, h - 1)
        return carry
    lax.fori_loop(0, qi, body, 0)
    accumulate(jnp.maximum(qi - 1, 0), nh - 1)
    for h in range(nh):
        o_ref[0, :, h * MLA_DV:(h + 1) * MLA_DV] = (acc_ref[h] / l_ref[h]).T.astype(BF16)


def _attention(mq, mk, mv, B, L):
    tq = ATTN_TQ
    nh = ATTN_HEADS_PER_STEP
    qw = nh * MLA_HEAD_PAD
    vw = nh * MLA_DV
    return pl.pallas_call(
        _attn_kernel,
        grid=(B, MLA_HEADS // nh, L // tq),
        in_specs=[pl.BlockSpec((1, tq, qw), lambda b, h, i: (b, i, h)),
                  pl.BlockSpec((1, L, qw), lambda b, h, i: (b, 0, h)),
                  pl.BlockSpec((1, L, vw), lambda b, h, i: (b, 0, h))],
        out_specs=pl.BlockSpec((1, tq, vw), lambda b, h, i: (b, i, h)),
        out_shape=jax.ShapeDtypeStruct((B, L, MLA_HEADS * MLA_DV), BF16),
        scratch_shapes=[pltpu.VMEM((nh, tq, tq), F32), pltpu.VMEM((nh, 1, tq), F32),
                        pltpu.VMEM((nh, 1, tq), F32), pltpu.VMEM((nh, 1, tq), F32),
                        pltpu.VMEM((nh, MLA_DV, tq), F32), pltpu.VMEM((nh, MLA_DV, L), BF16),
                        pltpu.VMEM((nh, MLA_HEAD_PAD, tq), BF16)],
        compiler_params=_cparams(("parallel", "parallel", "arbitrary")),
        name="mla_attention",
    )(mq.reshape(B, L, -1), mk.reshape(B, L, -1), mv.reshape(B, L, -1))


def _merge_kernel(x_ref, xn_ref, gin_ref, gout_ref, a_ref, y_ref, o_ref_in, wg_ref, wro_ref,
                  wga_ref, wgb_ref, wmo_ref, wout_ref, out_ref, ha_ref):
    half = x_ref.shape[0] // 2
    D = x_ref.shape[1]
    gin = gin_ref[...]
    gout = gout_ref[...]

    @pl.when(pl.program_id(0) == 0)
    def _():
        ha_ref[...] = _rms(x_ref[:half, :], gin).astype(BF16)

    def branches(h, rows):
        ys = y_ref[rows, :].astype(BF16)
        return (_dot(h, wg_ref[:, 0:D]), _dot(a_ref[rows, :], wro_ref[...]),
                _dot(h, wg_ref[:, D:2 * D]), _dot(ys, wga_ref[...]), _dot(ys, wgb_ref[...]),
                _dot(h, wg_ref[:, 2 * D:3 * D]), _dot(o_ref_in[rows, :], wmo_ref[...]))

    def combine(g0, y_ret, g1, ya, yb, g2, y_mla):
        merged = jax.nn.sigmoid(g0) * y_ret
        merged = merged + jax.nn.sigmoid(g1) * (ya * jax.nn.sigmoid(yb))
        merged = merged + jax.nn.sigmoid(g2) * y_mla
        return merged.astype(BF16)

    rows_a = slice(0, half)
    rows_b = slice(half, 2 * half)
    ba = branches(ha_ref[...], rows_a)
    bb = branches(_rms(x_ref[rows_b, :], gin).astype(BF16), rows_b)
    oa = _dot(combine(*ba), wout_ref[...])
    ob = _dot(combine(*bb), wout_ref[...])
    out_ref[rows_a, :] = x_ref[rows_a, :] + _rms(oa, gout)
    ha_ref[...] = _rms(xn_ref[...], gin).astype(BF16)
    out_ref[rows_b, :] = x_ref[rows_b, :] + _rms(ob, gout)


def _merge(x2, a_ret, y_s5, o_mla, p, l):
    T, D = x2.shape
    tm = ROW_TILE
    n = T // tm
    row = lambda w: pl.BlockSpec((tm, w), lambda i: (i, 0))
    nxt = pl.BlockSpec((tm // 2, D), lambda i: (jnp.minimum(2 * i + 2, 2 * n - 2), 0))
    n_sub = p["gains"].shape[0] // p["wgates"].shape[0]
    names = ["wgates", "ret_wo", "glu_a", "glu_b", "mla_wo", "w_out"]
    return pl.pallas_call(
        _merge_kernel,
        grid=(n,),
        in_specs=[row(D), nxt, _slab_spec(p["gains"], (l * n_sub + 2,)),
                  _slab_spec(p["gains"], (l * n_sub + 3,)),
                  row(a_ret.shape[1]), row(y_s5.shape[1]), row(o_mla.shape[1])]
                 + [_slab_spec(p[n], (l,)) for n in names],
        out_specs=row(D),
        out_shape=jax.ShapeDtypeStruct((T, D), F32),
        scratch_shapes=[pltpu.VMEM((tm // 2, D), BF16)],
        compiler_params=_cparams(("arbitrary",)),
        name="gated_merge",
    )(x2, x2, p["gains"], p["gains"], a_ret, y_s5, o_mla, *[p[n] for n in names])


def _prepare(norm_gains, ffn_w_gate, ffn_w_up, ffn_w_down, w_in, ret_w_o,
             s5_a_re, s5_a_im, s5_log_dt, s5_b_re, s5_b_im, s5_c_re, s5_c_im, s5_d,
             s5_glu_a, s5_glu_b, mla_q_norm, mla_kv_norm, mla_w_uq, mla_w_ukv, mla_w_o, w_out):
    depth, n_sub, D = norm_gains.shape
    q_rank = mla_q_norm.shape[-1]
    kv_rank = mla_kv_norm.shape[-1]
    s5_w = s5_d.shape[-1]
    ret_w = 2 * RET_HEADS * RET_DK + 2 * RET_HEADS * RET_DV
    m_lo = ret_w + s5_w
    m_w = q_rank + kv_rank + MLA_ROPE
    assert w_in.shape[-1] == m_lo + m_w + 3 * D
    hd_w = MLA_NOPE + MLA_ROPE
    wuq = jnp.pad(mla_w_uq.reshape(depth, q_rank, MLA_HEADS, hd_w),
                  ((0, 0), (0, 0), (0, 0), (0, MLA_HEAD_PAD - hd_w)))
    wukv = mla_w_ukv.reshape(depth, kv_rank, MLA_HEADS, MLA_NOPE + MLA_DV)
    p = {
        "gains": norm_gains.reshape(depth * n_sub, 1, D),
        "ffn_wg": ffn_w_gate.astype(BF16), "ffn_wu": ffn_w_up.astype(BF16),
        "ffn_wd": ffn_w_down.astype(BF16),
        "w_in": w_in.astype(BF16),
        "wm": jnp.pad(w_in[:, :, m_lo:m_lo + m_w],
                      ((0, 0), (0, 0), (0, LANES - MLA_ROPE))).astype(BF16),
        "wgates": w_in[:, :, m_lo + m_w:].astype(BF16),
        "qn": mla_q_norm.reshape(depth, 1, q_rank), "kvn": mla_kv_norm.reshape(depth, 1, kv_rank),
        "wuq": wuq.reshape(depth, q_rank, MLA_HEADS * MLA_HEAD_PAD).astype(BF16),
        "wuk": wukv[..., :MLA_NOPE].reshape(depth, kv_rank, MLA_HEADS * MLA_NOPE).astype(BF16),
        "wuv": wukv[..., MLA_NOPE:].reshape(depth, kv_rank, MLA_HEADS * MLA_DV).astype(BF16),
        "ret_wo": ret_w_o.astype(BF16), "glu_a": s5_glu_a.astype(BF16),
        "glu_b": s5_glu_b.astype(BF16), "mla_wo": mla_w_o.astype(BF16),
        "w_out": w_out.astype(BF16),
    }
    p.update(_s5_params(s5_a_re, s5_a_im, s5_log_dt, s5_b_re, s5_b_im, s5_c_re, s5_c_im, s5_d))
    return p


def kernel(x, positions, norm_gains, ffn_w_gate, ffn_w_up, ffn_w_down, w_in, ret_w_o,
           s5_a_re, s5_a_im, s5_log_dt, s5_b_re, s5_b_im, s5_c_re, s5_c_im, s5_d,
           s5_glu_a, s5_glu_b, mla_q_norm, mla_kv_norm, mla_w_uq, mla_w_ukv, mla_w_o, w_out):
    B, L, D = x.shape
    depth = norm_gains.shape[0]
    assert L % ROW_TILE == 0 and L % RET_ROWS == 0 and L % ATTN_TQ == 0 and L % S5_TL == 0
    assert B == SUBLANES, "the S5 scan keeps the batch on the sublane axis"

    p = _prepare(norm_gains, ffn_w_gate, ffn_w_up, ffn_w_down, w_in, ret_w_o,
                 s5_a_re, s5_a_im, s5_log_dt, s5_b_re, s5_b_im, s5_c_re, s5_c_im, s5_d,
                 s5_glu_a, s5_glu_b, mla_q_norm, mla_kv_norm, mla_w_uq, mla_w_ukv, mla_w_o, w_out)
    tabs = _rope_tables(positions)
    ret_tabs = _ret_tables()
    x2 = x.reshape(B * L, D)
    for l in range(depth):
        x2 = _ffn(x2, p, l, 0)
        rq, rk, rv, rg, u, mq, mk, mv = _proj(x2, p, l, tabs)
        a_ret = _retention(rq, rk, rv, rg, ret_tabs, B, L)
        y_s5 = _s5(u, p, l, B, L)
        o_mla = _attention(mq, mk, mv, B, L)
        x2 = _merge(x2, a_ret.reshape(B * L, -1), y_s5.reshape(B * L, -1),
                    o_mla.reshape(B * L, -1), p, l)
        x2 = _ffn(x2, p, l, 1)
    return x2.reshape(B, L, D)
```

```python
import math

import jax
import jax.numpy as jnp
from jax import lax
from jax.experimental import pallas as pl
from jax.experimental.pallas import tpu as pltpu

F32 = jnp.float32
BF16 = jnp.bfloat16

ROPE_BASE = 10000.0
NORM_EPS = 1e-6
GN_EPS = 1e-5
FFN_RES = 0.5
RET_HEADS = 4
RET_DK = 128
RET_DV = 128
RET_CHUNK = 128
MLA_HEADS = 4
MLA_NOPE = 128
MLA_ROPE = 64
MLA_DV = 128
MLA_HEAD_PAD = 256
MLA_LOG2_SCALE = (MLA_NOPE + MLA_ROPE) ** -0.5 * math.log2(math.e)

LANES = 128
SUBLANES = 8
VMEM_LIMIT = 56 * 1024 * 1024

ROW_TILE = 512
FFN_CHUNKS = ((0, 1024), (1024, 2048), (2048, 2816))
RET_ROWS = 512
S5_TL = 64
S5_HALF = 1024
ATTN_TQ = 512
ATTN_HEADS_PER_STEP = 4
ATTN_LOOKAHEAD = 2


def _cparams(sem):
    return pltpu.CompilerParams(dimension_semantics=sem, vmem_limit_bytes=VMEM_LIMIT)


def _const_spec(shape):
    nd = len(shape)
    return pl.BlockSpec(shape, lambda *_: (0,) * nd, pipeline_mode=pl.Buffered(1))


def _slab_spec(arr, lead, block=None, col=0):
    tail = tuple(arr.shape[len(lead):]) if block is None else tuple(block)
    idx = tuple(lead) + (0,) * (len(tail) - 1) + (col,)
    return pl.BlockSpec((None,) * len(lead) + tail, lambda *_: idx,
                        pipeline_mode=pl.Buffered(1))


def _rms(x, g):
    y = x * lax.rsqrt(jnp.mean(x * x, axis=-1, keepdims=True) + NORM_EPS)
    return y * g


def _dot(a, b):
    return jnp.dot(a, b, preferred_element_type=F32)


def _rope_kernel(pos_ref, f_ref, a_ref, na_ref, b_ref, p_ref, np_ref, q_ref,
                 cr_ref, sr_ref, cm_ref, s1_ref, s2_ref):
    ang = pos_ref[...] * f_ref[...]
    c = jnp.cos(ang)
    s = jnp.sin(ang)
    hr = RET_DK // 2
    c_hi = pltpu.roll(c, hr, 1)
    s_hi = pltpu.roll(s, hr, 1)
    c_m2 = pltpu.roll(c, hr + MLA_ROPE // 2, 1)
    s_m2 = pltpu.roll(s, hr + MLA_ROPE // 2, 1)
    cr_ref[...] = c * a_ref[...] + c_hi * b_ref[...]
    sr_ref[...] = s * na_ref[...] + s_hi * b_ref[...]
    cm_ref[...] = c_hi * p_ref[...] + c_m2 * q_ref[...]
    s1_ref[...] = s_hi * np_ref[...]
    s2_ref[...] = s_m2 * q_ref[...]


def _rope_tables(positions):
    B, L = positions.shape
    T = B * L
    pos = positions.astype(F32).reshape(T, 1)
    inv_r = ROPE_BASE ** (-jnp.arange(0, RET_DK, 2, dtype=F32) / RET_DK)
    inv_m = ROPE_BASE ** (-jnp.arange(0, MLA_ROPE, 2, dtype=F32) / MLA_ROPE)
    hr = RET_DK // 2
    hm = MLA_ROPE // 2
    lane = jnp.arange(LANES)
    freq = jnp.concatenate([inv_r, inv_m, jnp.zeros((LANES - hr - hm,), F32)])
    a = (lane < hr).astype(F32)
    b = (lane >= hr).astype(F32)
    p = (lane < hm).astype(F32)
    q = ((lane >= hm) & (lane < 2 * hm)).astype(F32)
    rows = [r.reshape(1, LANES) for r in (freq, a, -a, b, p, -p, q)]
    tm = 2048
    row_spec = pl.BlockSpec((1, LANES), lambda i: (0, 0))
    tab_spec = pl.BlockSpec((tm, LANES), lambda i: (i, 0))
    tab_shape = jax.ShapeDtypeStruct((T, LANES), F32)
    return pl.pallas_call(
        _rope_kernel,
        grid=(T // tm,),
        in_specs=[pl.BlockSpec((tm, 1), lambda i: (i, 0))] + [row_spec] * len(rows),
        out_specs=[tab_spec] * 5,
        out_shape=[tab_shape] * 5,
        compiler_params=_cparams(("parallel",)),
        name="rope_tables",
    )(pos, *rows)


def _ffn_kernel(x_ref, xn_ref, gpre_ref, gpost_ref, wg_ref, wu_ref, wd_ref, o_ref, ha_ref):
    half = x_ref.shape[0] // 2
    gpre = gpre_ref[...]
    gpost = gpost_ref[...]

    @pl.when(pl.program_id(0) == 0)
    def _():
        ha_ref[...] = _rms(x_ref[:half, :], gpre).astype(BF16)

    def swiglu(h):
        y = None
        pending = None
        for c0, c1 in FFN_CHUNKS + ((None, None),):
            if c0 is not None:
                g = _dot(h, wg_ref[:, c0:c1])
                u = _dot(h, wu_ref[:, c0:c1])
            if pending is not None:
                a, p0, p1 = pending
                part = _dot(a, wd_ref[p0:p1, :])
                y = part if y is None else y + part
            if c0 is not None:
                pending = ((g * jax.nn.sigmoid(g) * u).astype(BF16), c0, c1)
        return y

    ya = swiglu(ha_ref[...])
    yb = swiglu(_rms(x_ref[half:, :], gpre).astype(BF16))
    o_ref[:half, :] = x_ref[:half, :] + FFN_RES * _rms(ya, gpost)
    ha_ref[...] = _rms(xn_ref[...], gpre).astype(BF16)
    o_ref[half:, :] = x_ref[half:, :] + FFN_RES * _rms(yb, gpost)


def _ffn(x2, p, l, k):
    T, D = x2.shape
    half = ROW_TILE
    n = T // (2 * half)
    row = pl.BlockSpec((2 * half, D), lambda i: (i, 0))
    nxt = pl.BlockSpec((half, D), lambda i: (jnp.minimum(2 * i + 2, 2 * n - 2), 0))
    n_sub = p["gains"].shape[0] // p["ffn_wg"].shape[0]
    gi = l * n_sub + 4 * k
    return pl.pallas_call(
        _ffn_kernel,
        grid=(n,),
        in_specs=[row, nxt, _slab_spec(p["gains"], (gi,)), _slab_spec(p["gains"], (gi + 1,)),
                  _slab_spec(p["ffn_wg"], (l, k)), _slab_spec(p["ffn_wu"], (l, k)),
                  _slab_spec(p["ffn_wd"], (l, k))],
        out_specs=row,
        out_shape=jax.ShapeDtypeStruct((T, D), F32),
        scratch_shapes=[pltpu.VMEM((half, D), BF16)],
        compiler_params=_cparams(("arbitrary",)),
        name="ffn",
    )(x2, x2, p["gains"], p["gains"], p["ffn_wg"], p["ffn_wu"], p["ffn_wd"])


def _rope_ret(v, cos, sin):
    return v * cos + pltpu.roll(v, RET_DK // 2, 1) * sin


def _rope_mla(v, cos, sin1, sin2):
    hm = MLA_ROPE // 2
    return v * cos + pltpu.roll(v, LANES - hm, 1) * sin1 + pltpu.roll(v, hm, 1) * sin2


def _proj_kernel(x_ref, xn_ref, g_ref, wr_ref, wu_ref, wm_ref, qn_ref, kvn_ref,
                 wuq_ref, wuk_ref, wuv_ref, cr_ref, sr_ref, cm_ref, s1_ref, s2_ref,
                 rq_ref, rk_ref, rv_ref, rg_ref, u_ref, mq_ref, mk_ref, mv_ref, ha_ref):
    half = x_ref.shape[0] // 2
    gain = g_ref[...]
    qk_w = RET_HEADS * RET_DK
    v_w = RET_HEADS * RET_DV
    q_rank = qn_ref.shape[1]
    kv_rank = kvn_ref.shape[1]

    @pl.when(pl.program_id(0) == 0)
    def _():
        ha_ref[...] = _rms(x_ref[:half, :], gain).astype(BF16)

    def project(h):
        m = _dot(h, wm_ref[...])
        r = _dot(h, wr_ref[...])
        u = _dot(h, wu_ref[...])
        return m, r, u

    def mla_up(m):
        cq = _rms(m[:, :q_rank], qn_ref[...]).astype(BF16)
        ckv = _rms(m[:, q_rank:q_rank + kv_rank], kvn_ref[...]).astype(BF16)
        q = _dot(cq, wuq_ref[...])
        kn = _dot(ckv, wuk_ref[...])
        v = _dot(ckv, wuv_ref[...])
        return q, kn, v

    def store(rows, m, r, u, q, kn, v):
        cr = cr_ref[rows, :]
        sr = sr_ref[rows, :]
        for hd in range(RET_HEADS):
            lo = hd * RET_DK
            hcols = slice(lo, lo + RET_DK)
            qr = _rope_ret(r[:, lo:lo + RET_DK], cr, sr)
            kr = _rope_ret(r[:, qk_w + lo:qk_w + lo + RET_DK], cr, sr) * (RET_DK ** -0.5)
            rq_ref[rows, hcols] = qr.astype(BF16)
            rk_ref[rows, hcols] = kr.astype(BF16)
        rv_ref[rows, :] = r[:, 2 * qk_w:2 * qk_w + v_w].astype(BF16)
        rg_ref[rows, :] = r[:, 2 * qk_w + v_w:].astype(BF16)
        u_ref[rows, :] = u
        cm = cm_ref[rows, :]
        s1 = s1_ref[rows, :]
        s2 = s2_ref[rows, :]
        kpe = _rope_mla(m[:, q_rank + kv_rank:], cm, s1, s2).astype(BF16)
        mv_ref[rows, :] = v.astype(BF16)
        for hd in range(MLA_HEADS):
            lo = hd * MLA_HEAD_PAD
            mq_ref[rows, lo:lo + MLA_NOPE] = (q[:, lo:lo + MLA_NOPE] * MLA_LOG2_SCALE).astype(BF16)
            mq_ref[rows, lo + MLA_NOPE:lo + MLA_HEAD_PAD] = (_rope_mla(
                q[:, lo + MLA_NOPE:lo + MLA_HEAD_PAD], cm, s1, s2) * MLA_LOG2_SCALE).astype(BF16)
            mk_ref[rows, lo:lo + MLA_NOPE] = kn[:, hd * MLA_NOPE:(hd + 1) * MLA_NOPE].astype(BF16)
            mk_ref[rows, lo + MLA_NOPE:lo + MLA_HEAD_PAD] = kpe

    pa = project(ha_ref[...])
    pb = project(_rms(x_ref[half:, :], gain).astype(BF16))
    ua = mla_up(pa[0])
    ub = mla_up(pb[0])
    store(slice(0, half), *pa, *ua)
    ha_ref[...] = _rms(xn_ref[...], gain).astype(BF16)
    store(slice(half, 2 * half), *pb, *ub)


def _proj(x2, p, l, tabs):
    T, D = x2.shape
    tm = ROW_TILE
    n = T // tm
    row = lambda n: pl.BlockSpec((tm, n), lambda i: (i, 0))
    nxt = pl.BlockSpec((tm // 2, D), lambda i: (jnp.minimum(2 * i + 2, 2 * n - 2), 0))
    n_sub = p["gains"].shape[0] // p["w_in"].shape[0]
    ret_w = 2 * RET_HEADS * RET_DK + 2 * RET_HEADS * RET_DV
    s5_w = p["s5_d"].shape[-1]
    assert ret_w % s5_w == 0
    consts = [(p["gains"], _slab_spec(p["gains"], (l * n_sub + 2,))),
              (p["w_in"], _slab_spec(p["w_in"], (l,), block=(D, ret_w), col=0)),
              (p["w_in"], _slab_spec(p["w_in"], (l,), block=(D, s5_w), col=ret_w // s5_w)),
              (p["wm"], _slab_spec(p["wm"], (l,))),
              (p["qn"], _slab_spec(p["qn"], (l,))), (p["kvn"], _slab_spec(p["kvn"], (l,))),
              (p["wuq"], _slab_spec(p["wuq"], (l,))), (p["wuk"], _slab_spec(p["wuk"], (l,))),
              (p["wuv"], _slab_spec(p["wuv"], (l,)))]
    out_w = [(RET_HEADS * RET_DK, BF16)] * 2 + [
             (RET_HEADS * RET_DV, BF16), (RET_HEADS * RET_DV, BF16), (s5_w, F32),
             (MLA_HEADS * MLA_HEAD_PAD, BF16), (MLA_HEADS * MLA_HEAD_PAD, BF16),
             (MLA_HEADS * MLA_DV, BF16)]
    return pl.pallas_call(
        _proj_kernel,
        grid=(n,),
        in_specs=[row(D), nxt] + [s for _, s in consts] + [row(LANES)] * 5,
        out_specs=[row(w) for w, _ in out_w],
        out_shape=[jax.ShapeDtypeStruct((T, w), dt) for w, dt in out_w],
        scratch_shapes=[pltpu.VMEM((tm // 2, D), BF16)],
        compiler_params=_cparams(("arbitrary",)),
        name="mixer_proj",
    )(x2, x2, *[a for a, _ in consts], *tabs)


def _ret_kernel(q_ref, k_ref, v_ref, g_ref, intra_ref, qd_ref, kd_ref, cd_ref, o_ref, st_ref):
    @pl.when(pl.program_id(1) == 0)
    def _():
        st_ref[...] = jnp.zeros_like(st_ref)

    C = RET_CHUNK
    nt = (((1,), (1,)), ((), ()))
    tn = (((0,), (0,)), ((), ()))
    states = [st_ref[hd] for hd in range(RET_HEADS)]
    for c in range(RET_ROWS // C):
        rows = slice(c * C, (c + 1) * C)
        for hd in range(RET_HEADS):
            cols = slice(hd * RET_DK, (hd + 1) * RET_DK)
            vcols = slice(hd * RET_DV, (hd + 1) * RET_DV)
            q = q_ref[0, rows, cols]
            k = k_ref[0, rows, cols]
            v = v_ref[0, rows, vcols]
            scores = lax.dot_general(q, k, nt, preferred_element_type=F32) * intra_ref[hd]
            inner = _dot(scores.astype(BF16), v)
            cross = _dot((q.astype(F32) * qd_ref[hd]).astype(BF16), states[hd].astype(BF16))
            kv = lax.dot_general((k.astype(F32) * kd_ref[hd]).astype(BF16), v, tn,
                                 preferred_element_type=F32)
            states[hd] = cd_ref[hd] * states[hd] + kv
            o = inner + cross
            mu = jnp.mean(o, axis=-1, keepdims=True)
            var = jnp.mean(jnp.square(o - mu), axis=-1, keepdims=True)
            on = (o - mu) * lax.rsqrt(var + GN_EPS)
            g = g_ref[0, rows, vcols].astype(F32)
            o_ref[0, rows, vcols] = (g * jax.nn.sigmoid(g) * on).astype(BF16)
    for hd in range(RET_HEADS):
        st_ref[hd] = states[hd]


def _ret_tables():
    C = RET_CHUNK
    log_gamma = jnp.log1p(-jnp.exp2(-5.0 - jnp.arange(RET_HEADS, dtype=F32)))
    pos = jnp.arange(C, dtype=F32)
    rel = pos[:, None] - pos[None, :]
    intra = jnp.where(rel[None] >= 0.0,
                      jnp.exp(jnp.maximum(rel, 0.0)[None] * log_gamma[:, None, None]), 0.0)
    k_decay = jnp.exp((C - 1.0 - pos)[:, None] * log_gamma[None, :])
    q_decay = jnp.exp((pos + 1.0)[:, None] * log_gamma[None, :])
    chunk_decay = jnp.exp(C * log_gamma)
    qd = jnp.broadcast_to(q_decay.T[:, :, None], (RET_HEADS, C, RET_DK))
    kd = jnp.broadcast_to(k_decay.T[:, :, None], (RET_HEADS, C, RET_DK))
    cd = jnp.broadcast_to(chunk_decay[:, None, None], (RET_HEADS, 1, RET_DV))
    return intra, qd, kd, cd


def _retention(rq, rk, rv, rg, tables, B, L):
    W = rq.shape[-1]
    shp = (B, L, W)
    blk = pl.BlockSpec((1, RET_ROWS, W), lambda b, i: (b, i, 0))
    seq = [a.reshape(shp) for a in (rq, rk, rv, rg)]
    return pl.pallas_call(
        _ret_kernel,
        grid=(B, L // RET_ROWS),
        in_specs=[blk] * len(seq) + [_const_spec(t.shape) for t in tables],
        out_specs=blk,
        out_shape=jax.ShapeDtypeStruct(shp, BF16),
        scratch_shapes=[pltpu.VMEM((RET_HEADS, RET_DK, RET_DV), F32)],
        compiler_params=_cparams(("parallel", "arbitrary")),
        name="retention",
    )(*seq, *tables)


def _s5_zoh_kernel(are_ref, aim_ref, ldt_ref, bre_ref, bim_ref,
                   abr_ref, abi_ref, bbr_ref, bbi_ref):
    a_re = are_ref[...]
    a_im = aim_ref[...]
    dt = jnp.exp(ldt_ref[...])
    mag = jnp.exp(a_re * dt)
    abar_re = mag * jnp.cos(a_im * dt)
    abar_im = mag * jnp.sin(a_im * dt)
    den = a_re * a_re + a_im * a_im
    nr = abar_re - 1.0
    f_re = (nr * a_re + abar_im * a_im) / den
    f_im = (abar_im * a_re - nr * a_im) / den
    b_re = bre_ref[...]
    b_im = bim_ref[...]
    abr_ref[...] = abar_re
    abi_ref[...] = abar_im
    bbr_ref[...] = f_re * b_re - f_im * b_im
    bbi_ref[...] = f_re * b_im + f_im * b_re


def _s5_params(a_re, a_im, log_dt, b_re, b_im, c_re, c_im, d):
    Dp, G, P = a_re.shape
    H = b_re.shape[-1]
    R = Dp * G * H
    rep = lambda a: jnp.broadcast_to(a[:, :, None, :], (Dp, G, H, a.shape[-1])).reshape(R, -1)
    bt = lambda b: jnp.swapaxes(b, 2, 3).reshape(R, P)
    args = [rep(a_re), rep(a_im), rep(log_dt[:, :, None]), bt(b_re), bt(b_im)]
    out = jax.ShapeDtypeStruct((R, P), F32)
    abr, abi, bbr, bbi = pl.pallas_call(
        _s5_zoh_kernel, out_shape=[out] * 4, name="s5_discretise",
        compiler_params=pltpu.CompilerParams(vmem_limit_bytes=VMEM_LIMIT),
    )(*args)
    abar_re = abr.reshape(Dp, G, H, P)[:, :, 0, :].reshape(Dp, 1, G * P)
    abar_im = abi.reshape(Dp, G, H, P)[:, :, 0, :].reshape(Dp, 1, G * P)
    gpb = LANES // H
    nblk = G // gpb
    eye = jnp.eye(gpb, dtype=F32)

    def pack_in(bb):
        bb = bb.reshape(Dp, nblk, gpb, H, P)
        m = jnp.einsum("dkghp,gf->dkghfp", bb, eye)
        return m.reshape(Dp, nblk, gpb * H, gpb * P).astype(BF16)

    def pack_out(cc):
        cc = cc.reshape(Dp, nblk, gpb, H, P)
        m = jnp.einsum("dkghp,gf->dkgpfh", cc, eye)
        return m.reshape(Dp, nblk, gpb * P, gpb * H).astype(BF16)

    return {"s5_are": abar_re, "s5_aim": abar_im,
            "s5_wbr": pack_in(bbr), "s5_wbi": pack_in(bbi),
            "s5_wcr": pack_out(c_re), "s5_wci": pack_out(c_im),
            "s5_d": d.reshape(Dp, 1, G * H)}


def _s5_kernel(u_ref, are_ref, aim_ref, wbr_ref, wbi_ref, wcr_ref, wci_ref, d_ref, y_ref,
               sre_ref, sim_ref, u2_ref, ut_ref, bre_ref, bim_ref, yt_ref):
    B, tl, W = u_ref.shape
    nblk = wbr_ref.shape[0]
    in_w = wbr_ref.shape[1]
    st_w = wbr_ref.shape[2]

    @pl.when(pl.program_id(0) == 0)
    def _():
        sre_ref[...] = jnp.zeros_like(sre_ref)
        sim_ref[...] = jnp.zeros_like(sim_ref)

    assert in_w == LANES and nblk * LANES == W

    def gather(kb):
        ch = slice(kb * LANES, (kb + 1) * LANES)
        for b in range(B):
            u2_ref[kb, b * tl:(b + 1) * tl, :] = u_ref[b, :, ch]
        for t in range(tl):
            ut_ref[t * B:(t + 1) * B, ch] = u2_ref[kb, pl.ds(t, B, stride=tl), :]

    def emit(kb):
        for b in range(B):
            y_ref[b, :, kb * LANES:(kb + 1) * LANES] = yt_ref[kb, pl.ds(b, tl, stride=B), :]

    def drive(kb):
        sl = slice(kb * st_w, (kb + 1) * st_w)
        lhs = ut_ref[:, kb * in_w:(kb + 1) * in_w].astype(BF16)
        bre_ref[:, sl] = _dot(lhs, wbr_ref[kb])
        bim_ref[:, sl] = _dot(lhs, wbi_ref[kb])

    def scan(kb):
        sl = slice(kb * st_w, (kb + 1) * st_w)
        ar = jnp.broadcast_to(are_ref[:, sl], (B, st_w))
        ai = jnp.broadcast_to(aim_ref[:, sl], (B, st_w))
        sr = sre_ref[:, sl]
        si = sim_ref[:, sl]
        for t in range(tl):
            rows = slice(t * B, (t + 1) * B)
            sr, si = (ar * sr - ai * si + bre_ref[rows, sl],
                      ar * si + ai * sr + bim_ref[rows, sl])
            bre_ref[rows, sl] = sr
            bim_ref[rows, sl] = si
        sre_ref[:, sl] = sr
        sim_ref[:, sl] = si

    def project(kb):
        sl = slice(kb * st_w, (kb + 1) * st_w)
        ch = slice(kb * in_w, (kb + 1) * in_w)
        y = (_dot(bre_ref[:, sl].astype(BF16), wcr_ref[kb])
             - _dot(bim_ref[:, sl].astype(BF16), wci_ref[kb]))
        yt_ref[kb] = jax.nn.gelu(y + d_ref[:, ch] * ut_ref[:, ch])

    gather(0)
    drive(0)
    for kb in range(nblk):
        if kb + 1 < nblk:
            gather(kb + 1)
            drive(kb + 1)
        scan(kb)
        if kb >= 1:
            project(kb - 1)
            emit(kb - 1)
    project(nblk - 1)
    emit(nblk - 1)


def _s5(u, p, l, B, L):
    W = u.shape[-1]
    tl = S5_TL
    n_state = p["s5_are"].shape[-1]
    blk = pl.BlockSpec((B, tl, W), lambda i: (0, i, 0))
    names = ["s5_are", "s5_aim", "s5_wbr", "s5_wbi", "s5_wcr", "s5_wci", "s5_d"]
    return pl.pallas_call(
        _s5_kernel,
        grid=(L // tl,),
        in_specs=[blk] + [_slab_spec(p[n], (l,)) for n in names],
        out_specs=blk,
        out_shape=jax.ShapeDtypeStruct((B, L, W), F32),
        scratch_shapes=[pltpu.VMEM((B, n_state), F32), pltpu.VMEM((B, n_state), F32),
                        pltpu.VMEM((W // LANES, B * tl, LANES), F32), pltpu.VMEM((B * tl, W), F32),
                        pltpu.VMEM((B * tl, n_state), F32), pltpu.VMEM((B * tl, n_state), F32),
                        pltpu.VMEM((W // LANES, B * tl, LANES), F32)],
        compiler_params=_cparams(("arbitrary",)),
        name="s5_scan",
    )(u.reshape(B, L, W), *[p[n] for n in names])


def _attn_kernel(q_ref, k_ref, v_ref, o_ref, s_ref, m_ref, mprev_ref, l_ref, acc_ref, vt_ref,
                 qt_ref):
    tq = q_ref.shape[1]
    nh = m_ref.shape[0]
    qi = pl.program_id(2)
    m_ref[...] = jnp.full(m_ref.shape, -1e30, F32)
    l_ref[...] = jnp.zeros_like(l_ref)
    acc_ref[...] = jnp.zeros_like(acc_ref)
    for h in range(nh):
        qt_ref[h] = q_ref[0, :, h * MLA_HEAD_PAD:(h + 1) * MLA_HEAD_PAD].T

    def scores(j, h, masked):
        r0 = pl.multiple_of(j * tq, tq)
        qk = slice(h * MLA_HEAD_PAD, (h + 1) * MLA_HEAD_PAD)
        st = _dot(k_ref[0, pl.ds(r0, tq), qk], qt_ref[h])
        if masked:
            kidx = lax.broadcasted_iota(jnp.int32, st.shape, 0)
            qidx = lax.broadcasted_iota(jnp.int32, st.shape, 1)
            st = jnp.where(kidx <= qidx, st, -jnp.inf)
        s_ref[h] = st
        m_old = m_ref[h]
        mprev_ref[h] = m_old
        m_ref[h] = jnp.maximum(m_old, jnp.max(st, axis=0, keepdims=True))

    @pl.when(qi == 0)
    def _():
        for h in range(nh):
            for blk in range(v_ref.shape[1] // tq):
                rows = slice(blk * tq, (blk + 1) * tq)
                vt_ref[h, :, rows] = v_ref[0, rows, h * MLA_DV:(h + 1) * MLA_DV].T

    def accumulate(j, h):
        r0 = pl.multiple_of(j * tq, tq)
        m_new = m_ref[h]
        alpha = jnp.exp2(mprev_ref[h] - m_new)
        pt = jnp.exp2(s_ref[h] - m_new)
        l_ref[h] = alpha * l_ref[h] + jnp.sum(pt, axis=0, keepdims=True)
        acc_ref[h] = alpha * acc_ref[h] + _dot(vt_ref[h, :, pl.ds(r0, tq)],
                                               pt.astype(BF16))

    la = ATTN_LOOKAHEAD
    assert 0 < la < nh
    for h in range(nh):
        scores(qi, h, True)
        if h >= la:
            accumulate(qi, h - la)

    def body(j, carry):
        prev = jnp.where(j == 0, qi, j - 1)
        for h in range(nh):
            scores(j, h, False)
            if h >= la:
                accumulate(j, h - la)
            else:
                accumulate(prev, h - la + nh)
        return carry
    lax.fori_loop(0, qi, body, 0)
    last = jnp.maximum(qi - 1, 0)
    for h in range(nh - la, nh):
        accumulate(last, h)
    for h in range(nh):
        o_ref[0, :, h * MLA_DV:(h + 1) * MLA_DV] = (acc_ref[h] / l_ref[h]).T.astype(BF16)


def _attention(mq, mk, mv, B, L):
    tq = ATTN_TQ
    nh = ATTN_HEADS_PER_STEP
    qw = nh * MLA_HEAD_PAD
    vw = nh * MLA_DV
    return pl.pallas_call(
        _attn_kernel,
        grid=(B, MLA_HEADS // nh, L // tq),
        in_specs=[pl.BlockSpec((1, tq, qw), lambda b, h, i: (b, i, h)),
                  pl.BlockSpec((1, L, qw), lambda b, h, i: (b, 0, h)),
                  pl.BlockSpec((1, L, vw), lambda b, h, i: (b, 0, h))],
        out_specs=pl.BlockSpec((1, tq, vw), lambda b, h, i: (b, i, h)),
        out_shape=jax.ShapeDtypeStruct((B, L, MLA_HEADS * MLA_DV), BF16),
        scratch_shapes=[pltpu.VMEM((nh, tq, tq), F32), pltpu.VMEM((nh, 1, tq), F32),
                        pltpu.VMEM((nh, 1, tq), F32), pltpu.VMEM((nh, 1, tq), F32),
                        pltpu.VMEM((nh, MLA_DV, tq), F32), pltpu.VMEM((nh, MLA_DV, L), BF16),
                        pltpu.VMEM((nh, MLA_HEAD_PAD, tq), BF16)],
        compiler_params=_cparams(("parallel", "parallel", "arbitrary")),
        name="mla_attention",
    )(mq.reshape(B, L, -1), mk.reshape(B, L, -1), mv.reshape(B, L, -1))


def _merge_kernel(x_ref, xn_ref, gin_ref, gout_ref, a_ref, y_ref, o_ref_in, wg_ref, wro_ref,
                  wga_ref, wgb_ref, wmo_ref, wout_ref, out_ref, ha_ref):
    half = x_ref.shape[0] // 2
    D = x_ref.shape[1]
    gin = gin_ref[...]
    gout = gout_ref[...]

    @pl.when(pl.program_id(0) == 0)
    def _():
        ha_ref[...] = _rms(x_ref[:half, :], gin).astype(BF16)

    def branches(h, rows):
        ys = y_ref[rows, :].astype(BF16)
        return (_dot(h, wg_ref[:, 0:D]), _dot(a_ref[rows, :], wro_ref[...]),
                _dot(h, wg_ref[:, D:2 * D]), _dot(ys, wga_ref[...]), _dot(ys, wgb_ref[...]),
                _dot(h, wg_ref[:, 2 * D:3 * D]), _dot(o_ref_in[rows, :], wmo_ref[...]))

    def combine(g0, y_ret, g1, ya, yb, g2, y_mla):
        merged = jax.nn.sigmoid(g0) * y_ret
        merged = merged + jax.nn.sigmoid(g1) * (ya * jax.nn.sigmoid(yb))
        merged = merged + jax.nn.sigmoid(g2) * y_mla
        return merged.astype(BF16)

    rows_a = slice(0, half)
    rows_b = slice(half, 2 * half)
    ba = branches(ha_ref[...], rows_a)
    bb = branches(_rms(x_ref[rows_b, :], gin).astype(BF16), rows_b)
    oa = _dot(combine(*ba), wout_ref[...])
    ob = _dot(combine(*bb), wout_ref[...])
    out_ref[rows_a, :] = x_ref[rows_a, :] + _rms(oa, gout)
    ha_ref[...] = _rms(xn_ref[...], gin).astype(BF16)
    out_ref[rows_b, :] = x_ref[rows_b, :] + _rms(ob, gout)


def _merge(x2, a_ret, y_s5, o_mla, p, l):
    T, D = x2.shape
    tm = ROW_TILE
    n = T // tm
    row = lambda w: pl.BlockSpec((tm, w), lambda i: (i, 0))
    nxt = pl.BlockSpec((tm // 2, D), lambda i: (jnp.minimum(2 * i + 2, 2 * n - 2), 0))
    n_sub = p["gains"].shape[0] // p["wgates"].shape[0]
    names = ["wgates", "ret_wo", "glu_a", "glu_b", "mla_wo", "w_out"]
    return pl.pallas_call(
        _merge_kernel,
        grid=(n,),
        in_specs=[row(D), nxt, _slab_spec(p["gains"], (l * n_sub + 2,)),
                  _slab_spec(p["gains"], (l * n_sub + 3,)),
                  row(a_ret.shape[1]), row(y_s5.shape[1]), row(o_mla.shape[1])]
                 + [_slab_spec(p[n], (l,)) for n in names],
        out_specs=row(D),
        out_shape=jax.ShapeDtypeStruct((T, D), F32),
        scratch_shapes=[pltpu.VMEM((tm // 2, D), BF16)],
        compiler_params=_cparams(("arbitrary",)),
        name="gated_merge",
    )(x2, x2, p["gains"], p["gains"], a_ret, y_s5, o_mla, *[p[n] for n in names])


def _prepare(norm_gains, ffn_w_gate, ffn_w_up, ffn_w_down, w_in, ret_w_o,
             s5_a_re, s5_a_im, s5_log_dt, s5_b_re, s5_b_im, s5_c_re, s5_c_im, s5_d,
             s5_glu_a, s5_glu_b, mla_q_norm, mla_kv_norm, mla_w_uq, mla_w_ukv, mla_w_o, w_out):
    depth, n_sub, D = norm_gains.shape
    q_rank = mla_q_norm.shape[-1]
    kv_rank = mla_kv_norm.shape[-1]
    s5_w = s5_d.shape[-1]
    ret_w = 2 * RET_HEADS * RET_DK + 2 * RET_HEADS * RET_DV
    m_lo = ret_w + s5_w
    m_w = q_rank + kv_rank + MLA_ROPE
    assert w_in.shape[-1] == m_lo + m_w + 3 * D
    hd_w = MLA_NOPE + MLA_ROPE
    wuq = jnp.pad(mla_w_uq.reshape(depth, q_rank, MLA_HEADS, hd_w),
                  ((0, 0), (0, 0), (0, 0), (0, MLA_HEAD_PAD - hd_w)))
    wukv = mla_w_ukv.reshape(depth, kv_rank, MLA_HEADS, MLA_NOPE + MLA_DV)
    p = {
        "gains": norm_gains.reshape(depth * n_sub, 1, D),
        "ffn_wg": ffn_w_gate.astype(BF16), "ffn_wu": ffn_w_up.astype(BF16),
        "ffn_wd": ffn_w_down.astype(BF16),
        "w_in": w_in.astype(BF16),
        "wm": jnp.pad(w_in[:, :, m_lo:m_lo + m_w],
                      ((0, 0), (0, 0), (0, LANES - MLA_ROPE))).astype(BF16),
        "wgates": w_in[:, :, m_lo + m_w:].astype(BF16),
        "qn": mla_q_norm.reshape(depth, 1, q_rank), "kvn": mla_kv_norm.reshape(depth, 1, kv_rank),
        "wuq": wuq.reshape(depth, q_rank, MLA_HEADS * MLA_HEAD_PAD).astype(BF16),
        "wuk": wukv[..., :MLA_NOPE].reshape(depth, kv_rank, MLA_HEADS * MLA_NOPE).astype(BF16),
        "wuv": wukv[..., MLA_NOPE:].reshape(depth, kv_rank, MLA_HEADS * MLA_DV).astype(BF16),
        "ret_wo": ret_w_o.astype(BF16), "glu_a": s5_glu_a.astype(BF16),
        "glu_b": s5_glu_b.astype(BF16), "mla_wo": mla_w_o.astype(BF16),
        "w_out": w_out.astype(BF16),
    }
    p.update(_s5_params(s5_a_re, s5_a_im, s5_log_dt, s5_b_re, s5_b_im, s5_c_re, s5_c_im, s5_d))
    return p


def kernel(x, positions, norm_gains, ffn_w_gate, ffn_w_up, ffn_w_down, w_in, ret_w_o,
           s5_a_re, s5_a_im, s5_log_dt, s5_b_re, s5_b_im, s5_c_re, s5_c_im, s5_d,
           s5_glu_a, s5_glu_b, mla_q_norm, mla_kv_norm, mla_w_uq, mla_w_ukv, mla_w_o, w_out):
    B, L, D = x.shape
    depth = norm_gains.shape[0]
    assert L % ROW_TILE == 0 and L % RET_ROWS == 0 and L % ATTN_TQ == 0 and L % S5_TL == 0
    assert B == SUBLANES, "the S5 scan keeps the batch on the sublane axis"

    p = _prepare(norm_gains, ffn_w_gate, ffn_w_up, ffn_w_down, w_in, ret_w_o,
                 s5_a_re, s5_a_im, s5_log_dt, s5_b_re, s5_b_im, s5_c_re, s5_c_im, s5_d,
                 s5_glu_a, s5_glu_b, mla_q_norm, mla_kv_norm, mla_w_uq, mla_w_ukv, mla_w_o, w_out)
    tabs = _rope_tables(positions)
    ret_tabs = _ret_tables()
    x2 = x.reshape(B * L, D)
    for l in range(depth):
        x2 = _ffn(x2, p, l, 0)
        rq, rk, rv, rg, u, mq, mk, mv = _proj(x2, p, l, tabs)
        a_ret = _retention(rq, rk, rv, rg, ret_tabs, B, L)
        y_s5 = _s5(u, p, l, B, L)
        o_mla = _attention(mq, mk, mv, B, L)
        x2 = _merge(x2, a_ret.reshape(B * L, -1), y_s5.reshape(B * L, -1),
                    o_mla.reshape(B * L, -1), p, l)
        x2 = _ffn(x2, p, l, 1)
    return x2.reshape(B, L, D)
```

```python
import math

import jax
import jax.numpy as jnp
from jax import lax
from jax.experimental import pallas as pl
from jax.experimental.pallas import tpu as pltpu

F32 = jnp.float32
BF16 = jnp.bfloat16

ROPE_BASE = 10000.0
NORM_EPS = 1e-6
GN_EPS = 1e-5
FFN_RES = 0.5
RET_HEADS = 4
RET_DK = 128
RET_DV = 128
RET_CHUNK = 128
MLA_HEADS = 4
MLA_NOPE = 128
MLA_ROPE = 64
MLA_DV = 128
MLA_HEAD_PAD = 256
MLA_LOG2_SCALE = (MLA_NOPE + MLA_ROPE) ** -0.5 * math.log2(math.e)

LANES = 128
SUBLANES = 8
VMEM_LIMIT = 56 * 1024 * 1024

ROW_TILE = 512
FFN_PIECE = 256
FFN_PIECES = 4
MERGE_PIECE = 256
MERGE_PIECES = 4
FFN_CHUNKS = ((0, 1024), (1024, 2048), (2048, 2816))
RET_ROWS = 512
S5_TL = 128
S5_HALF = 1024
ATTN_TQ = 512
ATTN_HEADS_PER_STEP = 4
ATTN_LOOKAHEAD = 2


def _cparams(sem):
    return pltpu.CompilerParams(dimension_semantics=sem, vmem_limit_bytes=VMEM_LIMIT)


def _const_spec(shape):
    nd = len(shape)
    return pl.BlockSpec(shape, lambda *_: (0,) * nd, pipeline_mode=pl.Buffered(1))


def _slab_spec(arr, lead, block=None, col=0):
    tail = tuple(arr.shape[len(lead):]) if block is None else tuple(block)
    idx = tuple(lead) + (0,) * (len(tail) - 1) + (col,)
    return pl.BlockSpec((None,) * len(lead) + tail, lambda *_: idx,
                        pipeline_mode=pl.Buffered(1))


def _rms(x, g):
    y = x * lax.rsqrt(jnp.mean(x * x, axis=-1, keepdims=True) + NORM_EPS)
    return y * g


def _dot(a, b):
    return jnp.dot(a, b, preferred_element_type=F32)


def _rope_kernel(pos_ref, f_ref, a_ref, na_ref, b_ref, p_ref, np_ref, q_ref,
                 cr_ref, sr_ref, cm_ref, s1_ref, s2_ref):
    ang = pos_ref[...] * f_ref[...]
    c = jnp.cos(ang)
    s = jnp.sin(ang)
    hr = RET_DK // 2
    c_hi = pltpu.roll(c, hr, 1)
    s_hi = pltpu.roll(s, hr, 1)
    c_m2 = pltpu.roll(c, hr + MLA_ROPE // 2, 1)
    s_m2 = pltpu.roll(s, hr + MLA_ROPE // 2, 1)
    cr_ref[...] = c * a_ref[...] + c_hi * b_ref[...]
    sr_ref[...] = s * na_ref[...] + s_hi * b_ref[...]
    cm_ref[...] = c_hi * p_ref[...] + c_m2 * q_ref[...]
    s1_ref[...] = s_hi * np_ref[...]
    s2_ref[...] = s_m2 * q_ref[...]


def _rope_tables(positions):
    B, L = positions.shape
    T = B * L
    pos = positions.astype(F32).reshape(T, 1)
    inv_r = ROPE_BASE ** (-jnp.arange(0, RET_DK, 2, dtype=F32) / RET_DK)
    inv_m = ROPE_BASE ** (-jnp.arange(0, MLA_ROPE, 2, dtype=F32) / MLA_ROPE)
    hr = RET_DK // 2
    hm = MLA_ROPE // 2
    lane = jnp.arange(LANES)
    freq = jnp.concatenate([inv_r, inv_m, jnp.zeros((LANES - hr - hm,), F32)])
    a = (lane < hr).astype(F32)
    b = (lane >= hr).astype(F32)
    p = (lane < hm).astype(F32)
    q = ((lane >= hm) & (lane < 2 * hm)).astype(F32)
    rows = [r.reshape(1, LANES) for r in (freq, a, -a, b, p, -p, q)]
    tm = 2048
    row_spec = pl.BlockSpec((1, LANES), lambda i: (0, 0))
    tab_spec = pl.BlockSpec((tm, LANES), lambda i: (i, 0))
    tab_shape = jax.ShapeDtypeStruct((T, LANES), F32)
    return pl.pallas_call(
        _rope_kernel,
        grid=(T // tm,),
        in_specs=[pl.BlockSpec((tm, 1), lambda i: (i, 0))] + [row_spec] * len(rows),
        out_specs=[tab_spec] * 5,
        out_shape=[tab_shape] * 5,
        compiler_params=_cparams(("parallel",)),
        name="rope_tables",
    )(pos, *rows)


def _ffn_kernel(x_ref, xn_ref, gpre_ref, gpost_ref, wg_ref, wu_ref, wd_ref, o_ref, ha_ref):
    pr = ha_ref.shape[0]
    n_pieces = x_ref.shape[0] // pr
    gpre = gpre_ref[...]
    gpost = gpost_ref[...]

    @pl.when(pl.program_id(0) == 0)
    def _():
        ha_ref[...] = _rms(x_ref[:pr, :], gpre).astype(BF16)

    def swiglu(h):
        y = None
        pending = None
        for c0, c1 in FFN_CHUNKS + ((None, None),):
            if c0 is not None:
                g = _dot(h, wg_ref[:, c0:c1])
                u = _dot(h, wu_ref[:, c0:c1])
            if pending is not None:
                a, p0, p1 = pending
                part = _dot(a, wd_ref[p0:p1, :])
                y = part if y is None else y + part
            if c0 is not None:
                pending = ((g * jax.nn.sigmoid(g) * u).astype(BF16), c0, c1)
        return y

    def finish(piece, y):
        rows = slice(piece * pr, (piece + 1) * pr)
        o_ref[rows, :] = x_ref[rows, :] + FFN_RES * _rms(y, gpost)

    y_prev = swiglu(ha_ref[...])
    for piece in range(1, n_pieces):
        y = swiglu(_rms(x_ref[piece * pr:(piece + 1) * pr, :], gpre).astype(BF16))
        finish(piece - 1, y_prev)
        y_prev = y
    ha_ref[...] = _rms(xn_ref[...], gpre).astype(BF16)
    finish(n_pieces - 1, y_prev)


def _ffn(x2, p, l, k):
    T, D = x2.shape
    half = FFN_PIECE
    tile = FFN_PIECE * FFN_PIECES
    n = T // tile
    row = pl.BlockSpec((tile, D), lambda i: (i, 0))
    nxt = pl.BlockSpec((half, D), lambda i: (jnp.minimum(i + 1, n - 1) * FFN_PIECES, 0))
    n_sub = p["gains"].shape[0] // p["ffn_wg"].shape[0]
    gi = l * n_sub + 4 * k
    return pl.pallas_call(
        _ffn_kernel,
        grid=(n,),
        in_specs=[row, nxt, _slab_spec(p["gains"], (gi,)), _slab_spec(p["gains"], (gi + 1,)),
                  _slab_spec(p["ffn_wg"], (l, k)), _slab_spec(p["ffn_wu"], (l, k)),
                  _slab_spec(p["ffn_wd"], (l, k))],
        out_specs=row,
        out_shape=jax.ShapeDtypeStruct((T, D), F32),
        scratch_shapes=[pltpu.VMEM((half, D), BF16)],
        compiler_params=_cparams(("arbitrary",)),
        name="ffn",
    )(x2, x2, p["gains"], p["gains"], p["ffn_wg"], p["ffn_wu"], p["ffn_wd"])


def _rope_ret(v, cos, sin):
    return v * cos + pltpu.roll(v, RET_DK // 2, 1) * sin


def _rope_mla(v, cos, sin1, sin2):
    hm = MLA_ROPE // 2
    return v * cos + pltpu.roll(v, LANES - hm, 1) * sin1 + pltpu.roll(v, hm, 1) * sin2


def _proj_kernel(x_ref, xn_ref, g_ref, wr_ref, wu_ref, wm_ref, qn_ref, kvn_ref,
                 wuq_ref, wuk_ref, wuv_ref, cr_ref, sr_ref, cm_ref, s1_ref, s2_ref,
                 rq_ref, rk_ref, rv_ref, rg_ref, u_ref, mq_ref, mk_ref, mv_ref, ha_ref):
    half = x_ref.shape[0] // 2
    gain = g_ref[...]
    qk_w = RET_HEADS * RET_DK
    v_w = RET_HEADS * RET_DV
    q_rank = qn_ref.shape[1]
    kv_rank = kvn_ref.shape[1]

    @pl.when(pl.program_id(0) == 0)
    def _():
        ha_ref[...] = _rms(x_ref[:half, :], gain).astype(BF16)

    def project(h):
        m = _dot(h, wm_ref[...])
        r = _dot(h, wr_ref[...])
        u = _dot(h, wu_ref[...])
        return m, r, u

    def mla_up(m):
        cq = _rms(m[:, :q_rank], qn_ref[...]).astype(BF16)
        ckv = _rms(m[:, q_rank:q_rank + kv_rank], kvn_ref[...]).astype(BF16)
        q = _dot(cq, wuq_ref[...])
        kn = _dot(ckv, wuk_ref[...])
        v = _dot(ckv, wuv_ref[...])
        return q, kn, v

    def store(rows, m, r, u, q, kn, v):
        cr = cr_ref[rows, :]
        sr = sr_ref[rows, :]
        for hd in range(RET_HEADS):
            lo = hd * RET_DK
            hcols = slice(lo, lo + RET_DK)
            qr = _rope_ret(r[:, lo:lo + RET_DK], cr, sr)
            kr = _rope_ret(r[:, qk_w + lo:qk_w + lo + RET_DK], cr, sr) * (RET_DK ** -0.5)
            rq_ref[rows, hcols] = qr.astype(BF16)
            rk_ref[rows, hcols] = kr.astype(BF16)
        rv_ref[rows, :] = r[:, 2 * qk_w:2 * qk_w + v_w].astype(BF16)
        rg_ref[rows, :] = r[:, 2 * qk_w + v_w:].astype(BF16)
        u_ref[rows, :] = u
        cm = cm_ref[rows, :]
        s1 = s1_ref[rows, :]
        s2 = s2_ref[rows, :]
        kpe = _rope_mla(m[:, q_rank + kv_rank:], cm, s1, s2).astype(BF16)
        mv_ref[rows, :] = v.astype(BF16)
        for hd in range(MLA_HEADS):
            lo = hd * MLA_HEAD_PAD
            mq_ref[rows, lo:lo + MLA_NOPE] = (q[:, lo:lo + MLA_NOPE] * MLA_LOG2_SCALE).astype(BF16)
            mq_ref[rows, lo + MLA_NOPE:lo + MLA_HEAD_PAD] = (_rope_mla(
                q[:, lo + MLA_NOPE:lo + MLA_HEAD_PAD], cm, s1, s2) * MLA_LOG2_SCALE).astype(BF16)
            mk_ref[rows, lo:lo + MLA_NOPE] = kn[:, hd * MLA_NOPE:(hd + 1) * MLA_NOPE].astype(BF16)
            mk_ref[rows, lo + MLA_NOPE:lo + MLA_HEAD_PAD] = kpe

    pa = project(ha_ref[...])
    pb = project(_rms(x_ref[half:, :], gain).astype(BF16))
    ua = mla_up(pa[0])
    ub = mla_up(pb[0])
    store(slice(0, half), *pa, *ua)
    ha_ref[...] = _rms(xn_ref[...], gain).astype(BF16)
    store(slice(half, 2 * half), *pb, *ub)


def _proj(x2, p, l, tabs):
    T, D = x2.shape
    tm = ROW_TILE
    n = T // tm
    row = lambda n: pl.BlockSpec((tm, n), lambda i: (i, 0))
    nxt = pl.BlockSpec((tm // 2, D), lambda i: (jnp.minimum(2 * i + 2, 2 * n - 2), 0))
    n_sub = p["gains"].shape[0] // p["w_in"].shape[0]
    ret_w = 2 * RET_HEADS * RET_DK + 2 * RET_HEADS * RET_DV
    s5_w = p["s5_d"].shape[-1]
    assert ret_w % s5_w == 0
    consts = [(p["gains"], _slab_spec(p["gains"], (l * n_sub + 2,))),
              (p["w_in"], _slab_spec(p["w_in"], (l,), block=(D, ret_w), col=0)),
              (p["w_in"], _slab_spec(p["w_in"], (l,), block=(D, s5_w), col=ret_w // s5_w)),
              (p["wm"], _slab_spec(p["wm"], (l,))),
              (p["qn"], _slab_spec(p["qn"], (l,))), (p["kvn"], _slab_spec(p["kvn"], (l,))),
              (p["wuq"], _slab_spec(p["wuq"], (l,))), (p["wuk"], _slab_spec(p["wuk"], (l,))),
              (p["wuv"], _slab_spec(p["wuv"], (l,)))]
    out_w = [(RET_HEADS * RET_DK, BF16)] * 2 + [
             (RET_HEADS * RET_DV, BF16), (RET_HEADS * RET_DV, BF16), (s5_w, F32),
             (MLA_HEADS * MLA_HEAD_PAD, BF16), (MLA_HEADS * MLA_HEAD_PAD, BF16),
             (MLA_HEADS * MLA_DV, BF16)]
    return pl.pallas_call(
        _proj_kernel,
        grid=(n,),
        in_specs=[row(D), nxt] + [s for _, s in consts] + [row(LANES)] * 5,
        out_specs=[row(w) for w, _ in out_w],
        out_shape=[jax.ShapeDtypeStruct((T, w), dt) for w, dt in out_w],
        scratch_shapes=[pltpu.VMEM((tm // 2, D), BF16)],
        compiler_params=_cparams(("arbitrary",)),
        name="mixer_proj",
    )(x2, x2, *[a for a, _ in consts], *tabs)


def _ret_kernel(q_ref, k_ref, v_ref, g_ref, intra_ref, qd_ref, kd_ref, cd_ref, o_ref, st_ref):
    @pl.when(pl.program_id(1) == 0)
    def _():
        st_ref[...] = jnp.zeros_like(st_ref)

    C = RET_CHUNK
    nt = (((1,), (1,)), ((), ()))
    tn = (((0,), (0,)), ((), ()))
    states = [st_ref[hd] for hd in range(RET_HEADS)]
    for c in range(RET_ROWS // C):
        rows = slice(c * C, (c + 1) * C)
        for hd in range(RET_HEADS):
            cols = slice(hd * RET_DK, (hd + 1) * RET_DK)
            vcols = slice(hd * RET_DV, (hd + 1) * RET_DV)
            q = q_ref[0, rows, cols]
            k = k_ref[0, rows, cols]
            v = v_ref[0, rows, vcols]
            scores = lax.dot_general(q, k, nt, preferred_element_type=F32) * intra_ref[hd]
            inner = _dot(scores.astype(BF16), v)
            cross = _dot((q.astype(F32) * qd_ref[hd]).astype(BF16), states[hd].astype(BF16))
            kv = lax.dot_general((k.astype(F32) * kd_ref[hd]).astype(BF16), v, tn,
                                 preferred_element_type=F32)
            states[hd] = cd_ref[hd] * states[hd] + kv
            o = inner + cross
            mu = jnp.mean(o, axis=-1, keepdims=True)
            var = jnp.mean(jnp.square(o - mu), axis=-1, keepdims=True)
            on = (o - mu) * lax.rsqrt(var + GN_EPS)
            g = g_ref[0, rows, vcols].astype(F32)
            o_ref[0, rows, vcols] = (g * jax.nn.sigmoid(g) * on).astype(BF16)
    for hd in range(RET_HEADS):
        st_ref[hd] = states[hd]


def _ret_tables():
    C = RET_CHUNK
    log_gamma = jnp.log1p(-jnp.exp2(-5.0 - jnp.arange(RET_HEADS, dtype=F32)))
    pos = jnp.arange(C, dtype=F32)
    rel = pos[:, None] - pos[None, :]
    intra = jnp.where(rel[None] >= 0.0,
                      jnp.exp(jnp.maximum(rel, 0.0)[None] * log_gamma[:, None, None]), 0.0)
    k_decay = jnp.exp((C - 1.0 - pos)[:, None] * log_gamma[None, :])
    q_decay = jnp.exp((pos + 1.0)[:, None] * log_gamma[None, :])
    chunk_decay = jnp.exp(C * log_gamma)
    qd = jnp.broadcast_to(q_decay.T[:, :, None], (RET_HEADS, C, RET_DK))
    kd = jnp.broadcast_to(k_decay.T[:, :, None], (RET_HEADS, C, RET_DK))
    cd = jnp.broadcast_to(chunk_decay[:, None, None], (RET_HEADS, 1, RET_DV))
    return intra, qd, kd, cd


def _retention(rq, rk, rv, rg, tables, B, L):
    W = rq.shape[-1]
    shp = (B, L, W)
    blk = pl.BlockSpec((1, RET_ROWS, W), lambda b, i: (b, i, 0))
    seq = [a.reshape(shp) for a in (rq, rk, rv, rg)]
    return pl.pallas_call(
        _ret_kernel,
        grid=(B, L // RET_ROWS),
        in_specs=[blk] * len(seq) + [_const_spec(t.shape) for t in tables],
        out_specs=blk,
        out_shape=jax.ShapeDtypeStruct(shp, BF16),
        scratch_shapes=[pltpu.VMEM((RET_HEADS, RET_DK, RET_DV), F32)],
        compiler_params=_cparams(("parallel", "arbitrary")),
        name="retention",
    )(*seq, *tables)


def _s5_zoh_kernel(are_ref, aim_ref, ldt_ref, bre_ref, bim_ref,
                   abr_ref, abi_ref, bbr_ref, bbi_ref):
    a_re = are_ref[...]
    a_im = aim_ref[...]
    dt = jnp.exp(ldt_ref[...])
    mag = jnp.exp(a_re * dt)
    abar_re = mag * jnp.cos(a_im * dt)
    abar_im = mag * jnp.sin(a_im * dt)
    den = a_re * a_re + a_im * a_im
    nr = abar_re - 1.0
    f_re = (nr * a_re + abar_im * a_im) / den
    f_im = (abar_im * a_re - nr * a_im) / den
    b_re = bre_ref[...]
    b_im = bim_ref[...]
    abr_ref[...] = abar_re
    abi_ref[...] = abar_im
    bbr_ref[...] = f_re * b_re - f_im * b_im
    bbi_ref[...] = f_re * b_im + f_im * b_re


def _s5_params(a_re, a_im, log_dt, b_re, b_im, c_re, c_im, d):
    Dp, G, P = a_re.shape
    H = b_re.shape[-1]
    R = Dp * G * H
    rep = lambda a: jnp.broadcast_to(a[:, :, None, :], (Dp, G, H, a.shape[-1])).reshape(R, -1)
    bt = lambda b: jnp.swapaxes(b, 2, 3).reshape(R, P)
    args = [rep(a_re), rep(a_im), rep(log_dt[:, :, None]), bt(b_re), bt(b_im)]
    out = jax.ShapeDtypeStruct((R, P), F32)
    abr, abi, bbr, bbi = pl.pallas_call(
        _s5_zoh_kernel, out_shape=[out] * 4, name="s5_discretise",
        compiler_params=pltpu.CompilerParams(vmem_limit_bytes=VMEM_LIMIT),
    )(*args)
    abar_re = abr.reshape(Dp, G, H, P)[:, :, 0, :].reshape(Dp, 1, G * P)
    abar_im = abi.reshape(Dp, G, H, P)[:, :, 0, :].reshape(Dp, 1, G * P)
    gpb = LANES // H
    nblk = G // gpb
    eye = jnp.eye(gpb, dtype=F32)

    def pack_in(bb):
        bb = bb.reshape(Dp, nblk, gpb, H, P)
        m = jnp.einsum("dkghp,gf->dkghfp", bb, eye)
        return m.reshape(Dp, nblk, gpb * H, gpb * P).astype(BF16)

    def pack_out(cc):
        cc = cc.reshape(Dp, nblk, gpb, H, P)
        m = jnp.einsum("dkghp,gf->dkgpfh", cc, eye)
        return m.reshape(Dp, nblk, gpb * P, gpb * H).astype(BF16)

    return {"s5_are": abar_re, "s5_aim": abar_im,
            "s5_wbr": pack_in(bbr), "s5_wbi": pack_in(bbi),
            "s5_wcr": pack_out(c_re), "s5_wci": pack_out(c_im),
            "s5_d": d.reshape(Dp, 1, G * H)}


def _s5_kernel(u_ref, are_ref, aim_ref, wbr_ref, wbi_ref, wcr_ref, wci_ref, d_ref, y_ref,
               sre_ref, sim_ref, u2_ref, ut_ref, bre_ref, bim_ref, yt_ref):
    B, tl, W = u_ref.shape
    nblk = wbr_ref.shape[0]
    in_w = wbr_ref.shape[1]
    st_w = wbr_ref.shape[2]

    @pl.when(pl.program_id(0) == 0)
    def _():
        sre_ref[...] = jnp.zeros_like(sre_ref)
        sim_ref[...] = jnp.zeros_like(sim_ref)

    assert in_w == LANES and nblk * LANES == W

    def gather(kb):
        ch = slice(kb * LANES, (kb + 1) * LANES)
        for b in range(B):
            u2_ref[kb, b * tl:(b + 1) * tl, :] = u_ref[b, :, ch]
        for t in range(tl):
            ut_ref[t * B:(t + 1) * B, ch] = u2_ref[kb, pl.ds(t, B, stride=tl), :]

    def emit(kb):
        for b in range(B):
            y_ref[b, :, kb * LANES:(kb + 1) * LANES] = yt_ref[kb, pl.ds(b, tl, stride=B), :]

    def drive(kb):
        sl = slice(kb * st_w, (kb + 1) * st_w)
        lhs = ut_ref[:, kb * in_w:(kb + 1) * in_w].astype(BF16)
        bre_ref[:, sl] = _dot(lhs, wbr_ref[kb])
        bim_ref[:, sl] = _dot(lhs, wbi_ref[kb])

    def scan(kb):
        sl = slice(kb * st_w, (kb + 1) * st_w)
        ar = jnp.broadcast_to(are_ref[:, sl], (B, st_w))
        ai = jnp.broadcast_to(aim_ref[:, sl], (B, st_w))
        sr = sre_ref[:, sl]
        si = sim_ref[:, sl]
        for t in range(tl):
            rows = slice(t * B, (t + 1) * B)
            sr, si = (ar * sr - ai * si + bre_ref[rows, sl],
                      ar * si + ai * sr + bim_ref[rows, sl])
            bre_ref[rows, sl] = sr
            bim_ref[rows, sl] = si
        sre_ref[:, sl] = sr
        sim_ref[:, sl] = si

    def project(kb):
        sl = slice(kb * st_w, (kb + 1) * st_w)
        ch = slice(kb * in_w, (kb + 1) * in_w)
        y = (_dot(bre_ref[:, sl].astype(BF16), wcr_ref[kb])
             - _dot(bim_ref[:, sl].astype(BF16), wci_ref[kb]))
        yt_ref[kb] = jax.nn.gelu(y + d_ref[:, ch] * ut_ref[:, ch])

    gather(0)
    drive(0)
    for kb in range(nblk):
        if kb + 1 < nblk:
            gather(kb + 1)
            drive(kb + 1)
        scan(kb)
        if kb >= 1:
            project(kb - 1)
            emit(kb - 1)
    project(nblk - 1)
    emit(nblk - 1)


def _s5(u, p, l, B, L):
    W = u.shape[-1]
    tl = S5_TL
    n_state = p["s5_are"].shape[-1]
    blk = pl.BlockSpec((B, tl, W), lambda i: (0, i, 0))
    names = ["s5_are", "s5_aim", "s5_wbr", "s5_wbi", "s5_wcr", "s5_wci", "s5_d"]
    return pl.pallas_call(
        _s5_kernel,
        grid=(L // tl,),
        in_specs=[blk] + [_slab_spec(p[n], (l,)) for n in names],
        out_specs=blk,
        out_shape=jax.ShapeDtypeStruct((B, L, W), F32),
        scratch_shapes=[pltpu.VMEM((B, n_state), F32), pltpu.VMEM((B, n_state), F32),
                        pltpu.VMEM((W // LANES, B * tl, LANES), F32), pltpu.VMEM((B * tl, W), F32),
                        pltpu.VMEM((B * tl, n_state), F32), pltpu.VMEM((B * tl, n_state), F32),
                        pltpu.VMEM((W // LANES, B * tl, LANES), F32)],
        compiler_params=_cparams(("arbitrary",)),
        name="s5_scan",
    )(u.reshape(B, L, W), *[p[n] for n in names])


def _attn_kernel(q_ref, k_ref, v_ref, o_ref, s_ref, m_ref, mprev_ref, l_ref, acc_ref, vt_ref,
                 qt_ref):
    tq = q_ref.shape[1]
    nh = m_ref.shape[0]
    qi = pl.program_id(2)
    m_ref[...] = jnp.full(m_ref.shape, -1e30, F32)
    l_ref[...] = jnp.zeros_like(l_ref)
    acc_ref[...] = jnp.zeros_like(acc_ref)
    for h in range(nh):
        qt_ref[h] = q_ref[0, :, h * MLA_HEAD_PAD:(h + 1) * MLA_HEAD_PAD].T

    def scores(j, h, masked):
        r0 = pl.multiple_of(j * tq, tq)
        qk = slice(h * MLA_HEAD_PAD, (h + 1) * MLA_HEAD_PAD)
        st = _dot(k_ref[0, pl.ds(r0, tq), qk], qt_ref[h])
        if masked:
            kidx = lax.broadcasted_iota(jnp.int32, st.shape, 0)
            qidx = lax.broadcasted_iota(jnp.int32, st.shape, 1)
            st = jnp.where(kidx <= qidx, st, -jnp.inf)
        s_ref[h] = st
        m_old = m_ref[h]
        mprev_ref[h] = m_old
        m_ref[h] = jnp.maximum(m_old, jnp.max(st, axis=0, keepdims=True))

    @pl.when(qi == 0)
    def _():
        for h in range(nh):
            for blk in range(v_ref.shape[1] // tq):
                rows = slice(blk * tq, (blk + 1) * tq)
                vt_ref[h, :, rows] = v_ref[0, rows, h * MLA_DV:(h + 1) * MLA_DV].T

    def accumulate(j, h):
        r0 = pl.multiple_of(j * tq, tq)
        m_new = m_ref[h]
        alpha = jnp.exp2(mprev_ref[h] - m_new)
        pt = jnp.exp2(s_ref[h] - m_new)
        l_ref[h] = alpha * l_ref[h] + jnp.sum(pt, axis=0, keepdims=True)
        acc_ref[h] = alpha * acc_ref[h] + _dot(vt_ref[h, :, pl.ds(r0, tq)],
                                               pt.astype(BF16))

    la = ATTN_LOOKAHEAD
    assert 0 < la < nh
    for h in range(nh):
        scores(qi, h, True)
        if h >= la:
            accumulate(qi, h - la)

    def body(j, carry):
        prev = jnp.where(j == 0, qi, j - 1)
        for h in range(nh):
            scores(j, h, False)
            if h >= la:
                accumulate(j, h - la)
            else:
                accumulate(prev, h - la + nh)
        return carry
    lax.fori_loop(0, qi, body, 0)
    last = jnp.maximum(qi - 1, 0)
    for h in range(nh - la, nh):
        accumulate(last, h)
    for h in range(nh):
        o_ref[0, :, h * MLA_DV:(h + 1) * MLA_DV] = (acc_ref[h] / l_ref[h]).T.astype(BF16)


def _attention(mq, mk, mv, B, L):
    tq = ATTN_TQ
    nh = ATTN_HEADS_PER_STEP
    qw = nh * MLA_HEAD_PAD
    vw = nh * MLA_DV
    return pl.pallas_call(
        _attn_kernel,
        grid=(B, MLA_HEADS // nh, L // tq),
        in_specs=[pl.BlockSpec((1, tq, qw), lambda b, h, i: (b, i, h)),
                  pl.BlockSpec((1, L, qw), lambda b, h, i: (b, 0, h)),
                  pl.BlockSpec((1, L, vw), lambda b, h, i: (b, 0, h))],
        out_specs=pl.BlockSpec((1, tq, vw), lambda b, h, i: (b, i, h)),
        out_shape=jax.ShapeDtypeStruct((B, L, MLA_HEADS * MLA_DV), BF16),
        scratch_shapes=[pltpu.VMEM((nh, tq, tq), F32), pltpu.VMEM((nh, 1, tq), F32),
                        pltpu.VMEM((nh, 1, tq), F32), pltpu.VMEM((nh, 1, tq), F32),
                        pltpu.VMEM((nh, MLA_DV, tq), F32), pltpu.VMEM((nh, MLA_DV, L), BF16),
                        pltpu.VMEM((nh, MLA_HEAD_PAD, tq), BF16)],
        compiler_params=_cparams(("parallel", "parallel", "arbitrary")),
        name="mla_attention",
    )(mq.reshape(B, L, -1), mk.reshape(B, L, -1), mv.reshape(B, L, -1))


def _merge_kernel(x_ref, xn_ref, gin_ref, gout_ref, a_ref, y_ref, o_ref_in, wg_ref, wro_ref,
                  wga_ref, wgb_ref, wmo_ref, wout_ref, out_ref, ha_ref):
    pr = ha_ref.shape[0]
    n_pieces = x_ref.shape[0] // pr
    D = x_ref.shape[1]
    gin = gin_ref[...]
    gout = gout_ref[...]

    @pl.when(pl.program_id(0) == 0)
    def _():
        ha_ref[...] = _rms(x_ref[:pr, :], gin).astype(BF16)

    def branches(h, piece):
        rows = slice(piece * pr, (piece + 1) * pr)
        ys = y_ref[rows, :].astype(BF16)
        return (_dot(h, wg_ref[:, 0:D]), _dot(a_ref[rows, :], wro_ref[...]),
                _dot(h, wg_ref[:, D:2 * D]), _dot(ys, wga_ref[...]), _dot(ys, wgb_ref[...]),
                _dot(h, wg_ref[:, 2 * D:3 * D]), _dot(o_ref_in[rows, :], wmo_ref[...]))

    def combine(g0, y_ret, g1, ya, yb, g2, y_mla):
        merged = jax.nn.sigmoid(g0) * y_ret
        merged = merged + jax.nn.sigmoid(g1) * (ya * jax.nn.sigmoid(yb))
        merged = merged + jax.nn.sigmoid(g2) * y_mla
        return merged.astype(BF16)

    def finish(piece, o):
        rows = slice(piece * pr, (piece + 1) * pr)
        out_ref[rows, :] = x_ref[rows, :] + _rms(o, gout)

    b_prev = branches(ha_ref[...], 0)
    o_prev = None
    for piece in range(1, n_pieces):
        b = branches(_rms(x_ref[piece * pr:(piece + 1) * pr, :], gin).astype(BF16), piece)
        o = _dot(combine(*b_prev), wout_ref[...])
        if o_prev is not None:
            finish(piece - 2, o_prev)
        b_prev, o_prev = b, o
    o = _dot(combine(*b_prev), wout_ref[...])
    if o_prev is not None:
        finish(n_pieces - 2, o_prev)
    ha_ref[...] = _rms(xn_ref[...], gin).astype(BF16)
    finish(n_pieces - 1, o)


def _merge(x2, a_ret, y_s5, o_mla, p, l):
    T, D = x2.shape
    tm = MERGE_PIECE * MERGE_PIECES
    n = T // tm
    row = lambda w: pl.BlockSpec((tm, w), lambda i: (i, 0))
    nxt = pl.BlockSpec((MERGE_PIECE, D),
                       lambda i: (jnp.minimum(i + 1, n - 1) * MERGE_PIECES, 0))
    n_sub = p["gains"].shape[0] // p["wgates"].shape[0]
    names = ["wgates", "ret_wo", "glu_a", "glu_b", "mla_wo", "w_out"]
    return pl.pallas_call(
        _merge_kernel,
        grid=(n,),
        in_specs=[row(D), nxt, _slab_spec(p["gains"], (l * n_sub + 2,)),
                  _slab_spec(p["gains"], (l * n_sub + 3,)),
                  row(a_ret.shape[1]), row(y_s5.shape[1]), row(o_mla.shape[1])]
                 + [_slab_spec(p[n], (l,)) for n in names],
        out_specs=row(D),
        out_shape=jax.ShapeDtypeStruct((T, D), F32),
        scratch_shapes=[pltpu.VMEM((MERGE_PIECE, D), BF16)],
        compiler_params=_cparams(("arbitrary",)),
        name="gated_merge",
    )(x2, x2, p["gains"], p["gains"], a_ret, y_s5, o_mla, *[p[n] for n in names])


def _prepare(norm_gains, ffn_w_gate, ffn_w_up, ffn_w_down, w_in, ret_w_o,
             s5_a_re, s5_a_im, s5_log_dt, s5_b_re, s5_b_im, s5_c_re, s5_c_im, s5_d,
             s5_glu_a, s5_glu_b, mla_q_norm, mla_kv_norm, mla_w_uq, mla_w_ukv, mla_w_o, w_out):
    depth, n_sub, D = norm_gains.shape
    q_rank = mla_q_norm.shape[-1]
    kv_rank = mla_kv_norm.shape[-1]
    s5_w = s5_d.shape[-1]
    ret_w = 2 * RET_HEADS * RET_DK + 2 * RET_HEADS * RET_DV
    m_lo = ret_w + s5_w
    m_w = q_rank + kv_rank + MLA_ROPE
    assert w_in.shape[-1] == m_lo + m_w + 3 * D
    hd_w = MLA_NOPE + MLA_ROPE
    wuq = jnp.pad(mla_w_uq.reshape(depth, q_rank, MLA_HEADS, hd_w),
                  ((0, 0), (0, 0), (0, 0), (0, MLA_HEAD_PAD - hd_w)))
    wukv = mla_w_ukv.reshape(depth, kv_rank, MLA_HEADS, MLA_NOPE + MLA_DV)
    p = {
        "gains": norm_gains.reshape(depth * n_sub, 1, D),
        "ffn_wg": ffn_w_gate.astype(BF16), "ffn_wu": ffn_w_up.astype(BF16),
        "ffn_wd": ffn_w_down.astype(BF16),
        "w_in": w_in.astype(BF16),
        "wm": jnp.pad(w_in[:, :, m_lo:m_lo + m_w],
                      ((0, 0), (0, 0), (0, LANES - MLA_ROPE))).astype(BF16),
        "wgates": w_in[:, :, m_lo + m_w:].astype(BF16),
        "qn": mla_q_norm.reshape(depth, 1, q_rank), "kvn": mla_kv_norm.reshape(depth, 1, kv_rank),
        "wuq": wuq.reshape(depth, q_rank, MLA_HEADS * MLA_HEAD_PAD).astype(BF16),
        "wuk": wukv[..., :MLA_NOPE].reshape(depth, kv_rank, MLA_HEADS * MLA_NOPE).astype(BF16),
        "wuv": wukv[..., MLA_NOPE:].reshape(depth, kv_rank, MLA_HEADS * MLA_DV).astype(BF16),
        "ret_wo": ret_w_o.astype(BF16), "glu_a": s5_glu_a.astype(BF16),
        "glu_b": s5_glu_b.astype(BF16), "mla_wo": mla_w_o.astype(BF16),
        "w_out": w_out.astype(BF16),
    }
    p.update(_s5_params(s5_a_re, s5_a_im, s5_log_dt, s5_b_re, s5_b_im, s5_c_re, s5_c_im, s5_d))
    return p


def kernel(x, positions, norm_gains, ffn_w_gate, ffn_w_up, ffn_w_down, w_in, ret_w_o,
           s5_a_re, s5_a_im, s5_log_dt, s5_b_re, s5_b_im, s5_c_re, s5_c_im, s5_d,
           s5_glu_a, s5_glu_b, mla_q_norm, mla_kv_norm, mla_w_uq, mla_w_ukv, mla_w_o, w_out):
    B, L, D = x.shape
    depth = norm_gains.shape[0]
    assert L % ROW_TILE == 0 and L % RET_ROWS == 0 and L % ATTN_TQ == 0 and L % S5_TL == 0
    assert B == SUBLANES, "the S5 scan keeps the batch on the sublane axis"

    p = _prepare(norm_gains, ffn_w_gate, ffn_w_up, ffn_w_down, w_in, ret_w_o,
                 s5_a_re, s5_a_im, s5_log_dt, s5_b_re, s5_b_im, s5_c_re, s5_c_im, s5_d,
                 s5_glu_a, s5_glu_b, mla_q_norm, mla_kv_norm, mla_w_uq, mla_w_ukv, mla_w_o, w_out)
    tabs = _rope_tables(positions)
    ret_tabs = _ret_tables()
    x2 = x.reshape(B * L, D)
    for l in range(depth):
        x2 = _ffn(x2, p, l, 0)
        rq, rk, rv, rg, u, mq, mk, mv = _proj(x2, p, l, tabs)
        a_ret = _retention(rq, rk, rv, rg, ret_tabs, B, L)
        y_s5 = _s5(u, p, l, B, L)
        o_mla = _attention(mq, mk, mv, B, L)
        x2 = _merge(x2, a_ret.reshape(B * L, -1), y_s5.reshape(B * L, -1),
                    o_mla.reshape(B * L, -1), p, l)
        x2 = _ffn(x2, p, l, 1)
    return x2.reshape(B, L, D)
```

```python
import math

import jax
import jax.numpy as jnp
from jax import lax
from jax.experimental import pallas as pl
from jax.experimental.pallas import tpu as pltpu

F32 = jnp.float32
BF16 = jnp.bfloat16

ROPE_BASE = 10000.0
NORM_EPS = 1e-6
GN_EPS = 1e-5
FFN_RES = 0.5
RET_HEADS = 4
RET_DK = 128
RET_DV = 128
RET_CHUNK = 128
MLA_HEADS = 4
MLA_NOPE = 128
MLA_ROPE = 64
MLA_DV = 128
MLA_HEAD_PAD = 256
MLA_LOG2_SCALE = (MLA_NOPE + MLA_ROPE) ** -0.5 * math.log2(math.e)

LANES = 128
SUBLANES = 8
VMEM_LIMIT = 56 * 1024 * 1024

ROW_TILE = 512
FFN_PIECE = 256
FFN_PIECES = 4
MERGE_PIECE = 256
MERGE_PIECES = 2
FFN_CHUNKS = ((0, 1024), (1024, 2048), (2048, 2816))
RET_ROWS = 512
S5_TL = 128
S5_HALF = 1024
ATTN_TQ = 512
ATTN_HEADS_PER_STEP = 4
ATTN_LOOKAHEAD = 2


def _cparams(sem):
    return pltpu.CompilerParams(dimension_semantics=sem, vmem_limit_bytes=VMEM_LIMIT)


def _const_spec(shape):
    nd = len(shape)
    return pl.BlockSpec(shape, lambda *_: (0,) * nd, pipeline_mode=pl.Buffered(1))


def _slab_spec(arr, lead, block=None, col=0):
    tail = tuple(arr.shape[len(lead):]) if block is None else tuple(block)
    idx = tuple(lead) + (0,) * (len(tail) - 1) + (col,)
    return pl.BlockSpec((None,) * len(lead) + tail, lambda *_: idx,
                        pipeline_mode=pl.Buffered(1))


def _rms(x, g):
    y = x * lax.rsqrt(jnp.mean(x * x, axis=-1, keepdims=True) + NORM_EPS)
    return y * g


def _dot(a, b):
    return jnp.dot(a, b, preferred_element_type=F32)


def _rope_kernel(pos_ref, f_ref, a_ref, na_ref, b_ref, p_ref, np_ref, q_ref,
                 cr_ref, sr_ref, cm_ref, s1_ref, s2_ref):
    ang = pos_ref[...] * f_ref[...]
    c = jnp.cos(ang)
    s = jnp.sin(ang)
    hr = RET_DK // 2
    c_hi = pltpu.roll(c, hr, 1)
    s_hi = pltpu.roll(s, hr, 1)
    c_m2 = pltpu.roll(c, hr + MLA_ROPE // 2, 1)
    s_m2 = pltpu.roll(s, hr + MLA_ROPE // 2, 1)
    cr_ref[...] = c * a_ref[...] + c_hi * b_ref[...]
    sr_ref[...] = s * na_ref[...] + s_hi * b_ref[...]
    cm_ref[...] = c_hi * p_ref[...] + c_m2 * q_ref[...]
    s1_ref[...] = s_hi * np_ref[...]
    s2_ref[...] = s_m2 * q_ref[...]


def _rope_tables(positions):
    B, L = positions.shape
    T = B * L
    pos = positions.astype(F32).reshape(T, 1)
    inv_r = ROPE_BASE ** (-jnp.arange(0, RET_DK, 2, dtype=F32) / RET_DK)
    inv_m = ROPE_BASE ** (-jnp.arange(0, MLA_ROPE, 2, dtype=F32) / MLA_ROPE)
    hr = RET_DK // 2
    hm = MLA_ROPE // 2
    lane = jnp.arange(LANES)
    freq = jnp.concatenate([inv_r, inv_m, jnp.zeros((LANES - hr - hm,), F32)])
    a = (lane < hr).astype(F32)
    b = (lane >= hr).astype(F32)
    p = (lane < hm).astype(F32)
    q = ((lane >= hm) & (lane < 2 * hm)).astype(F32)
    rows = [r.reshape(1, LANES) for r in (freq, a, -a, b, p, -p, q)]
    tm = 2048
    row_spec = pl.BlockSpec((1, LANES), lambda i: (0, 0))
    tab_spec = pl.BlockSpec((tm, LANES), lambda i: (i, 0))
    tab_shape = jax.ShapeDtypeStruct((T, LANES), F32)
    return pl.pallas_call(
        _rope_kernel,
        grid=(T // tm,),
        in_specs=[pl.BlockSpec((tm, 1), lambda i: (i, 0))] + [row_spec] * len(rows),
        out_specs=[tab_spec] * 5,
        out_shape=[tab_shape] * 5,
        compiler_params=_cparams(("parallel",)),
        name="rope_tables",
    )(pos, *rows)


def _ffn_kernel(x_ref, xn_ref, gpre_ref, gpost_ref, wg_ref, wu_ref, wd_ref, o_ref, ha_ref):
    pr = ha_ref.shape[0]
    n_pieces = x_ref.shape[0] // pr
    gpre = gpre_ref[...]
    gpost = gpost_ref[...]

    @pl.when(pl.program_id(0) == 0)
    def _():
        ha_ref[...] = _rms(x_ref[:pr, :], gpre).astype(BF16)

    def swiglu(h):
        y = None
        pending = None
        for c0, c1 in FFN_CHUNKS + ((None, None),):
            if c0 is not None:
                g = _dot(h, wg_ref[:, c0:c1])
                u = _dot(h, wu_ref[:, c0:c1])
            if pending is not None:
                a, p0, p1 = pending
                part = _dot(a, wd_ref[p0:p1, :])
                y = part if y is None else y + part
            if c0 is not None:
                pending = ((g * jax.nn.sigmoid(g) * u).astype(BF16), c0, c1)
        return y

    def finish(piece, y):
        rows = slice(piece * pr, (piece + 1) * pr)
        o_ref[rows, :] = x_ref[rows, :] + FFN_RES * _rms(y, gpost)

    y_prev = swiglu(ha_ref[...])
    for piece in range(1, n_pieces):
        y = swiglu(_rms(x_ref[piece * pr:(piece + 1) * pr, :], gpre).astype(BF16))
        finish(piece - 1, y_prev)
        y_prev = y
    ha_ref[...] = _rms(xn_ref[...], gpre).astype(BF16)
    finish(n_pieces - 1, y_prev)


def _ffn(x2, p, l, k):
    T, D = x2.shape
    half = FFN_PIECE
    tile = FFN_PIECE * FFN_PIECES
    n = T // tile
    row = pl.BlockSpec((tile, D), lambda i: (i, 0))
    nxt = pl.BlockSpec((half, D), lambda i: (jnp.minimum(i + 1, n - 1) * FFN_PIECES, 0))
    n_sub = p["gains"].shape[0] // p["ffn_wg"].shape[0]
    gi = l * n_sub + 4 * k
    return pl.pallas_call(
        _ffn_kernel,
        grid=(n,),
        in_specs=[row, nxt, _slab_spec(p["gains"], (gi,)), _slab_spec(p["gains"], (gi + 1,)),
                  _slab_spec(p["ffn_wg"], (l, k)), _slab_spec(p["ffn_wu"], (l, k)),
                  _slab_spec(p["ffn_wd"], (l, k))],
        out_specs=row,
        out_shape=jax.ShapeDtypeStruct((T, D), F32),
        scratch_shapes=[pltpu.VMEM((half, D), BF16)],
        compiler_params=_cparams(("arbitrary",)),
        name="ffn",
    )(x2, x2, p["gains"], p["gains"], p["ffn_wg"], p["ffn_wu"], p["ffn_wd"])


def _rope_ret(v, cos, sin):
    return v * cos + pltpu.roll(v, RET_DK // 2, 1) * sin


def _rope_mla(v, cos, sin1, sin2):
    hm = MLA_ROPE // 2
    return v * cos + pltpu.roll(v, LANES - hm, 1) * sin1 + pltpu.roll(v, hm, 1) * sin2


def _proj_kernel(x_ref, xn_ref, g_ref, wr_ref, wu_ref, wm_ref, qn_ref, kvn_ref,
                 wuq_ref, wuk_ref, wuv_ref, cr_ref, sr_ref, cm_ref, s1_ref, s2_ref,
                 rq_ref, rk_ref, rv_ref, rg_ref, u_ref, mq_ref, mk_ref, mv_ref, ha_ref):
    half = x_ref.shape[0] // 2
    gain = g_ref[...]
    qk_w = RET_HEADS * RET_DK
    v_w = RET_HEADS * RET_DV
    q_rank = qn_ref.shape[1]
    kv_rank = kvn_ref.shape[1]

    @pl.when(pl.program_id(0) == 0)
    def _():
        ha_ref[...] = _rms(x_ref[:half, :], gain).astype(BF16)

    def project(h):
        m = _dot(h, wm_ref[...])
        r = _dot(h, wr_ref[...])
        u = _dot(h, wu_ref[...])
        return m, r, u

    def mla_up(m):
        cq = _rms(m[:, :q_rank], qn_ref[...]).astype(BF16)
        ckv = _rms(m[:, q_rank:q_rank + kv_rank], kvn_ref[...]).astype(BF16)
        q = _dot(cq, wuq_ref[...])
        kn = _dot(ckv, wuk_ref[...])
        v = _dot(ckv, wuv_ref[...])
        return q, kn, v

    def store(rows, m, r, u, q, kn, v):
        cr = cr_ref[rows, :]
        sr = sr_ref[rows, :]
        for hd in range(RET_HEADS):
            lo = hd * RET_DK
            hcols = slice(lo, lo + RET_DK)
            qr = _rope_ret(r[:, lo:lo + RET_DK], cr, sr)
            kr = _rope_ret(r[:, qk_w + lo:qk_w + lo + RET_DK], cr, sr) * (RET_DK ** -0.5)
            rq_ref[rows, hcols] = qr.astype(BF16)
            rk_ref[rows, hcols] = kr.astype(BF16)
        rv_ref[rows, :] = r[:, 2 * qk_w:2 * qk_w + v_w].astype(BF16)
        rg_ref[rows, :] = r[:, 2 * qk_w + v_w:].astype(BF16)
        u_ref[rows, :] = u
        cm = cm_ref[rows, :]
        s1 = s1_ref[rows, :]
        s2 = s2_ref[rows, :]
        kpe = _rope_mla(m[:, q_rank + kv_rank:], cm, s1, s2).astype(BF16)
        mv_ref[rows, :] = v.astype(BF16)
        for hd in range(MLA_HEADS):
            lo = hd * MLA_HEAD_PAD
            mq_ref[rows, lo:lo + MLA_NOPE] = (q[:, lo:lo + MLA_NOPE] * MLA_LOG2_SCALE).astype(BF16)
            mq_ref[rows, lo + MLA_NOPE:lo + MLA_HEAD_PAD] = (_rope_mla(
                q[:, lo + MLA_NOPE:lo + MLA_HEAD_PAD], cm, s1, s2) * MLA_LOG2_SCALE).astype(BF16)
            mk_ref[rows, lo:lo + MLA_NOPE] = kn[:, hd * MLA_NOPE:(hd + 1) * MLA_NOPE].astype(BF16)
            mk_ref[rows, lo + MLA_NOPE:lo + MLA_HEAD_PAD] = kpe

    pa = project(ha_ref[...])
    pb = project(_rms(x_ref[half:, :], gain).astype(BF16))
    ua = mla_up(pa[0])
    ub = mla_up(pb[0])
    store(slice(0, half), *pa, *ua)
    ha_ref[...] = _rms(xn_ref[...], gain).astype(BF16)
    store(slice(half, 2 * half), *pb, *ub)


def _proj(x2, p, l, tabs):
    T, D = x2.shape
    tm = ROW_TILE
    n = T // tm
    row = lambda n: pl.BlockSpec((tm, n), lambda i: (i, 0))
    nxt = pl.BlockSpec((tm // 2, D), lambda i: (jnp.minimum(2 * i + 2, 2 * n - 2), 0))
    n_sub = p["gains"].shape[0] // p["w_in"].shape[0]
    ret_w = 2 * RET_HEADS * RET_DK + 2 * RET_HEADS * RET_DV
    s5_w = p["s5_d"].shape[-1]
    assert ret_w % s5_w == 0
    consts = [(p["gains"], _slab_spec(p["gains"], (l * n_sub + 2,))),
              (p["w_in"], _slab_spec(p["w_in"], (l,), block=(D, ret_w), col=0)),
              (p["w_in"], _slab_spec(p["w_in"], (l,), block=(D, s5_w), col=ret_w // s5_w)),
              (p["wm"], _slab_spec(p["wm"], (l,))),
              (p["qn"], _slab_spec(p["qn"], (l,))), (p["kvn"], _slab_spec(p["kvn"], (l,))),
              (p["wuq"], _slab_spec(p["wuq"], (l,))), (p["wuk"], _slab_spec(p["wuk"], (l,))),
              (p["wuv"], _slab_spec(p["wuv"], (l,)))]
    out_w = [(RET_HEADS * RET_DK, BF16)] * 2 + [
             (RET_HEADS * RET_DV, BF16), (RET_HEADS * RET_DV, BF16), (s5_w, F32),
             (MLA_HEADS * MLA_HEAD_PAD, BF16), (MLA_HEADS * MLA_HEAD_PAD, BF16),
             (MLA_HEADS * MLA_DV, BF16)]
    return pl.pallas_call(
        _proj_kernel,
        grid=(n,),
        in_specs=[row(D), nxt] + [s for _, s in consts] + [row(LANES)] * 5,
        out_specs=[row(w) for w, _ in out_w],
        out_shape=[jax.ShapeDtypeStruct((T, w), dt) for w, dt in out_w],
        scratch_shapes=[pltpu.VMEM((tm // 2, D), BF16)],
        compiler_params=_cparams(("arbitrary",)),
        name="mixer_proj",
    )(x2, x2, *[a for a, _ in consts], *tabs)


def _ret_kernel(q_ref, k_ref, v_ref, g_ref, intra_ref, qd_ref, kd_ref, cd_ref, o_ref, st_ref):
    @pl.when(pl.program_id(1) == 0)
    def _():
        st_ref[...] = jnp.zeros_like(st_ref)

    C = RET_CHUNK
    nt = (((1,), (1,)), ((), ()))
    tn = (((0,), (0,)), ((), ()))
    states = [st_ref[hd] for hd in range(RET_HEADS)]
    for c in range(RET_ROWS // C):
        rows = slice(c * C, (c + 1) * C)
        for hd in range(RET_HEADS):
            cols = slice(hd * RET_DK, (hd + 1) * RET_DK)
            vcols = slice(hd * RET_DV, (hd + 1) * RET_DV)
            q = q_ref[0, rows, cols]
            k = k_ref[0, rows, cols]
            v = v_ref[0, rows, vcols]
            scores = lax.dot_general(q, k, nt, preferred_element_type=F32) * intra_ref[hd]
            inner = _dot(scores.astype(BF16), v)
            cross = _dot((q.astype(F32) * qd_ref[hd]).astype(BF16), states[hd].astype(BF16))
            kv = lax.dot_general((k.astype(F32) * kd_ref[hd]).astype(BF16), v, tn,
                                 preferred_element_type=F32)
            states[hd] = cd_ref[hd] * states[hd] + kv
            o = inner + cross
            mu = jnp.mean(o, axis=-1, keepdims=True)
            var = jnp.mean(jnp.square(o - mu), axis=-1, keepdims=True)
            on = (o - mu) * lax.rsqrt(var + GN_EPS)
            g = g_ref[0, rows, vcols].astype(F32)
            o_ref[0, rows, vcols] = (g * jax.nn.sigmoid(g) * on).astype(BF16)
    for hd in range(RET_HEADS):
        st_ref[hd] = states[hd]


def _ret_tables():
    C = RET_CHUNK
    log_gamma = jnp.log1p(-jnp.exp2(-5.0 - jnp.arange(RET_HEADS, dtype=F32)))
    pos = jnp.arange(C, dtype=F32)
    rel = pos[:, None] - pos[None, :]
    intra = jnp.where(rel[None] >= 0.0,
                      jnp.exp(jnp.maximum(rel, 0.0)[None] * log_gamma[:, None, None]), 0.0)
    k_decay = jnp.exp((C - 1.0 - pos)[:, None] * log_gamma[None, :])
    q_decay = jnp.exp((pos + 1.0)[:, None] * log_gamma[None, :])
    chunk_decay = jnp.exp(C * log_gamma)
    qd = jnp.broadcast_to(q_decay.T[:, :, None], (RET_HEADS, C, RET_DK))
    kd = jnp.broadcast_to(k_decay.T[:, :, None], (RET_HEADS, C, RET_DK))
    cd = jnp.broadcast_to(chunk_decay[:, None, None], (RET_HEADS, 1, RET_DV))
    return intra, qd, kd, cd


def _retention(rq, rk, rv, rg, tables, B, L):
    W = rq.shape[-1]
    shp = (B, L, W)
    blk = pl.BlockSpec((1, RET_ROWS, W), lambda b, i: (b, i, 0))
    seq = [a.reshape(shp) for a in (rq, rk, rv, rg)]
    return pl.pallas_call(
        _ret_kernel,
        grid=(B, L // RET_ROWS),
        in_specs=[blk] * len(seq) + [_const_spec(t.shape) for t in tables],
        out_specs=blk,
        out_shape=jax.ShapeDtypeStruct(shp, BF16),
        scratch_shapes=[pltpu.VMEM((RET_HEADS, RET_DK, RET_DV), F32)],
        compiler_params=_cparams(("parallel", "arbitrary")),
        name="retention",
    )(*seq, *tables)


def _s5_zoh_kernel(are_ref, aim_ref, ldt_ref, bre_ref, bim_ref,
                   abr_ref, abi_ref, bbr_ref, bbi_ref):
    a_re = are_ref[...]
    a_im = aim_ref[...]
    dt = jnp.exp(ldt_ref[...])
    mag = jnp.exp(a_re * dt)
    abar_re = mag * jnp.cos(a_im * dt)
    abar_im = mag * jnp.sin(a_im * dt)
    den = a_re * a_re + a_im * a_im
    nr = abar_re - 1.0
    f_re = (nr * a_re + abar_im * a_im) / den
    f_im = (abar_im * a_re - nr * a_im) / den
    b_re = bre_ref[...]
    b_im = bim_ref[...]
    abr_ref[...] = abar_re
    abi_ref[...] = abar_im
    bbr_ref[...] = f_re * b_re - f_im * b_im
    bbi_ref[...] = f_re * b_im + f_im * b_re


def _s5_params(a_re, a_im, log_dt, b_re, b_im, c_re, c_im, d):
    Dp, G, P = a_re.shape
    H = b_re.shape[-1]
    R = Dp * G * H
    rep = lambda a: jnp.broadcast_to(a[:, :, None, :], (Dp, G, H, a.shape[-1])).reshape(R, -1)
    bt = lambda b: jnp.swapaxes(b, 2, 3).reshape(R, P)
    args = [rep(a_re), rep(a_im), rep(log_dt[:, :, None]), bt(b_re), bt(b_im)]
    out = jax.ShapeDtypeStruct((R, P), F32)
    abr, abi, bbr, bbi = pl.pallas_call(
        _s5_zoh_kernel, out_shape=[out] * 4, name="s5_discretise",
        compiler_params=pltpu.CompilerParams(vmem_limit_bytes=VMEM_LIMIT),
    )(*args)
    abar_re = abr.reshape(Dp, G, H, P)[:, :, 0, :].reshape(Dp, 1, G * P)
    abar_im = abi.reshape(Dp, G, H, P)[:, :, 0, :].reshape(Dp, 1, G * P)
    gpb = LANES // H
    nblk = G // gpb
    eye = jnp.eye(gpb, dtype=F32)

    def pack_in(bb):
        bb = bb.reshape(Dp, nblk, gpb, H, P)
        m = jnp.einsum("dkghp,gf->dkghfp", bb, eye)
        return m.reshape(Dp, nblk, gpb * H, gpb * P).astype(BF16)

    def pack_out(cc):
        cc = cc.reshape(Dp, nblk, gpb, H, P)
        m = jnp.einsum("dkghp,gf->dkgpfh", cc, eye)
        return m.reshape(Dp, nblk, gpb * P, gpb * H).astype(BF16)

    return {"s5_are": abar_re, "s5_aim": abar_im,
            "s5_wbr": pack_in(bbr), "s5_wbi": pack_in(bbi),
            "s5_wcr": pack_out(c_re), "s5_wci": pack_out(c_im),
            "s5_d": d.reshape(Dp, 1, G * H)}


def _s5_kernel(u_ref, are_ref, aim_ref, wbr_ref, wbi_ref, wcr_ref, wci_ref, d_ref, y_ref,
               sre_ref, sim_ref, *block_refs):
    B, tl, W = u_ref.shape
    nblk = wbr_ref.shape[0]
    in_w = wbr_ref.shape[1]
    st_w = wbr_ref.shape[2]
    per_block = len(block_refs) // nblk
    blocks = [block_refs[kb * per_block:(kb + 1) * per_block] for kb in range(nblk)]

    @pl.when(pl.program_id(0) == 0)
    def _():
        sre_ref[...] = jnp.zeros_like(sre_ref)
        sim_ref[...] = jnp.zeros_like(sim_ref)

    assert in_w == LANES and nblk * LANES == W

    def gather(kb):
        u2_ref, ut_ref = blocks[kb][0], blocks[kb][1]
        for b in range(B):
            u2_ref[b * tl:(b + 1) * tl, :] = u_ref[b, :, kb * LANES:(kb + 1) * LANES]
        for t in range(tl):
            ut_ref[t * B:(t + 1) * B, :] = u2_ref[pl.ds(t, B, stride=tl), :]

    def emit(kb):
        yt_ref = blocks[kb][4]
        for b in range(B):
            y_ref[b, :, kb * LANES:(kb + 1) * LANES] = yt_ref[pl.ds(b, tl, stride=B), :]

    def drive(kb):
        _, ut_ref, bre_ref, bim_ref, _ = blocks[kb]
        lhs = ut_ref[...].astype(BF16)
        bre_ref[...] = _dot(lhs, wbr_ref[kb])
        bim_ref[...] = _dot(lhs, wbi_ref[kb])

    def scan(kb):
        _, _, bre_ref, bim_ref, _ = blocks[kb]
        sl = slice(kb * st_w, (kb + 1) * st_w)
        ar = jnp.broadcast_to(are_ref[:, sl], (B, st_w))
        ai = jnp.broadcast_to(aim_ref[:, sl], (B, st_w))
        sr = sre_ref[:, sl]
        si = sim_ref[:, sl]
        for t in range(tl):
            rows = slice(t * B, (t + 1) * B)
            sr, si = (ar * sr - ai * si + bre_ref[rows, :],
                      ar * si + ai * sr + bim_ref[rows, :])
            bre_ref[rows, :] = sr
            bim_ref[rows, :] = si
        sre_ref[:, sl] = sr
        sim_ref[:, sl] = si

    def project(kb):
        _, ut_ref, bre_ref, bim_ref, yt_ref = blocks[kb]
        y = (_dot(bre_ref[...].astype(BF16), wcr_ref[kb])
             - _dot(bim_ref[...].astype(BF16), wci_ref[kb]))
        yt_ref[...] = jax.nn.gelu(y + d_ref[:, kb * in_w:(kb + 1) * in_w] * ut_ref[...])

    gather(0)
    drive(0)
    for kb in range(nblk):
        if kb + 1 < nblk:
            gather(kb + 1)
            drive(kb + 1)
        scan(kb)
        if kb >= 1:
            project(kb - 1)
            emit(kb - 1)
    project(nblk - 1)
    emit(nblk - 1)


def _s5(u, p, l, B, L):
    W = u.shape[-1]
    tl = S5_TL
    n_state = p["s5_are"].shape[-1]
    nblk, in_w, st_w = p["s5_wbr"].shape[1:]
    blk = pl.BlockSpec((B, tl, W), lambda i: (0, i, 0))
    names = ["s5_are", "s5_aim", "s5_wbr", "s5_wbi", "s5_wcr", "s5_wci", "s5_d"]
    rows = B * tl
    per_block = [pltpu.VMEM((rows, in_w), F32), pltpu.VMEM((rows, in_w), F32),
                 pltpu.VMEM((rows, st_w), F32), pltpu.VMEM((rows, st_w), F32),
                 pltpu.VMEM((rows, in_w), F32)]
    return pl.pallas_call(
        _s5_kernel,
        grid=(L // tl,),
        in_specs=[blk] + [_slab_spec(p[n], (l,)) for n in names],
        out_specs=blk,
        out_shape=jax.ShapeDtypeStruct((B, L, W), F32),
        scratch_shapes=[pltpu.VMEM((B, n_state), F32), pltpu.VMEM((B, n_state), F32)]
                       + per_block * nblk,
        compiler_params=_cparams(("arbitrary",)),
        name="s5_scan",
    )(u.reshape(B, L, W), *[p[n] for n in names])


def _attn_kernel(q_ref, k_ref, v_ref, o_ref, s_ref, m_ref, mprev_ref, l_ref, acc_ref, vt_ref,
                 qt_ref):
    tq = q_ref.shape[1]
    nh = m_ref.shape[0]
    qi = pl.program_id(2)
    m_ref[...] = jnp.full(m_ref.shape, -1e30, F32)
    l_ref[...] = jnp.zeros_like(l_ref)
    acc_ref[...] = jnp.zeros_like(acc_ref)
    for h in range(nh):
        qt_ref[h] = q_ref[0, :, h * MLA_HEAD_PAD:(h + 1) * MLA_HEAD_PAD].T

    def scores(j, h, masked):
        r0 = pl.multiple_of(j * tq, tq)
        qk = slice(h * MLA_HEAD_PAD, (h + 1) * MLA_HEAD_PAD)
        st = _dot(k_ref[0, pl.ds(r0, tq), qk], qt_ref[h])
        if masked:
            kidx = lax.broadcasted_iota(jnp.int32, st.shape, 0)
            qidx = lax.broadcasted_iota(jnp.int32, st.shape, 1)
            st = jnp.where(kidx <= qidx, st, -jnp.inf)
        s_ref[h] = st
        m_old = m_ref[h]
        mprev_ref[h] = m_old
        m_ref[h] = jnp.maximum(m_old, jnp.max(st, axis=0, keepdims=True))

    @pl.when(qi == 0)
    def _():
        for h in range(nh):
            for blk in range(v_ref.shape[1] // tq):
                rows = slice(blk * tq, (blk + 1) * tq)
                vt_ref[h, :, rows] = v_ref[0, rows, h * MLA_DV:(h + 1) * MLA_DV].T

    def accumulate(j, h):
        r0 = pl.multiple_of(j * tq, tq)
        m_new = m_ref[h]
        alpha = jnp.exp2(mprev_ref[h] - m_new)
        pt = jnp.exp2(s_ref[h] - m_new)
        l_ref[h] = alpha * l_ref[h] + jnp.sum(pt, axis=0, keepdims=True)
        acc_ref[h] = alpha * acc_ref[h] + _dot(vt_ref[h, :, pl.ds(r0, tq)],
                                               pt.astype(BF16))

    la = ATTN_LOOKAHEAD
    assert 0 < la < nh
    for h in range(nh):
        scores(qi, h, True)
        if h >= la:
            accumulate(qi, h - la)

    def one_block(j):
        prev = jnp.where(j == 0, qi, j - 1)
        for h in range(nh):
            scores(j, h, False)
            if h >= la:
                accumulate(j, h - la)
            else:
                accumulate(prev, h - la + nh)

    odd = qi % 2

    @pl.when(odd == 1)
    def _():
        one_block(0)

    def body(t, carry):
        j = odd + 2 * t
        one_block(j)
        one_block(j + 1)
        return carry
    lax.fori_loop(0, qi // 2, body, 0)
    last = jnp.maximum(qi - 1, 0)
    for h in range(nh - la, nh):
        accumulate(last, h)
    for h in range(nh):
        o_ref[0, :, h * MLA_DV:(h + 1) * MLA_DV] = (acc_ref[h] / l_ref[h]).T.astype(BF16)


def _attention(mq, mk, mv, B, L):
    tq = ATTN_TQ
    nh = ATTN_HEADS_PER_STEP
    qw = nh * MLA_HEAD_PAD
    vw = nh * MLA_DV
    return pl.pallas_call(
        _attn_kernel,
        grid=(B, MLA_HEADS // nh, L // tq),
        in_specs=[pl.BlockSpec((1, tq, qw), lambda b, h, i: (b, i, h)),
                  pl.BlockSpec((1, L, qw), lambda b, h, i: (b, 0, h)),
                  pl.BlockSpec((1, L, vw), lambda b, h, i: (b, 0, h))],
        out_specs=pl.BlockSpec((1, tq, vw), lambda b, h, i: (b, i, h)),
        out_shape=jax.ShapeDtypeStruct((B, L, MLA_HEADS * MLA_DV), BF16),
        scratch_shapes=[pltpu.VMEM((nh, tq, tq), F32), pltpu.VMEM((nh, 1, tq), F32),
                        pltpu.VMEM((nh, 1, tq), F32), pltpu.VMEM((nh, 1, tq), F32),
                        pltpu.VMEM((nh, MLA_DV, tq), F32), pltpu.VMEM((nh, MLA_DV, L), BF16),
                        pltpu.VMEM((nh, MLA_HEAD_PAD, tq), BF16)],
        compiler_params=_cparams(("parallel", "parallel", "arbitrary")),
        name="mla_attention",
    )(mq.reshape(B, L, -1), mk.reshape(B, L, -1), mv.reshape(B, L, -1))


def _merge_kernel(x_ref, xn_ref, gin_ref, gout_ref, a_ref, y_ref, o_ref_in, wg_ref, wro_ref,
                  wga_ref, wgb_ref, wmo_ref, wout_ref, out_ref, ha_ref):
    pr = ha_ref.shape[0]
    n_pieces = x_ref.shape[0] // pr
    D = x_ref.shape[1]
    gin = gin_ref[...]
    gout = gout_ref[...]

    @pl.when(pl.program_id(0) == 0)
    def _():
        ha_ref[...] = _rms(x_ref[:pr, :], gin).astype(BF16)

    def branches(h, piece):
        rows = slice(piece * pr, (piece + 1) * pr)
        ys = y_ref[rows, :].astype(BF16)
        return (_dot(h, wg_ref[:, 0:D]), _dot(a_ref[rows, :], wro_ref[...]),
                _dot(h, wg_ref[:, D:2 * D]), _dot(ys, wga_ref[...]), _dot(ys, wgb_ref[...]),
                _dot(h, wg_ref[:, 2 * D:3 * D]), _dot(o_ref_in[rows, :], wmo_ref[...]))

    def combine(g0, y_ret, g1, ya, yb, g2, y_mla):
        merged = jax.nn.sigmoid(g0) * y_ret
        merged = merged + jax.nn.sigmoid(g1) * (ya * jax.nn.sigmoid(yb))
        merged = merged + jax.nn.sigmoid(g2) * y_mla
        return merged.astype(BF16)

    def finish(piece, o):
        rows = slice(piece * pr, (piece + 1) * pr)
        out_ref[rows, :] = x_ref[rows, :] + _rms(o, gout)

    b_prev = branches(ha_ref[...], 0)
    o_prev = None
    for piece in range(1, n_pieces):
        b = branches(_rms(x_ref[piece * pr:(piece + 1) * pr, :], gin).astype(BF16), piece)
        o = _dot(combine(*b_prev), wout_ref[...])
        if o_prev is not None:
            finish(piece - 2, o_prev)
        b_prev, o_prev = b, o
    o = _dot(combine(*b_prev), wout_ref[...])
    if o_prev is not None:
        finish(n_pieces - 2, o_prev)
    ha_ref[...] = _rms(xn_ref[...], gin).astype(BF16)
    finish(n_pieces - 1, o)


def _merge(x2, a_ret, y_s5, o_mla, p, l):
    T, D = x2.shape
    tm = MERGE_PIECE * MERGE_PIECES
    n = T // tm
    row = lambda w: pl.BlockSpec((tm, w), lambda i: (i, 0))
    nxt = pl.BlockSpec((MERGE_PIECE, D),
                       lambda i: (jnp.minimum(i + 1, n - 1) * MERGE_PIECES, 0))
    n_sub = p["gains"].shape[0] // p["wgates"].shape[0]
    names = ["wgates", "ret_wo", "glu_a", "glu_b", "mla_wo", "w_out"]
    return pl.pallas_call(
        _merge_kernel,
        grid=(n,),
        in_specs=[row(D), nxt, _slab_spec(p["gains"], (l * n_sub + 2,)),
                  _slab_spec(p["gains"], (l * n_sub + 3,)),
                  row(a_ret.shape[1]), row(y_s5.shape[1]), row(o_mla.shape[1])]
                 + [_slab_spec(p[n], (l,)) for n in names],
        out_specs=row(D),
        out_shape=jax.ShapeDtypeStruct((T, D), F32),
        scratch_shapes=[pltpu.VMEM((MERGE_PIECE, D), BF16)],
        compiler_params=_cparams(("arbitrary",)),
        name="gated_merge",
    )(x2, x2, p["gains"], p["gains"], a_ret, y_s5, o_mla, *[p[n] for n in names])


def _prepare(norm_gains, ffn_w_gate, ffn_w_up, ffn_w_down, w_in, ret_w_o,
             s5_a_re, s5_a_im, s5_log_dt, s5_b_re, s5_b_im, s5_c_re, s5_c_im, s5_d,
             s5_glu_a, s5_glu_b, mla_q_norm, mla_kv_norm, mla_w_uq, mla_w_ukv, mla_w_o, w_out):
    depth, n_sub, D = norm_gains.shape
    q_rank = mla_q_norm.shape[-1]
    kv_rank = mla_kv_norm.shape[-1]
    s5_w = s5_d.shape[-1]
    ret_w = 2 * RET_HEADS * RET_DK + 2 * RET_HEADS * RET_DV
    m_lo = ret_w + s5_w
    m_w = q_rank + kv_rank + MLA_ROPE
    assert w_in.shape[-1] == m_lo + m_w + 3 * D
    hd_w = MLA_NOPE + MLA_ROPE
    wuq = jnp.pad(mla_w_uq.reshape(depth, q_rank, MLA_HEADS, hd_w),
                  ((0, 0), (0, 0), (0, 0), (0, MLA_HEAD_PAD - hd_w)))
    wukv = mla_w_ukv.reshape(depth, kv_rank, MLA_HEADS, MLA_NOPE + MLA_DV)
    p = {
        "gains": norm_gains.reshape(depth * n_sub, 1, D),
        "ffn_wg": ffn_w_gate.astype(BF16), "ffn_wu": ffn_w_up.astype(BF16),
        "ffn_wd": ffn_w_down.astype(BF16),
        "w_in": w_in.astype(BF16),
        "wm": jnp.pad(w_in[:, :, m_lo:m_lo + m_w],
                      ((0, 0), (0, 0), (0, LANES - MLA_ROPE))).astype(BF16),
        "wgates": w_in[:, :, m_lo + m_w:].astype(BF16),
        "qn": mla_q_norm.reshape(depth, 1, q_rank), "kvn": mla_kv_norm.reshape(depth, 1, kv_rank),
        "wuq": wuq.reshape(depth, q_rank, MLA_HEADS * MLA_HEAD_PAD).astype(BF16),
        "wuk": wukv[..., :MLA_NOPE].reshape(depth, kv_rank, MLA_HEADS * MLA_NOPE).astype(BF16),
        "wuv": wukv[..., MLA_NOPE:].reshape(depth, kv_rank, MLA_HEADS * MLA_DV).astype(BF16),
        "ret_wo": ret_w_o.astype(BF16), "glu_a": s5_glu_a.astype(BF16),
        "glu_b": s5_glu_b.astype(BF16), "mla_wo": mla_w_o.astype(BF16),
        "w_out": w_out.astype(BF16),
    }
    p.update(_s5_params(s5_a_re, s5_a_im, s5_log_dt, s5_b_re, s5_b_im, s5_c_re, s5_c_im, s5_d))
    return p


def kernel(x, positions, norm_gains, ffn_w_gate, ffn_w_up, ffn_w_down, w_in, ret_w_o,
           s5_a_re, s5_a_im, s5_log_dt, s5_b_re, s5_b_im, s5_c_re, s5_c_im, s5_d,
           s5_glu_a, s5_glu_b, mla_q_norm, mla_kv_norm, mla_w_uq, mla_w_ukv, mla_w_o, w_out):
    B, L, D = x.shape
    depth = norm_gains.shape[0]
    assert L % ROW_TILE == 0 and L % RET_ROWS == 0 and L % ATTN_TQ == 0 and L % S5_TL == 0
    assert B == SUBLANES, "the S5 scan keeps the batch on the sublane axis"

    p = _prepare(norm_gains, ffn_w_gate, ffn_w_up, ffn_w_down, w_in, ret_w_o,
                 s5_a_re, s5_a_im, s5_log_dt, s5_b_re, s5_b_im, s5_c_re, s5_c_im, s5_d,
                 s5_glu_a, s5_glu_b, mla_q_norm, mla_kv_norm, mla_w_uq, mla_w_ukv, mla_w_o, w_out)
    tabs = _rope_tables(positions)
    ret_tabs = _ret_tables()
    x2 = x.reshape(B * L, D)
    for l in range(depth):
        x2 = _ffn(x2, p, l, 0)
        rq, rk, rv, rg, u, mq, mk, mv = _proj(x2, p, l, tabs)
        a_ret = _retention(rq, rk, rv, rg, ret_tabs, B, L)
        y_s5 = _s5(u, p, l, B, L)
        o_mla = _attention(mq, mk, mv, B, L)
        x2 = _merge(x2, a_ret.reshape(B * L, -1), y_s5.reshape(B * L, -1),
                    o_mla.reshape(B * L, -1), p, l)
        x2 = _ffn(x2, p, l, 1)
    return x2.reshape(B, L, D)
```

```python
import math

import jax
import jax.numpy as jnp
from jax import lax
from jax.experimental import pallas as pl
from jax.experimental.pallas import tpu as pltpu

F32 = jnp.float32
BF16 = jnp.bfloat16

ROPE_BASE = 10000.0
NORM_EPS = 1e-6
GN_EPS = 1e-5
FFN_RES = 0.5
RET_HEADS = 4
RET_DK = 128
RET_DV = 128
RET_CHUNK = 128
MLA_HEADS = 4
MLA_NOPE = 128
MLA_ROPE = 64
MLA_DV = 128
MLA_HEAD_PAD = 256
MLA_LOG2_SCALE = (MLA_NOPE + MLA_ROPE) ** -0.5 * math.log2(math.e)

LANES = 128
SUBLANES = 8
VMEM_LIMIT = 56 * 1024 * 1024

ROW_TILE = 512
FFN_PIECE = 256
FFN_PIECES = 4
MERGE_PIECE = 256
MERGE_PIECES = 2
FFN_CHUNK = 1024
RET_ROWS = 512
S5_TL = 128
ATTN_TQ = 512
ATTN_HEADS_PER_STEP = 4
ATTN_LOOKAHEAD = 2


def _cparams(sem):
    return pltpu.CompilerParams(dimension_semantics=sem, vmem_limit_bytes=VMEM_LIMIT)


def _const_spec(shape):
    nd = len(shape)
    return pl.BlockSpec(shape, lambda *_: (0,) * nd, pipeline_mode=pl.Buffered(1))


def _slab_spec(arr, lead, block=None, col=0):
    tail = tuple(arr.shape[len(lead):]) if block is None else tuple(block)
    idx = tuple(lead) + (0,) * (len(tail) - 1) + (col,)
    return pl.BlockSpec((None,) * len(lead) + tail, lambda *_: idx,
                        pipeline_mode=pl.Buffered(1))


def _rms(x, g):
    y = x * lax.rsqrt(jnp.mean(x * x, axis=-1, keepdims=True) + NORM_EPS)
    return y * g


def _dot(a, b):
    return jnp.dot(a, b, preferred_element_type=F32)


def _rope_kernel(pos_ref, f_ref, a_ref, na_ref, b_ref, p_ref, np_ref, q_ref,
                 cr_ref, sr_ref, cm_ref, s1_ref, s2_ref):
    ang = pos_ref[...] * f_ref[...]
    c = jnp.cos(ang)
    s = jnp.sin(ang)
    hr = RET_DK // 2
    c_hi = pltpu.roll(c, hr, 1)
    s_hi = pltpu.roll(s, hr, 1)
    c_m2 = pltpu.roll(c, hr + MLA_ROPE // 2, 1)
    s_m2 = pltpu.roll(s, hr + MLA_ROPE // 2, 1)
    cr_ref[...] = c * a_ref[...] + c_hi * b_ref[...]
    sr_ref[...] = s * na_ref[...] + s_hi * b_ref[...]
    cm_ref[...] = c_hi * p_ref[...] + c_m2 * q_ref[...]
    s1_ref[...] = s_hi * np_ref[...]
    s2_ref[...] = s_m2 * q_ref[...]


def _rope_tables(positions):
    B, L = positions.shape
    T = B * L
    pos = positions.astype(F32).reshape(T, 1)
    inv_r = ROPE_BASE ** (-jnp.arange(0, RET_DK, 2, dtype=F32) / RET_DK)
    inv_m = ROPE_BASE ** (-jnp.arange(0, MLA_ROPE, 2, dtype=F32) / MLA_ROPE)
    hr = RET_DK // 2
    hm = MLA_ROPE // 2
    lane = jnp.arange(LANES)
    freq = jnp.concatenate([inv_r, inv_m, jnp.zeros((LANES - hr - hm,), F32)])
    a = (lane < hr).astype(F32)
    b = (lane >= hr).astype(F32)
    p = (lane < hm).astype(F32)
    q = ((lane >= hm) & (lane < 2 * hm)).astype(F32)
    rows = [r.reshape(1, LANES) for r in (freq, a, -a, b, p, -p, q)]
    tm = 2048
    row_spec = pl.BlockSpec((1, LANES), lambda i: (0, 0))
    tab_spec = pl.BlockSpec((tm, LANES), lambda i: (i, 0))
    tab_shape = jax.ShapeDtypeStruct((T, LANES), F32)
    return pl.pallas_call(
        _rope_kernel,
        grid=(T // tm,),
        in_specs=[pl.BlockSpec((tm, 1), lambda i: (i, 0))] + [row_spec] * len(rows),
        out_specs=[tab_spec] * 5,
        out_shape=[tab_shape] * 5,
        compiler_params=_cparams(("parallel",)),
        name="rope_tables",
    )(pos, *rows)


def _ffn_kernel(x_ref, xn_ref, gpre_ref, gpost_ref, wg_ref, wu_ref, wd_ref, o_ref, ha_ref):
    pr = ha_ref.shape[0]
    n_pieces = x_ref.shape[0] // pr
    gpre = gpre_ref[...]
    gpost = gpost_ref[...]

    @pl.when(pl.program_id(0) == 0)
    def _():
        ha_ref[...] = _rms(x_ref[:pr, :], gpre).astype(BF16)

    def swiglu(h):
        d_ff = wg_ref.shape[1]
        chunks = [(c, min(c + FFN_CHUNK, d_ff)) for c in range(0, d_ff, FFN_CHUNK)]
        y = None
        pending = None
        for c0, c1 in chunks + [(None, None)]:
            if c0 is not None:
                g = _dot(h, wg_ref[:, c0:c1])
                u = _dot(h, wu_ref[:, c0:c1])
            if pending is not None:
                a, p0, p1 = pending
                part = _dot(a, wd_ref[p0:p1, :])
                y = part if y is None else y + part
            if c0 is not None:
                pending = ((g * jax.nn.sigmoid(g) * u).astype(BF16), c0, c1)
        return y

    def finish(piece, y):
        rows = slice(piece * pr, (piece + 1) * pr)
        o_ref[rows, :] = x_ref[rows, :] + FFN_RES * _rms(y, gpost)

    y_prev = swiglu(ha_ref[...])
    for piece in range(1, n_pieces):
        y = swiglu(_rms(x_ref[piece * pr:(piece + 1) * pr, :], gpre).astype(BF16))
        finish(piece - 1, y_prev)
        y_prev = y
    ha_ref[...] = _rms(xn_ref[...], gpre).astype(BF16)
    finish(n_pieces - 1, y_prev)


def _ffn(x2, p, l, k):
    T, D = x2.shape
    half = FFN_PIECE
    tile = FFN_PIECE * FFN_PIECES
    n = T // tile
    row = pl.BlockSpec((tile, D), lambda i: (i, 0))
    nxt = pl.BlockSpec((half, D), lambda i: (jnp.minimum(i + 1, n - 1) * FFN_PIECES, 0))
    n_sub = p["gains"].shape[0] // p["ffn_wg"].shape[0]
    gi = l * n_sub + 4 * k
    return pl.pallas_call(
        _ffn_kernel,
        grid=(n,),
        in_specs=[row, nxt, _slab_spec(p["gains"], (gi,)), _slab_spec(p["gains"], (gi + 1,)),
                  _slab_spec(p["ffn_wg"], (l, k)), _slab_spec(p["ffn_wu"], (l, k)),
                  _slab_spec(p["ffn_wd"], (l, k))],
        out_specs=row,
        out_shape=jax.ShapeDtypeStruct((T, D), F32),
        scratch_shapes=[pltpu.VMEM((half, D), BF16)],
        compiler_params=_cparams(("arbitrary",)),
        name="ffn",
    )(x2, x2, p["gains"], p["gains"], p["ffn_wg"], p["ffn_wu"], p["ffn_wd"])


def _rope_ret(v, cos, sin):
    return v * cos + pltpu.roll(v, RET_DK // 2, 1) * sin


def _rope_mla(v, cos, sin1, sin2):
    hm = MLA_ROPE // 2
    return v * cos + pltpu.roll(v, LANES - hm, 1) * sin1 + pltpu.roll(v, hm, 1) * sin2


def _proj_kernel(x_ref, xn_ref, g_ref, wr_ref, wu_ref, wm_ref, qn_ref, kvn_ref,
                 wuq_ref, wuk_ref, wuv_ref, cr_ref, sr_ref, cm_ref, s1_ref, s2_ref,
                 rq_ref, rk_ref, rv_ref, rg_ref, u_ref, mq_ref, mk_ref, mv_ref, ha_ref):
    half = x_ref.shape[0] // 2
    gain = g_ref[...]
    qk_w = RET_HEADS * RET_DK
    v_w = RET_HEADS * RET_DV
    q_rank = qn_ref.shape[1]
    kv_rank = kvn_ref.shape[1]

    @pl.when(pl.program_id(0) == 0)
    def _():
        ha_ref[...] = _rms(x_ref[:half, :], gain).astype(BF16)

    def project(h):
        m = _dot(h, wm_ref[...])
        r = _dot(h, wr_ref[...])
        u = _dot(h, wu_ref[...])
        return m, r, u

    def mla_up(m):
        cq = _rms(m[:, :q_rank], qn_ref[...]).astype(BF16)
        ckv = _rms(m[:, q_rank:q_rank + kv_rank], kvn_ref[...]).astype(BF16)
        q = _dot(cq, wuq_ref[...])
        kn = _dot(ckv, wuk_ref[...])
        v = _dot(ckv, wuv_ref[...])
        return q, kn, v

    def store(rows, m, r, u, q, kn, v):
        cr = cr_ref[rows, :]
        sr = sr_ref[rows, :]
        for hd in range(RET_HEADS):
            lo = hd * RET_DK
            hcols = slice(lo, lo + RET_DK)
            qr = _rope_ret(r[:, lo:lo + RET_DK], cr, sr)
            kr = _rope_ret(r[:, qk_w + lo:qk_w + lo + RET_DK], cr, sr) * (RET_DK ** -0.5)
            rq_ref[rows, hcols] = qr.astype(BF16)
            rk_ref[rows, hcols] = kr.astype(BF16)
        rv_ref[rows, :] = r[:, 2 * qk_w:2 * qk_w + v_w].astype(BF16)
        rg_ref[rows, :] = r[:, 2 * qk_w + v_w:].astype(BF16)
        u_ref[rows, :] = u
        cm = cm_ref[rows, :]
        s1 = s1_ref[rows, :]
        s2 = s2_ref[rows, :]
        kpe = _rope_mla(m[:, q_rank + kv_rank:], cm, s1, s2).astype(BF16)
        mv_ref[rows, :] = v.astype(BF16)
        for hd in range(MLA_HEADS):
            lo = hd * MLA_HEAD_PAD
            mq_ref[rows, lo:lo + MLA_NOPE] = (q[:, lo:lo + MLA_NOPE] * MLA_LOG2_SCALE).astype(BF16)
            mq_ref[rows, lo + MLA_NOPE:lo + MLA_HEAD_PAD] = (_rope_mla(
                q[:, lo + MLA_NOPE:lo + MLA_HEAD_PAD], cm, s1, s2) * MLA_LOG2_SCALE).astype(BF16)
            mk_ref[rows, lo:lo + MLA_NOPE] = kn[:, hd * MLA_NOPE:(hd + 1) * MLA_NOPE].astype(BF16)
            mk_ref[rows, lo + MLA_NOPE:lo + MLA_HEAD_PAD] = kpe

    pa = project(ha_ref[...])
    pb = project(_rms(x_ref[half:, :], gain).astype(BF16))
    ua = mla_up(pa[0])
    ub = mla_up(pb[0])
    store(slice(0, half), *pa, *ua)
    ha_ref[...] = _rms(xn_ref[...], gain).astype(BF16)
    store(slice(half, 2 * half), *pb, *ub)


def _proj(x2, p, l, tabs):
    T, D = x2.shape
    tm = ROW_TILE
    n = T // tm
    row = lambda n: pl.BlockSpec((tm, n), lambda i: (i, 0))
    nxt = pl.BlockSpec((tm // 2, D), lambda i: (jnp.minimum(2 * i + 2, 2 * n - 2), 0))
    n_sub = p["gains"].shape[0] // p["w_in"].shape[0]
    ret_w = 2 * RET_HEADS * RET_DK + 2 * RET_HEADS * RET_DV
    s5_w = p["s5_d"].shape[-1]
    assert ret_w % s5_w == 0
    consts = [(p["gains"], _slab_spec(p["gains"], (l * n_sub + 2,))),
              (p["w_in"], _slab_spec(p["w_in"], (l,), block=(D, ret_w), col=0)),
              (p["w_in"], _slab_spec(p["w_in"], (l,), block=(D, s5_w), col=ret_w // s5_w)),
              (p["wm"], _slab_spec(p["wm"], (l,))),
              (p["qn"], _slab_spec(p["qn"], (l,))), (p["kvn"], _slab_spec(p["kvn"], (l,))),
              (p["wuq"], _slab_spec(p["wuq"], (l,))), (p["wuk"], _slab_spec(p["wuk"], (l,))),
              (p["wuv"], _slab_spec(p["wuv"], (l,)))]
    out_w = [(RET_HEADS * RET_DK, BF16)] * 2 + [
             (RET_HEADS * RET_DV, BF16), (RET_HEADS * RET_DV, BF16), (s5_w, F32),
             (MLA_HEADS * MLA_HEAD_PAD, BF16), (MLA_HEADS * MLA_HEAD_PAD, BF16),
             (MLA_HEADS * MLA_DV, BF16)]
    return pl.pallas_call(
        _proj_kernel,
        grid=(n,),
        in_specs=[row(D), nxt] + [s for _, s in consts] + [row(LANES)] * 5,
        out_specs=[row(w) for w, _ in out_w],
        out_shape=[jax.ShapeDtypeStruct((T, w), dt) for w, dt in out_w],
        scratch_shapes=[pltpu.VMEM((tm // 2, D), BF16)],
        compiler_params=_cparams(("arbitrary",)),
        name="mixer_proj",
    )(x2, x2, *[a for a, _ in consts], *tabs)


def _ret_kernel(q_ref, k_ref, v_ref, g_ref, intra_ref, qd_ref, kd_ref, cd_ref, o_ref, st_ref):
    @pl.when(pl.program_id(1) == 0)
    def _():
        st_ref[...] = jnp.zeros_like(st_ref)

    C = RET_CHUNK
    states = [st_ref[hd] for hd in range(RET_HEADS)]
    for c in range(RET_ROWS // C):
        rows = slice(c * C, (c + 1) * C)
        for hd in range(RET_HEADS):
            cols = slice(hd * RET_DK, (hd + 1) * RET_DK)
            vcols = slice(hd * RET_DV, (hd + 1) * RET_DV)
            q = q_ref[0, rows, cols]
            kt = k_ref[0, rows, cols].T
            v = v_ref[0, rows, vcols]
            scores = _dot(q, kt) * intra_ref[hd]
            inner = _dot(scores.astype(BF16), v)
            cross = _dot((q.astype(F32) * qd_ref[hd]).astype(BF16), states[hd].astype(BF16))
            kv = _dot((kt.astype(F32) * kd_ref[hd]).astype(BF16), v)
            states[hd] = cd_ref[hd] * states[hd] + kv
            o = inner + cross
            mu = jnp.mean(o, axis=-1, keepdims=True)
            var = jnp.mean(jnp.square(o - mu), axis=-1, keepdims=True)
            on = (o - mu) * lax.rsqrt(var + GN_EPS)
            g = g_ref[0, rows, vcols].astype(F32)
            o_ref[0, rows, vcols] = (g * jax.nn.sigmoid(g) * on).astype(BF16)
    for hd in range(RET_HEADS):
        st_ref[hd] = states[hd]


def _ret_tables():
    C = RET_CHUNK
    log_gamma = jnp.log1p(-jnp.exp2(-5.0 - jnp.arange(RET_HEADS, dtype=F32)))
    pos = jnp.arange(C, dtype=F32)
    rel = pos[:, None] - pos[None, :]
    intra = jnp.where(rel[None] >= 0.0,
                      jnp.exp(jnp.maximum(rel, 0.0)[None] * log_gamma[:, None, None]), 0.0)
    k_decay = jnp.exp((C - 1.0 - pos)[:, None] * log_gamma[None, :])
    q_decay = jnp.exp((pos + 1.0)[:, None] * log_gamma[None, :])
    chunk_decay = jnp.exp(C * log_gamma)
    qd = jnp.broadcast_to(q_decay.T[:, :, None], (RET_HEADS, C, RET_DK))
    kd = jnp.broadcast_to(k_decay.T[:, None, :], (RET_HEADS, RET_DK, C))
    cd = jnp.broadcast_to(chunk_decay[:, None, None], (RET_HEADS, 1, RET_DV))
    return intra, qd, kd, cd


def _retention(rq, rk, rv, rg, tables, B, L):
    W = rq.shape[-1]
    shp = (B, L, W)
    blk = pl.BlockSpec((1, RET_ROWS, W), lambda b, i: (b, i, 0))
    seq = [a.reshape(shp) for a in (rq, rk, rv, rg)]
    return pl.pallas_call(
        _ret_kernel,
        grid=(B, L // RET_ROWS),
        in_specs=[blk] * len(seq) + [_const_spec(t.shape) for t in tables],
        out_specs=blk,
        out_shape=jax.ShapeDtypeStruct(shp, BF16),
        scratch_shapes=[pltpu.VMEM((RET_HEADS, RET_DK, RET_DV), F32)],
        compiler_params=_cparams(("parallel", "arbitrary")),
        name="retention",
    )(*seq, *tables)


def _s5_zoh_kernel(are_ref, aim_ref, ldt_ref, bre_ref, bim_ref,
                   abr_ref, abi_ref, bbr_ref, bbi_ref):
    a_re = are_ref[...]
    a_im = aim_ref[...]
    dt = jnp.exp(ldt_ref[...])
    mag = jnp.exp(a_re * dt)
    abar_re = mag * jnp.cos(a_im * dt)
    abar_im = mag * jnp.sin(a_im * dt)
    den = a_re * a_re + a_im * a_im
    nr = abar_re - 1.0
    f_re = (nr * a_re + abar_im * a_im) / den
    f_im = (abar_im * a_re - nr * a_im) / den
    b_re = bre_ref[...]
    b_im = bim_ref[...]
    abr_ref[...] = abar_re
    abi_ref[...] = abar_im
    bbr_ref[...] = f_re * b_re - f_im * b_im
    bbi_ref[...] = f_re * b_im + f_im * b_re


def _s5_params(a_re, a_im, log_dt, b_re, b_im, c_re, c_im, d):
    Dp, G, P = a_re.shape
    H = b_re.shape[-1]
    R = Dp * G * H
    rep = lambda a: jnp.broadcast_to(a[:, :, None, :], (Dp, G, H, a.shape[-1])).reshape(R, -1)
    bt = lambda b: jnp.swapaxes(b, 2, 3).reshape(R, P)
    args = [rep(a_re), rep(a_im), rep(log_dt[:, :, None]), bt(b_re), bt(b_im)]
    out = jax.ShapeDtypeStruct((R, P), F32)
    abr, abi, bbr, bbi = pl.pallas_call(
        _s5_zoh_kernel, out_shape=[out] * 4, name="s5_discretise",
        compiler_params=pltpu.CompilerParams(vmem_limit_bytes=VMEM_LIMIT),
    )(*args)
    abar_re = abr.reshape(Dp, G, H, P)[:, :, 0, :].reshape(Dp, 1, G * P)
    abar_im = abi.reshape(Dp, G, H, P)[:, :, 0, :].reshape(Dp, 1, G * P)
    gpb = LANES // H
    nblk = G // gpb
    eye = jnp.eye(gpb, dtype=F32)

    def pack_in(bb):
        bb = bb.reshape(Dp, nblk, gpb, H, P)
        m = jnp.einsum("dkghp,gf->dkghfp", bb, eye)
        return m.reshape(Dp, nblk, gpb * H, gpb * P).astype(BF16)

    def pack_out(cc):
        cc = cc.reshape(Dp, nblk, gpb, H, P)
        m = jnp.einsum("dkghp,gf->dkgpfh", cc, eye)
        return m.reshape(Dp, nblk, gpb * P, gpb * H).astype(BF16)

    return {"s5_are": abar_re, "s5_aim": abar_im,
            "s5_wbr": pack_in(bbr), "s5_wbi": pack_in(bbi),
            "s5_wcr": pack_out(c_re), "s5_wci": pack_out(c_im),
            "s5_d": d.reshape(Dp, 1, G * H)}


def _s5_kernel(u_ref, are_ref, aim_ref, wbr_ref, wbi_ref, wcr_ref, wci_ref, d_ref, y_ref,
               sre_ref, sim_ref, *block_refs):
    B, tl, W = u_ref.shape
    nblk = wbr_ref.shape[0]
    in_w = wbr_ref.shape[1]
    st_w = wbr_ref.shape[2]
    per_block = len(block_refs) // nblk
    blocks = [block_refs[kb * per_block:(kb + 1) * per_block] for kb in range(nblk)]

    @pl.when(pl.program_id(0) == 0)
    def _():
        sre_ref[...] = jnp.zeros_like(sre_ref)
        sim_ref[...] = jnp.zeros_like(sim_ref)

    assert in_w == LANES and nblk * LANES == W

    def gather(kb):
        u2_ref, ut_ref = blocks[kb][0], blocks[kb][1]
        for b in range(B):
            u2_ref[b * tl:(b + 1) * tl, :] = u_ref[b, :, kb * LANES:(kb + 1) * LANES]
        for t in range(tl):
            ut_ref[t * B:(t + 1) * B, :] = u2_ref[pl.ds(t, B, stride=tl), :]

    def emit(kb):
        yt_ref = blocks[kb][4]
        for b in range(B):
            y_ref[b, :, kb * LANES:(kb + 1) * LANES] = yt_ref[pl.ds(b, tl, stride=B), :]

    def drive(kb):
        _, ut_ref, bre_ref, bim_ref, _ = blocks[kb]
        lhs = ut_ref[...].astype(BF16)
        bre_ref[...] = _dot(lhs, wbr_ref[kb])
        bim_ref[...] = _dot(lhs, wbi_ref[kb])

    def scan(kb):
        _, _, bre_ref, bim_ref, _ = blocks[kb]
        sl = slice(kb * st_w, (kb + 1) * st_w)
        ar = jnp.broadcast_to(are_ref[:, sl], (B, st_w))
        ai = jnp.broadcast_to(aim_ref[:, sl], (B, st_w))
        sr = sre_ref[:, sl]
        si = sim_ref[:, sl]
        for t in range(tl):
            rows = slice(t * B, (t + 1) * B)
            sr, si = (ar * sr - ai * si + bre_ref[rows, :],
                      ar * si + ai * sr + bim_ref[rows, :])
            bre_ref[rows, :] = sr
            bim_ref[rows, :] = si
        sre_ref[:, sl] = sr
        sim_ref[:, sl] = si

    def project(kb):
        _, ut_ref, bre_ref, bim_ref, yt_ref = blocks[kb]
        y = (_dot(bre_ref[...].astype(BF16), wcr_ref[kb])
             - _dot(bim_ref[...].astype(BF16), wci_ref[kb]))
        yt_ref[...] = jax.nn.gelu(y + d_ref[:, kb * in_w:(kb + 1) * in_w] * ut_ref[...])

    gather(0)
    drive(0)
    for kb in range(nblk):
        if kb + 1 < nblk:
            gather(kb + 1)
            drive(kb + 1)
        scan(kb)
        if kb >= 1:
            project(kb - 1)
            emit(kb - 1)
    project(nblk - 1)
    emit(nblk - 1)


def _s5(u, p, l, B, L):
    W = u.shape[-1]
    tl = S5_TL
    n_state = p["s5_are"].shape[-1]
    nblk, in_w, st_w = p["s5_wbr"].shape[1:]
    blk = pl.BlockSpec((B, tl, W), lambda i: (0, i, 0))
    names = ["s5_are", "s5_aim", "s5_wbr", "s5_wbi", "s5_wcr", "s5_wci", "s5_d"]
    rows = B * tl
    per_block = [pltpu.VMEM((rows, in_w), F32), pltpu.VMEM((rows, in_w), F32),
                 pltpu.VMEM((rows, st_w), F32), pltpu.VMEM((rows, st_w), F32),
                 pltpu.VMEM((rows, in_w), F32)]
    return pl.pallas_call(
        _s5_kernel,
        grid=(L // tl,),
        in_specs=[blk] + [_slab_spec(p[n], (l,)) for n in names],
        out_specs=blk,
        out_shape=jax.ShapeDtypeStruct((B, L, W), F32),
        scratch_shapes=[pltpu.VMEM((B, n_state), F32), pltpu.VMEM((B, n_state), F32)]
                       + per_block * nblk,
        compiler_params=_cparams(("arbitrary",)),
        name="s5_scan",
    )(u.reshape(B, L, W), *[p[n] for n in names])


def _attn_kernel(q_ref, k_ref, v_ref, o_ref, s_ref, m_ref, mprev_ref, l_ref, acc_ref, vt_ref,
                 qt_ref):
    tq = q_ref.shape[1]
    nh = m_ref.shape[0]
    qi = pl.program_id(2)
    m_ref[...] = jnp.full(m_ref.shape, -1e30, F32)
    l_ref[...] = jnp.zeros_like(l_ref)
    acc_ref[...] = jnp.zeros_like(acc_ref)
    for h in range(nh):
        qt_ref[h] = q_ref[0, :, h * MLA_HEAD_PAD:(h + 1) * MLA_HEAD_PAD].T

    def scores(j, h, masked):
        r0 = pl.multiple_of(j * tq, tq)
        qk = slice(h * MLA_HEAD_PAD, (h + 1) * MLA_HEAD_PAD)
        st = _dot(k_ref[0, pl.ds(r0, tq), qk], qt_ref[h])
        if masked:
            kidx = lax.broadcasted_iota(jnp.int32, st.shape, 0)
            qidx = lax.broadcasted_iota(jnp.int32, st.shape, 1)
            st = jnp.where(kidx <= qidx, st, -jnp.inf)
        s_ref[h] = st
        m_old = m_ref[h]
        mprev_ref[h] = m_old
        m_ref[h] = jnp.maximum(m_old, jnp.max(st, axis=0, keepdims=True))

    @pl.when(qi == 0)
    def _():
        for h in range(nh):
            for blk in range(v_ref.shape[1] // tq):
                rows = slice(blk * tq, (blk + 1) * tq)
                vt_ref[h, :, rows] = v_ref[0, rows, h * MLA_DV:(h + 1) * MLA_DV].T

    def accumulate(j, h):
        r0 = pl.multiple_of(j * tq, tq)
        m_new = m_ref[h]
        alpha = jnp.exp2(mprev_ref[h] - m_new)
        pt = jnp.exp2(s_ref[h] - m_new)
        l_ref[h] = alpha * l_ref[h] + jnp.sum(pt, axis=0, keepdims=True)
        acc_ref[h] = alpha * acc_ref[h] + _dot(vt_ref[h, :, pl.ds(r0, tq)],
                                               pt.astype(BF16))

    la = ATTN_LOOKAHEAD
    assert 0 < la < nh
    for h in range(nh):
        scores(qi, h, True)
        if h >= la:
            accumulate(qi, h - la)

    def one_block(j):
        prev = jnp.where(j == 0, qi, j - 1)
        for h in range(nh):
            scores(j, h, False)
            if h >= la:
                accumulate(j, h - la)
            else:
                accumulate(prev, h - la + nh)

    @pl.when(qi % 2 == 1)
    def _():
        one_block(0)

    @pl.when(qi % 4 >= 2)
    def _():
        j = qi % 2
        one_block(j)
        one_block(j + 1)

    def body(t, carry):
        j = qi % 4 + 4 * t
        for d in range(4):
            one_block(j + d)
        return carry
    lax.fori_loop(0, qi // 4, body, 0)
    last = jnp.maximum(qi - 1, 0)
    for h in range(nh - la, nh):
        accumulate(last, h)
    for h in range(nh):
        o_ref[0, :, h * MLA_DV:(h + 1) * MLA_DV] = (acc_ref[h] / l_ref[h]).T.astype(BF16)


def _attention(mq, mk, mv, B, L):
    tq = ATTN_TQ
    nh = ATTN_HEADS_PER_STEP
    qw = nh * MLA_HEAD_PAD
    vw = nh * MLA_DV
    return pl.pallas_call(
        _attn_kernel,
        grid=(B, MLA_HEADS // nh, L // tq),
        in_specs=[pl.BlockSpec((1, tq, qw), lambda b, h, i: (b, i, h)),
                  pl.BlockSpec((1, L, qw), lambda b, h, i: (b, 0, h)),
                  pl.BlockSpec((1, L, vw), lambda b, h, i: (b, 0, h))],
        out_specs=pl.BlockSpec((1, tq, vw), lambda b, h, i: (b, i, h)),
        out_shape=jax.ShapeDtypeStruct((B, L, MLA_HEADS * MLA_DV), BF16),
        scratch_shapes=[pltpu.VMEM((nh, tq, tq), F32), pltpu.VMEM((nh, 1, tq), F32),
                        pltpu.VMEM((nh, 1, tq), F32), pltpu.VMEM((nh, 1, tq), F32),
                        pltpu.VMEM((nh, MLA_DV, tq), F32), pltpu.VMEM((nh, MLA_DV, L), BF16),
                        pltpu.VMEM((nh, MLA_HEAD_PAD, tq), BF16)],
        compiler_params=_cparams(("parallel", "parallel", "arbitrary")),
        name="mla_attention",
    )(mq.reshape(B, L, -1), mk.reshape(B, L, -1), mv.reshape(B, L, -1))


def _merge_kernel(x_ref, xn_ref, gin_ref, gout_ref, a_ref, y_ref, o_ref_in, wg_ref, wro_ref,
                  wga_ref, wgb_ref, wmo_ref, wout_ref, out_ref, ha_ref):
    pr = ha_ref.shape[0]
    n_pieces = x_ref.shape[0] // pr
    D = x_ref.shape[1]
    gin = gin_ref[...]
    gout = gout_ref[...]

    @pl.when(pl.program_id(0) == 0)
    def _():
        ha_ref[...] = _rms(x_ref[:pr, :], gin).astype(BF16)

    def branches(h, piece):
        rows = slice(piece * pr, (piece + 1) * pr)
        ys = y_ref[rows, :].astype(BF16)
        return (_dot(h, wg_ref[:, 0:D]), _dot(a_ref[rows, :], wro_ref[...]),
                _dot(h, wg_ref[:, D:2 * D]), _dot(ys, wga_ref[...]), _dot(ys, wgb_ref[...]),
                _dot(h, wg_ref[:, 2 * D:3 * D]), _dot(o_ref_in[rows, :], wmo_ref[...]))

    def combine(g0, y_ret, g1, ya, yb, g2, y_mla):
        merged = jax.nn.sigmoid(g0) * y_ret
        merged = merged + jax.nn.sigmoid(g1) * (ya * jax.nn.sigmoid(yb))
        merged = merged + jax.nn.sigmoid(g2) * y_mla
        return merged.astype(BF16)

    def finish(piece, o):
        rows = slice(piece * pr, (piece + 1) * pr)
        out_ref[rows, :] = x_ref[rows, :] + _rms(o, gout)

    b_prev = branches(ha_ref[...], 0)
    o_prev = None
    for piece in range(1, n_pieces):
        b = branches(_rms(x_ref[piece * pr:(piece + 1) * pr, :], gin).astype(BF16), piece)
        o = _dot(combine(*b_prev), wout_ref[...])
        if o_prev is not None:
            finish(piece - 2, o_prev)
        b_prev, o_prev = b, o
    o = _dot(combine(*b_prev), wout_ref[...])
    if o_prev is not None:
        finish(n_pieces - 2, o_prev)
    ha_ref[...] = _rms(xn_ref[...], gin).astype(BF16)
    finish(n_pieces - 1, o)


def _merge(x2, a_ret, y_s5, o_mla, p, l):
    T, D = x2.shape
    tm = MERGE_PIECE * MERGE_PIECES
    n = T // tm
    row = lambda w: pl.BlockSpec((tm, w), lambda i: (i, 0))
    nxt = pl.BlockSpec((MERGE_PIECE, D),
                       lambda i: (jnp.minimum(i + 1, n - 1) * MERGE_PIECES, 0))
    n_sub = p["gains"].shape[0] // p["wgates"].shape[0]
    names = ["wgates", "ret_wo", "glu_a", "glu_b", "mla_wo", "w_out"]
    return pl.pallas_call(
        _merge_kernel,
        grid=(n,),
        in_specs=[row(D), nxt, _slab_spec(p["gains"], (l * n_sub + 2,)),
                  _slab_spec(p["gains"], (l * n_sub + 3,)),
                  row(a_ret.shape[1]), row(y_s5.shape[1]), row(o_mla.shape[1])]
                 + [_slab_spec(p[n], (l,)) for n in names],
        out_specs=row(D),
        out_shape=jax.ShapeDtypeStruct((T, D), F32),
        scratch_shapes=[pltpu.VMEM((MERGE_PIECE, D), BF16)],
        compiler_params=_cparams(("arbitrary",)),
        name="gated_merge",
    )(x2, x2, p["gains"], p["gains"], a_ret, y_s5, o_mla, *[p[n] for n in names])


def _prepare(norm_gains, ffn_w_gate, ffn_w_up, ffn_w_down, w_in, ret_w_o,
             s5_a_re, s5_a_im, s5_log_dt, s5_b_re, s5_b_im, s5_c_re, s5_c_im, s5_d,
             s5_glu_a, s5_glu_b, mla_q_norm, mla_kv_norm, mla_w_uq, mla_w_ukv, mla_w_o, w_out):
    depth, n_sub, D = norm_gains.shape
    q_rank = mla_q_norm.shape[-1]
    kv_rank = mla_kv_norm.shape[-1]
    s5_w = s5_d.shape[-1]
    ret_w = 2 * RET_HEADS * RET_DK + 2 * RET_HEADS * RET_DV
    m_lo = ret_w + s5_w
    m_w = q_rank + kv_rank + MLA_ROPE
    assert w_in.shape[-1] == m_lo + m_w + 3 * D
    hd_w = MLA_NOPE + MLA_ROPE
    wuq = jnp.pad(mla_w_uq.reshape(depth, q_rank, MLA_HEADS, hd_w),
                  ((0, 0), (0, 0), (0, 0), (0, MLA_HEAD_PAD - hd_w)))
    wukv = mla_w_ukv.reshape(depth, kv_rank, MLA_HEADS, MLA_NOPE + MLA_DV)
    p = {
        "gains": norm_gains.reshape(depth * n_sub, 1, D),
        "ffn_wg": ffn_w_gate.astype(BF16), "ffn_wu": ffn_w_up.astype(BF16),
        "ffn_wd": ffn_w_down.astype(BF16),
        "w_in": w_in.astype(BF16),
        "wm": jnp.pad(w_in[:, :, m_lo:m_lo + m_w],
                      ((0, 0), (0, 0), (0, LANES - MLA_ROPE))).astype(BF16),
        "wgates": w_in[:, :, m_lo + m_w:].astype(BF16),
        "qn": mla_q_norm.reshape(depth, 1, q_rank), "kvn": mla_kv_norm.reshape(depth, 1, kv_rank),
        "wuq": wuq.reshape(depth, q_rank, MLA_HEADS * MLA_HEAD_PAD).astype(BF16),
        "wuk": wukv[..., :MLA_NOPE].reshape(depth, kv_rank, MLA_HEADS * MLA_NOPE).astype(BF16),
        "wuv": wukv[..., MLA_NOPE:].reshape(depth, kv_rank, MLA_HEADS * MLA_DV).astype(BF16),
        "ret_wo": ret_w_o.astype(BF16), "glu_a": s5_glu_a.astype(BF16),
        "glu_b": s5_glu_b.astype(BF16), "mla_wo": mla_w_o.astype(BF16),
        "w_out": w_out.astype(BF16),
    }
    p.update(_s5_params(s5_a_re, s5_a_im, s5_log_dt, s5_b_re, s5_b_im, s5_c_re, s5_c_im, s5_d))
    return p


def kernel(x, positions, norm_gains, ffn_w_gate, ffn_w_up, ffn_w_down, w_in, ret_w_o,
           s5_a_re, s5_a_im, s5_log_dt, s5_b_re, s5_b_im, s5_c_re, s5_c_im, s5_d,
           s5_glu_a, s5_glu_b, mla_q_norm, mla_kv_norm, mla_w_uq, mla_w_ukv, mla_w_o, w_out):
    B, L, D = x.shape
    depth = norm_gains.shape[0]
    assert L % RET_ROWS == 0 and L % ATTN_TQ == 0 and L % S5_TL == 0
    for tile in (ROW_TILE, FFN_PIECE * FFN_PIECES, MERGE_PIECE * MERGE_PIECES):
        assert (B * L) % tile == 0
    assert B == SUBLANES, "the S5 scan keeps the batch on the sublane axis"

    p = _prepare(norm_gains, ffn_w_gate, ffn_w_up, ffn_w_down, w_in, ret_w_o,
                 s5_a_re, s5_a_im, s5_log_dt, s5_b_re, s5_b_im, s5_c_re, s5_c_im, s5_d,
                 s5_glu_a, s5_glu_b, mla_q_norm, mla_kv_norm, mla_w_uq, mla_w_ukv, mla_w_o, w_out)
    tabs = _rope_tables(positions)
    ret_tabs = _ret_tables()
    x2 = x.reshape(B * L, D)
    for l in range(depth):
        x2 = _ffn(x2, p, l, 0)
        rq, rk, rv, rg, u, mq, mk, mv = _proj(x2, p, l, tabs)
        a_ret = _retention(rq, rk, rv, rg, ret_tabs, B, L)
        y_s5 = _s5(u, p, l, B, L)
        o_mla = _attention(mq, mk, mv, B, L)
        x2 = _merge(x2, a_ret.reshape(B * L, -1), y_s5.reshape(B * L, -1),
                    o_mla.reshape(B * L, -1), p, l)
        x2 = _ffn(x2, p, l, 1)
    return x2.reshape(B, L, D)
```

```python
import math

import jax
import jax.numpy as jnp
from jax import lax
from jax.experimental import pallas as pl
from jax.experimental.pallas import tpu as pltpu

F32 = jnp.float32
BF16 = jnp.bfloat16

ROPE_BASE = 10000.0
NORM_EPS = 1e-6
GN_EPS = 1e-5
FFN_RES = 0.5
RET_HEADS = 4
RET_DK = 128
RET_DV = 128
RET_CHUNK = 128
MLA_HEADS = 4
MLA_NOPE = 128
MLA_ROPE = 64
MLA_DV = 128
MLA_HEAD_PAD = 256
MLA_LOG2_SCALE = (MLA_NOPE + MLA_ROPE) ** -0.5 * math.log2(math.e)

LANES = 128
SUBLANES = 8
VMEM_LIMIT = 56 * 1024 * 1024

ROW_TILE = 512
FFN_PIECE = 256
FFN_PIECES = 4
MERGE_PIECE = 256
MERGE_PIECES = 2
FFN_CHUNK = 1024
RET_ROWS = 512
S5_TL = 128
ATTN_TQ = 512
ATTN_HEADS_PER_STEP = 4
ATTN_LOOKAHEAD = 2


def _cparams(sem):
    return pltpu.CompilerParams(dimension_semantics=sem, vmem_limit_bytes=VMEM_LIMIT)


def _const_spec(shape):
    nd = len(shape)
    return pl.BlockSpec(shape, lambda *_: (0,) * nd, pipeline_mode=pl.Buffered(1))


def _slab_spec(arr, lead, block=None, col=0):
    tail = tuple(arr.shape[len(lead):]) if block is None else tuple(block)
    idx = tuple(lead) + (0,) * (len(tail) - 1) + (col,)
    return pl.BlockSpec((None,) * len(lead) + tail, lambda *_: idx,
                        pipeline_mode=pl.Buffered(1))


def _rms(x, g):
    y = x * lax.rsqrt(jnp.mean(x * x, axis=-1, keepdims=True) + NORM_EPS)
    return y * g


def _dot(a, b):
    return jnp.dot(a, b, preferred_element_type=F32)


def _rope_kernel(pos_ref, f_ref, a_ref, na_ref, b_ref, p_ref, np_ref, q_ref,
                 cr_ref, sr_ref, cm_ref, s1_ref, s2_ref):
    ang = pos_ref[...] * f_ref[...]
    c = jnp.cos(ang)
    s = jnp.sin(ang)
    hr = RET_DK // 2
    c_hi = pltpu.roll(c, hr, 1)
    s_hi = pltpu.roll(s, hr, 1)
    c_m2 = pltpu.roll(c, hr + MLA_ROPE // 2, 1)
    s_m2 = pltpu.roll(s, hr + MLA_ROPE // 2, 1)
    cr_ref[...] = c * a_ref[...] + c_hi * b_ref[...]
    sr_ref[...] = s * na_ref[...] + s_hi * b_ref[...]
    cm_ref[...] = c_hi * p_ref[...] + c_m2 * q_ref[...]
    s1_ref[...] = s_hi * np_ref[...]
    s2_ref[...] = s_m2 * q_ref[...]


def _rope_tables(positions):
    B, L = positions.shape
    T = B * L
    pos = positions.astype(F32).reshape(T, 1)
    inv_r = ROPE_BASE ** (-jnp.arange(0, RET_DK, 2, dtype=F32) / RET_DK)
    inv_m = ROPE_BASE ** (-jnp.arange(0, MLA_ROPE, 2, dtype=F32) / MLA_ROPE)
    hr = RET_DK // 2
    hm = MLA_ROPE // 2
    lane = jnp.arange(LANES)
    freq = jnp.concatenate([inv_r, inv_m, jnp.zeros((LANES - hr - hm,), F32)])
    a = (lane < hr).astype(F32)
    b = (lane >= hr).astype(F32)
    p = (lane < hm).astype(F32)
    q = ((lane >= hm) & (lane < 2 * hm)).astype(F32)
    rows = [r.reshape(1, LANES) for r in (freq, a, -a, b, p, -p, q)]
    tm = 2048
    row_spec = pl.BlockSpec((1, LANES), lambda i: (0, 0))
    tab_spec = pl.BlockSpec((tm, LANES), lambda i: (i, 0))
    tab_shape = jax.ShapeDtypeStruct((T, LANES), F32)
    return pl.pallas_call(
        _rope_kernel,
        grid=(T // tm,),
        in_specs=[pl.BlockSpec((tm, 1), lambda i: (i, 0))] + [row_spec] * len(rows),
        out_specs=[tab_spec] * 5,
        out_shape=[tab_shape] * 5,
        compiler_params=_cparams(("parallel",)),
        name="rope_tables",
    )(pos, *rows)


def _ffn_kernel(x_ref, xn_ref, gpre_ref, gpost_ref, wg_ref, wu_ref, wd_ref, o_ref, ha_ref):
    pr = ha_ref.shape[0]
    n_pieces = x_ref.shape[0] // pr
    gpre = gpre_ref[...]
    gpost = gpost_ref[...]

    @pl.when(pl.program_id(0) == 0)
    def _():
        ha_ref[...] = _rms(x_ref[:pr, :], gpre).astype(BF16)

    def swiglu(h):
        d_ff = wg_ref.shape[1]
        chunks = [(c, min(c + FFN_CHUNK, d_ff)) for c in range(0, d_ff, FFN_CHUNK)]
        y = None
        pending = None
        for c0, c1 in chunks + [(None, None)]:
            if c0 is not None:
                g = _dot(h, wg_ref[:, c0:c1])
                u = _dot(h, wu_ref[:, c0:c1])
            if pending is not None:
                a, p0, p1 = pending
                part = _dot(a, wd_ref[p0:p1, :])
                y = part if y is None else y + part
            if c0 is not None:
                pending = ((g * jax.nn.sigmoid(g) * u).astype(BF16), c0, c1)
        return y

    def finish(piece, y):
        rows = slice(piece * pr, (piece + 1) * pr)
        o_ref[rows, :] = x_ref[rows, :] + FFN_RES * _rms(y, gpost)

    y_prev = swiglu(ha_ref[...])
    for piece in range(1, n_pieces):
        y = swiglu(_rms(x_ref[piece * pr:(piece + 1) * pr, :], gpre).astype(BF16))
        finish(piece - 1, y_prev)
        y_prev = y
    ha_ref[...] = _rms(xn_ref[...], gpre).astype(BF16)
    finish(n_pieces - 1, y_prev)


def _ffn(x2, p, l, k):
    T, D = x2.shape
    half = FFN_PIECE
    tile = FFN_PIECE * FFN_PIECES
    n = T // tile
    row = pl.BlockSpec((tile, D), lambda i: (i, 0))
    nxt = pl.BlockSpec((half, D), lambda i: (jnp.minimum(i + 1, n - 1) * FFN_PIECES, 0))
    n_sub = p["gains"].shape[0] // p["ffn_wg"].shape[0]
    gi = l * n_sub + 4 * k
    return pl.pallas_call(
        _ffn_kernel,
        grid=(n,),
        in_specs=[row, nxt, _slab_spec(p["gains"], (gi,)), _slab_spec(p["gains"], (gi + 1,)),
                  _slab_spec(p["ffn_wg"], (l, k)), _slab_spec(p["ffn_wu"], (l, k)),
                  _slab_spec(p["ffn_wd"], (l, k))],
        out_specs=row,
        out_shape=jax.ShapeDtypeStruct((T, D), F32),
        scratch_shapes=[pltpu.VMEM((half, D), BF16)],
        compiler_params=_cparams(("arbitrary",)),
        name="ffn",
    )(x2, x2, p["gains"], p["gains"], p["ffn_wg"], p["ffn_wu"], p["ffn_wd"])


def _rope_ret(v, cos, sin):
    return v * cos + pltpu.roll(v, RET_DK // 2, 1) * sin


def _rope_mla(v, cos, sin1, sin2):
    hm = MLA_ROPE // 2
    return v * cos + pltpu.roll(v, LANES - hm, 1) * sin1 + pltpu.roll(v, hm, 1) * sin2


def _proj_kernel(x_ref, xn_ref, g_ref, wr_ref, wu_ref, wm_ref, qn_ref, kvn_ref,
                 wuq_ref, wuk_ref, wuv_ref, cr_ref, sr_ref, cm_ref, s1_ref, s2_ref,
                 rq_ref, rk_ref, rv_ref, rg_ref, u_ref, mq_ref, mk_ref, mv_ref, ha_ref):
    half = x_ref.shape[0] // 2
    gain = g_ref[...]
    qk_w = RET_HEADS * RET_DK
    v_w = RET_HEADS * RET_DV
    q_rank = qn_ref.shape[1]
    kv_rank = kvn_ref.shape[1]

    @pl.when(pl.program_id(0) == 0)
    def _():
        ha_ref[...] = _rms(x_ref[:half, :], gain).astype(BF16)

    def project(h):
        m = _dot(h, wm_ref[...])
        r = _dot(h, wr_ref[...])
        u = _dot(h, wu_ref[...])
        return m, r, u

    def mla_up(m):
        cq = _rms(m[:, :q_rank], qn_ref[...]).astype(BF16)
        ckv = _rms(m[:, q_rank:q_rank + kv_rank], kvn_ref[...]).astype(BF16)
        q = _dot(cq, wuq_ref[...])
        kn = _dot(ckv, wuk_ref[...])
        v = _dot(ckv, wuv_ref[...])
        return q, kn, v

    def store(rows, m, r, u, q, kn, v):
        cr = cr_ref[rows, :]
        sr = sr_ref[rows, :]
        for hd in range(RET_HEADS):
            lo = hd * RET_DK
            hcols = slice(lo, lo + RET_DK)
            qr = _rope_ret(r[:, lo:lo + RET_DK], cr, sr)
            kr = _rope_ret(r[:, qk_w + lo:qk_w + lo + RET_DK], cr, sr) * (RET_DK ** -0.5)
            rq_ref[rows, hcols] = qr.astype(BF16)
            rk_ref[rows, hcols] = kr.astype(BF16)
        rv_ref[rows, :] = r[:, 2 * qk_w:2 * qk_w + v_w].astype(BF16)
        rg_ref[rows, :] = r[:, 2 * qk_w + v_w:].astype(BF16)
        u_ref[rows, :] = u
        cm = cm_ref[rows, :]
        s1 = s1_ref[rows, :]
        s2 = s2_ref[rows, :]
        kpe = _rope_mla(m[:, q_rank + kv_rank:], cm, s1, s2).astype(BF16)
        mv_ref[rows, :] = v.astype(BF16)
        for hd in range(MLA_HEADS):
            lo = hd * MLA_HEAD_PAD
            mq_ref[rows, lo:lo + MLA_NOPE] = (q[:, lo:lo + MLA_NOPE] * MLA_LOG2_SCALE).astype(BF16)
            mq_ref[rows, lo + MLA_NOPE:lo + MLA_HEAD_PAD] = (_rope_mla(
                q[:, lo + MLA_NOPE:lo + MLA_HEAD_PAD], cm, s1, s2) * MLA_LOG2_SCALE).astype(BF16)
            mk_ref[rows, lo:lo + MLA_NOPE] = kn[:, hd * MLA_NOPE:(hd + 1) * MLA_NOPE].astype(BF16)
            mk_ref[rows, lo + MLA_NOPE:lo + MLA_HEAD_PAD] = kpe

    pa = project(ha_ref[...])
    pb = project(_rms(x_ref[half:, :], gain).astype(BF16))
    ua = mla_up(pa[0])
    ub = mla_up(pb[0])
    store(slice(0, half), *pa, *ua)
    ha_ref[...] = _rms(xn_ref[...], gain).astype(BF16)
    store(slice(half, 2 * half), *pb, *ub)


def _proj(x2, p, l, tabs):
    T, D = x2.shape
    tm = ROW_TILE
    n = T // tm
    row = lambda n: pl.BlockSpec((tm, n), lambda i: (i, 0))
    nxt = pl.BlockSpec((tm // 2, D), lambda i: (jnp.minimum(2 * i + 2, 2 * n - 2), 0))
    n_sub = p["gains"].shape[0] // p["w_in"].shape[0]
    ret_w = 2 * RET_HEADS * RET_DK + 2 * RET_HEADS * RET_DV
    s5_w = p["s5_d"].shape[-1]
    assert ret_w % s5_w == 0
    consts = [(p["gains"], _slab_spec(p["gains"], (l * n_sub + 2,))),
              (p["w_in"], _slab_spec(p["w_in"], (l,), block=(D, ret_w), col=0)),
              (p["w_in"], _slab_spec(p["w_in"], (l,), block=(D, s5_w), col=ret_w // s5_w)),
              (p["wm"], _slab_spec(p["wm"], (l,))),
              (p["qn"], _slab_spec(p["qn"], (l,))), (p["kvn"], _slab_spec(p["kvn"], (l,))),
              (p["wuq"], _slab_spec(p["wuq"], (l,))), (p["wuk"], _slab_spec(p["wuk"], (l,))),
              (p["wuv"], _slab_spec(p["wuv"], (l,)))]
    out_w = [(RET_HEADS * RET_DK, BF16)] * 2 + [
             (RET_HEADS * RET_DV, BF16), (RET_HEADS * RET_DV, BF16), (s5_w, F32),
             (MLA_HEADS * MLA_HEAD_PAD, BF16), (MLA_HEADS * MLA_HEAD_PAD, BF16),
             (MLA_HEADS * MLA_DV, BF16)]
    return pl.pallas_call(
        _proj_kernel,
        grid=(n,),
        in_specs=[row(D), nxt] + [s for _, s in consts] + [row(LANES)] * 5,
        out_specs=[row(w) for w, _ in out_w],
        out_shape=[jax.ShapeDtypeStruct((T, w), dt) for w, dt in out_w],
        scratch_shapes=[pltpu.VMEM((tm // 2, D), BF16)],
        compiler_params=_cparams(("arbitrary",)),
        name="mixer_proj",
    )(x2, x2, *[a for a, _ in consts], *tabs)


def _ret_kernel(q_ref, k_ref, v_ref, g_ref, intra_ref, qd_ref, kd_ref, cd_ref, o_ref, st_ref):
    @pl.when(pl.program_id(1) == 0)
    def _():
        st_ref[...] = jnp.zeros_like(st_ref)

    C = RET_CHUNK
    states = [st_ref[hd] for hd in range(RET_HEADS)]
    for c in range(RET_ROWS // C):
        rows = slice(c * C, (c + 1) * C)
        for hd in range(RET_HEADS):
            cols = slice(hd * RET_DK, (hd + 1) * RET_DK)
            vcols = slice(hd * RET_DV, (hd + 1) * RET_DV)
            q = q_ref[0, rows, cols]
            kt = k_ref[0, rows, cols].T
            v = v_ref[0, rows, vcols]
            scores = _dot(q, kt) * intra_ref[hd]
            inner = _dot(scores.astype(BF16), v)
            cross = _dot((q.astype(F32) * qd_ref[hd]).astype(BF16), states[hd].astype(BF16))
            kv = _dot((kt.astype(F32) * kd_ref[hd]).astype(BF16), v)
            states[hd] = cd_ref[hd] * states[hd] + kv
            o = inner + cross
            mu = jnp.mean(o, axis=-1, keepdims=True)
            var = jnp.mean(jnp.square(o - mu), axis=-1, keepdims=True)
            on = (o - mu) * lax.rsqrt(var + GN_EPS)
            g = g_ref[0, rows, vcols].astype(F32)
            o_ref[0, rows, vcols] = (g * jax.nn.sigmoid(g) * on).astype(BF16)
    for hd in range(RET_HEADS):
        st_ref[hd] = states[hd]


def _ret_tables():
    C = RET_CHUNK
    log_gamma = jnp.log1p(-jnp.exp2(-5.0 - jnp.arange(RET_HEADS, dtype=F32)))
    pos = jnp.arange(C, dtype=F32)
    rel = pos[:, None] - pos[None, :]
    intra = jnp.where(rel[None] >= 0.0,
                      jnp.exp(jnp.maximum(rel, 0.0)[None] * log_gamma[:, None, None]), 0.0)
    k_decay = jnp.exp((C - 1.0 - pos)[:, None] * log_gamma[None, :])
    q_decay = jnp.exp((pos + 1.0)[:, None] * log_gamma[None, :])
    chunk_decay = jnp.exp(C * log_gamma)
    qd = jnp.broadcast_to(q_decay.T[:, :, None], (RET_HEADS, C, RET_DK))
    kd = jnp.broadcast_to(k_decay.T[:, None, :], (RET_HEADS, RET_DK, C))
    cd = jnp.broadcast_to(chunk_decay[:, None, None], (RET_HEADS, 1, RET_DV))
    return intra, qd, kd, cd


def _retention(rq, rk, rv, rg, tables, B, L):
    W = rq.shape[-1]
    shp = (B, L, W)
    blk = pl.BlockSpec((1, RET_ROWS, W), lambda b, i: (b, i, 0))
    seq = [a.reshape(shp) for a in (rq, rk, rv, rg)]
    return pl.pallas_call(
        _ret_kernel,
        grid=(B, L // RET_ROWS),
        in_specs=[blk] * len(seq) + [_const_spec(t.shape) for t in tables],
        out_specs=blk,
        out_shape=jax.ShapeDtypeStruct(shp, BF16),
        scratch_shapes=[pltpu.VMEM((RET_HEADS, RET_DK, RET_DV), F32)],
        compiler_params=_cparams(("parallel", "arbitrary")),
        name="retention",
    )(*seq, *tables)


def _s5_zoh_kernel(are_ref, aim_ref, ldt_ref, bre_ref, bim_ref,
                   abr_ref, abi_ref, bbr_ref, bbi_ref):
    a_re = are_ref[...]
    a_im = aim_ref[...]
    dt = jnp.exp(ldt_ref[...])
    mag = jnp.exp(a_re * dt)
    abar_re = mag * jnp.cos(a_im * dt)
    abar_im = mag * jnp.sin(a_im * dt)
    den = a_re * a_re + a_im * a_im
    nr = abar_re - 1.0
    f_re = (nr * a_re + abar_im * a_im) / den
    f_im = (abar_im * a_re - nr * a_im) / den
    b_re = bre_ref[...]
    b_im = bim_ref[...]
    abr_ref[...] = abar_re
    abi_ref[...] = abar_im
    bbr_ref[...] = f_re * b_re - f_im * b_im
    bbi_ref[...] = f_re * b_im + f_im * b_re


def _s5_params(a_re, a_im, log_dt, b_re, b_im, c_re, c_im, d):
    Dp, G, P = a_re.shape
    H = b_re.shape[-1]
    R = Dp * G * H
    rep = lambda a: jnp.broadcast_to(a[:, :, None, :], (Dp, G, H, a.shape[-1])).reshape(R, -1)
    bt = lambda b: jnp.swapaxes(b, 2, 3).reshape(R, P)
    args = [rep(a_re), rep(a_im), rep(log_dt[:, :, None]), bt(b_re), bt(b_im)]
    out = jax.ShapeDtypeStruct((R, P), F32)
    abr, abi, bbr, bbi = pl.pallas_call(
        _s5_zoh_kernel, out_shape=[out] * 4, name="s5_discretise",
        compiler_params=pltpu.CompilerParams(vmem_limit_bytes=VMEM_LIMIT),
    )(*args)
    abar_re = abr.reshape(Dp, G, H, P)[:, :, 0, :].reshape(Dp, 1, G * P)
    abar_im = abi.reshape(Dp, G, H, P)[:, :, 0, :].reshape(Dp, 1, G * P)
    gpb = LANES // H
    nblk = G // gpb
    eye = jnp.eye(gpb, dtype=F32)

    def pack_in(bb):
        bb = bb.reshape(Dp, nblk, gpb, H, P)
        m = jnp.einsum("dkghp,gf->dkghfp", bb, eye)
        return m.reshape(Dp, nblk, gpb * H, gpb * P).astype(BF16)

    def pack_out(cc):
        cc = cc.reshape(Dp, nblk, gpb, H, P)
        m = jnp.einsum("dkghp,gf->dkgpfh", cc, eye)
        return m.reshape(Dp, nblk, gpb * P, gpb * H).astype(BF16)

    return {"s5_are": abar_re, "s5_aim": abar_im,
            "s5_wbr": pack_in(bbr), "s5_wbi": pack_in(bbi),
            "s5_wcr": pack_out(c_re), "s5_wci": pack_out(c_im),
            "s5_d": d.reshape(Dp, 1, G * H)}


def _s5_kernel(u_ref, are_ref, aim_ref, wbr_ref, wbi_ref, wcr_ref, wci_ref, d_ref, y_ref,
               sre_ref, sim_ref, *block_refs):
    B, tl, W = u_ref.shape
    nblk = wbr_ref.shape[0]
    in_w = wbr_ref.shape[1]
    st_w = wbr_ref.shape[2]
    per_block = len(block_refs) // nblk
    blocks = [block_refs[kb * per_block:(kb + 1) * per_block] for kb in range(nblk)]

    @pl.when(pl.program_id(0) == 0)
    def _():
        sre_ref[...] = jnp.zeros_like(sre_ref)
        sim_ref[...] = jnp.zeros_like(sim_ref)

    assert in_w == LANES and nblk * LANES == W

    def gather(kb):
        u2_ref, ut_ref = blocks[kb][0], blocks[kb][1]
        for b in range(B):
            u2_ref[b * tl:(b + 1) * tl, :] = u_ref[b, :, kb * LANES:(kb + 1) * LANES]
        for t in range(tl):
            ut_ref[t * B:(t + 1) * B, :] = u2_ref[pl.ds(t, B, stride=tl), :]

    def emit(kb):
        yt_ref = blocks[kb][4]
        for b in range(B):
            y_ref[b, :, kb * LANES:(kb + 1) * LANES] = yt_ref[pl.ds(b, tl, stride=B), :]

    def drive(kb):
        _, ut_ref, bre_ref, bim_ref, _ = blocks[kb]
        lhs = ut_ref[...].astype(BF16)
        bre_ref[...] = _dot(lhs, wbr_ref[kb])
        bim_ref[...] = _dot(lhs, wbi_ref[kb])

    def scan(kb):
        _, _, bre_ref, bim_ref, _ = blocks[kb]
        sl = slice(kb * st_w, (kb + 1) * st_w)
        ar = jnp.broadcast_to(are_ref[:, sl], (B, st_w))
        ai = jnp.broadcast_to(aim_ref[:, sl], (B, st_w))
        sr = sre_ref[:, sl]
        si = sim_ref[:, sl]
        for t in range(tl):
            rows = slice(t * B, (t + 1) * B)
            sr, si = (ar * sr - ai * si + bre_ref[rows, :],
                      ar * si + ai * sr + bim_ref[rows, :])
            bre_ref[rows, :] = sr
            bim_ref[rows, :] = si
        sre_ref[:, sl] = sr
        sim_ref[:, sl] = si

    def project(kb):
        _, ut_ref, bre_ref, bim_ref, yt_ref = blocks[kb]
        y = (_dot(bre_ref[...].astype(BF16), wcr_ref[kb])
             - _dot(bim_ref[...].astype(BF16), wci_ref[kb]))
        yt_ref[...] = jax.nn.gelu(y + d_ref[:, kb * in_w:(kb + 1) * in_w] * ut_ref[...])

    gather(0)
    drive(0)
    for kb in range(nblk):
        if kb + 1 < nblk:
            gather(kb + 1)
            drive(kb + 1)
        scan(kb)
        if kb >= 1:
            project(kb - 1)
            emit(kb - 1)
    project(nblk - 1)
    emit(nblk - 1)


def _s5(u, p, l, B, L):
    W = u.shape[-1]
    tl = S5_TL
    n_state = p["s5_are"].shape[-1]
    nblk, in_w, st_w = p["s5_wbr"].shape[1:]
    blk = pl.BlockSpec((B, tl, W), lambda i: (0, i, 0))
    names = ["s5_are", "s5_aim", "s5_wbr", "s5_wbi", "s5_wcr", "s5_wci", "s5_d"]
    rows = B * tl
    per_block = [pltpu.VMEM((rows, in_w), F32), pltpu.VMEM((rows, in_w), F32),
                 pltpu.VMEM((rows, st_w), F32), pltpu.VMEM((rows, st_w), F32),
                 pltpu.VMEM((rows, in_w), F32)]
    return pl.pallas_call(
        _s5_kernel,
        grid=(L // tl,),
        in_specs=[blk] + [_slab_spec(p[n], (l,)) for n in names],
        out_specs=blk,
        out_shape=jax.ShapeDtypeStruct((B, L, W), F32),
        scratch_shapes=[pltpu.VMEM((B, n_state), F32), pltpu.VMEM((B, n_state), F32)]
                       + per_block * nblk,
        compiler_params=_cparams(("arbitrary",)),
        name="s5_scan",
    )(u.reshape(B, L, W), *[p[n] for n in names])


def _attn_kernel(q_ref, k_ref, v_ref, bias_ref, o_ref, s_ref, m_ref, mprev_ref, l_ref, acc_ref,
                 vt_ref, qt_ref):
    tq = q_ref.shape[1]
    nh = m_ref.shape[0]
    qi = pl.program_id(2)

    @pl.when(qi == 0)
    def _():
        for h in range(nh):
            for blk in range(v_ref.shape[1] // tq):
                rows = slice(blk * tq, (blk + 1) * tq)
                vt_ref[h, :, rows] = v_ref[0, rows, h * MLA_DV:(h + 1) * MLA_DV].T

    m_ref[...] = jnp.full(m_ref.shape, -1e30, F32)
    l_ref[...] = jnp.zeros_like(l_ref)
    acc_ref[...] = jnp.zeros_like(acc_ref)
    for h in range(nh):
        qt_ref[h] = q_ref[0, :, h * MLA_HEAD_PAD:(h + 1) * MLA_HEAD_PAD].T

    def scores(j, h, masked):
        r0 = pl.multiple_of(j * tq, tq)
        qk = slice(h * MLA_HEAD_PAD, (h + 1) * MLA_HEAD_PAD)
        st = _dot(k_ref[0, pl.ds(r0, tq), qk], qt_ref[h])
        if masked:
            st = st + bias_ref[...]
        s_ref[h] = st
        m_old = m_ref[h]
        mprev_ref[h] = m_old
        m_ref[h] = jnp.maximum(m_old, jnp.max(st, axis=0, keepdims=True))

    def accumulate(j, h):
        r0 = pl.multiple_of(j * tq, tq)
        m_new = m_ref[h]
        alpha = jnp.exp2(mprev_ref[h] - m_new)
        pt = jnp.exp2(s_ref[h] - m_new)
        l_ref[h] = alpha * l_ref[h] + jnp.sum(pt, axis=0, keepdims=True)
        acc_ref[h] = alpha * acc_ref[h] + _dot(vt_ref[h, :, pl.ds(r0, tq)],
                                               pt.astype(BF16))

    la = ATTN_LOOKAHEAD
    assert 0 < la < nh
    for h in range(nh):
        scores(qi, h, True)
        if h >= la:
            accumulate(qi, h - la)

    def one_block(j):
        prev = jnp.where(j == 0, qi, j - 1)
        for h in range(nh):
            scores(j, h, False)
            if h >= la:
                accumulate(j, h - la)
            else:
                accumulate(prev, h - la + nh)

    @pl.when(qi % 2 == 1)
    def _():
        one_block(0)

    @pl.when(qi % 4 >= 2)
    def _():
        j = qi % 2
        one_block(j)
        one_block(j + 1)

    def body(t, carry):
        j = qi % 4 + 4 * t
        for d in range(4):
            one_block(j + d)
        return carry
    lax.fori_loop(0, qi // 4, body, 0)
    last = jnp.maximum(qi - 1, 0)
    for h in range(nh - la, nh):
        accumulate(last, h)
    for h in range(nh):
        o_ref[0, :, h * MLA_DV:(h + 1) * MLA_DV] = (acc_ref[h] / l_ref[h]).T.astype(BF16)


def _attention(mq, mk, mv, B, L):
    tq = ATTN_TQ
    nh = ATTN_HEADS_PER_STEP
    qw = nh * MLA_HEAD_PAD
    vw = nh * MLA_DV
    kidx = lax.broadcasted_iota(jnp.int32, (tq, tq), 0)
    qidx = lax.broadcasted_iota(jnp.int32, (tq, tq), 1)
    bias = jnp.where(kidx <= qidx, 0.0, -jnp.inf).astype(F32)
    return pl.pallas_call(
        _attn_kernel,
        grid=(B, MLA_HEADS // nh, L // tq),
        in_specs=[pl.BlockSpec((1, tq, qw), lambda b, h, i: (b, i, h)),
                  pl.BlockSpec((1, L, qw), lambda b, h, i: (b, 0, h)),
                  pl.BlockSpec((1, L, vw), lambda b, h, i: (b, 0, h)),
                  _const_spec(bias.shape)],
        out_specs=pl.BlockSpec((1, tq, vw), lambda b, h, i: (b, i, h)),
        out_shape=jax.ShapeDtypeStruct((B, L, MLA_HEADS * MLA_DV), BF16),
        scratch_shapes=[pltpu.VMEM((nh, tq, tq), F32), pltpu.VMEM((nh, 1, tq), F32),
                        pltpu.VMEM((nh, 1, tq), F32), pltpu.VMEM((nh, 1, tq), F32),
                        pltpu.VMEM((nh, MLA_DV, tq), F32), pltpu.VMEM((nh, MLA_DV, L), BF16),
                        pltpu.VMEM((nh, MLA_HEAD_PAD, tq), BF16)],
        compiler_params=_cparams(("parallel", "parallel", "arbitrary")),
        name="mla_attention",
    )(mq.reshape(B, L, -1), mk.reshape(B, L, -1), mv.reshape(B, L, -1), bias)


def _merge_kernel(x_ref, xn_ref, gin_ref, gout_ref, a_ref, y_ref, o_ref_in, wg_ref, wro_ref,
                  wga_ref, wgb_ref, wmo_ref, wout_ref, out_ref, ha_ref):
    pr = ha_ref.shape[0]
    n_pieces = x_ref.shape[0] // pr
    D = x_ref.shape[1]
    gin = gin_ref[...]
    gout = gout_ref[...]

    @pl.when(pl.program_id(0) == 0)
    def _():
        ha_ref[...] = _rms(x_ref[:pr, :], gin).astype(BF16)

    def branches(h, piece):
        rows = slice(piece * pr, (piece + 1) * pr)
        ys = y_ref[rows, :].astype(BF16)
        return (_dot(h, wg_ref[:, 0:D]), _dot(a_ref[rows, :], wro_ref[...]),
                _dot(h, wg_ref[:, D:2 * D]), _dot(ys, wga_ref[...]), _dot(ys, wgb_ref[...]),
                _dot(h, wg_ref[:, 2 * D:3 * D]), _dot(o_ref_in[rows, :], wmo_ref[...]))

    def combine(g0, y_ret, g1, ya, yb, g2, y_mla):
        merged = jax.nn.sigmoid(g0) * y_ret
        merged = merged + jax.nn.sigmoid(g1) * (ya * jax.nn.sigmoid(yb))
        merged = merged + jax.nn.sigmoid(g2) * y_mla
        return merged.astype(BF16)

    def finish(piece, o):
        rows = slice(piece * pr, (piece + 1) * pr)
        out_ref[rows, :] = x_ref[rows, :] + _rms(o, gout)

    b_prev = branches(ha_ref[...], 0)
    o_prev = None
    for piece in range(1, n_pieces):
        b = branches(_rms(x_ref[piece * pr:(piece + 1) * pr, :], gin).astype(BF16), piece)
        o = _dot(combine(*b_prev), wout_ref[...])
        if o_prev is not None:
            finish(piece - 2, o_prev)
        b_prev, o_prev = b, o
    o = _dot(combine(*b_prev), wout_ref[...])
    if o_prev is not None:
        finish(n_pieces - 2, o_prev)
    ha_ref[...] = _rms(xn_ref[...], gin).astype(BF16)
    finish(n_pieces - 1, o)


def _merge(x2, a_ret, y_s5, o_mla, p, l):
    T, D = x2.shape
    tm = MERGE_PIECE * MERGE_PIECES
    n = T // tm
    row = lambda w: pl.BlockSpec((tm, w), lambda i: (i, 0))
    nxt = pl.BlockSpec((MERGE_PIECE, D),
                       lambda i: (jnp.minimum(i + 1, n - 1) * MERGE_PIECES, 0))
    n_sub = p["gains"].shape[0] // p["wgates"].shape[0]
    names = ["wgates", "ret_wo", "glu_a", "glu_b", "mla_wo", "w_out"]
    return pl.pallas_call(
        _merge_kernel,
        grid=(n,),
        in_specs=[row(D), nxt, _slab_spec(p["gains"], (l * n_sub + 2,)),
                  _slab_spec(p["gains"], (l * n_sub + 3,)),
                  row(a_ret.shape[1]), row(y_s5.shape[1]), row(o_mla.shape[1])]
                 + [_slab_spec(p[n], (l,)) for n in names],
        out_specs=row(D),
        out_shape=jax.ShapeDtypeStruct((T, D), F32),
        scratch_shapes=[pltpu.VMEM((MERGE_PIECE, D), BF16)],
        compiler_params=_cparams(("arbitrary",)),
        name="gated_merge",
    )(x2, x2, p["gains"], p["gains"], a_ret, y_s5, o_mla, *[p[n] for n in names])


def _prepare(norm_gains, ffn_w_gate, ffn_w_up, ffn_w_down, w_in, ret_w_o,
             s5_a_re, s5_a_im, s5_log_dt, s5_b_re, s5_b_im, s5_c_re, s5_c_im, s5_d,
             s5_glu_a, s5_glu_b, mla_q_norm, mla_kv_norm, mla_w_uq, mla_w_ukv, mla_w_o, w_out):
    depth, n_sub, D = norm_gains.shape
    q_rank = mla_q_norm.shape[-1]
    kv_rank = mla_kv_norm.shape[-1]
    s5_w = s5_d.shape[-1]
    ret_w = 2 * RET_HEADS * RET_DK + 2 * RET_HEADS * RET_DV
    m_lo = ret_w + s5_w
    m_w = q_rank + kv_rank + MLA_ROPE
    assert w_in.shape[-1] == m_lo + m_w + 3 * D
    hd_w = MLA_NOPE + MLA_ROPE
    wuq = jnp.pad(mla_w_uq.reshape(depth, q_rank, MLA_HEADS, hd_w),
                  ((0, 0), (0, 0), (0, 0), (0, MLA_HEAD_PAD - hd_w)))
    wukv = mla_w_ukv.reshape(depth, kv_rank, MLA_HEADS, MLA_NOPE + MLA_DV)
    p = {
        "gains": norm_gains.reshape(depth * n_sub, 1, D),
        "ffn_wg": ffn_w_gate.astype(BF16), "ffn_wu": ffn_w_up.astype(BF16),
        "ffn_wd": ffn_w_down.astype(BF16),
        "w_in": w_in.astype(BF16),
        "wm": jnp.pad(w_in[:, :, m_lo:m_lo + m_w],
                      ((0, 0), (0, 0), (0, LANES - MLA_ROPE))).astype(BF16),
        "wgates": w_in[:, :, m_lo + m_w:].astype(BF16),
        "qn": mla_q_norm.reshape(depth, 1, q_rank), "kvn": mla_kv_norm.reshape(depth, 1, kv_rank),
        "wuq": wuq.reshape(depth, q_rank, MLA_HEADS * MLA_HEAD_PAD).astype(BF16),
        "wuk": wukv[..., :MLA_NOPE].reshape(depth, kv_rank, MLA_HEADS * MLA_NOPE).astype(BF16),
        "wuv": wukv[..., MLA_NOPE:].reshape(depth, kv_rank, MLA_HEADS * MLA_DV).astype(BF16),
        "ret_wo": ret_w_o.astype(BF16), "glu_a": s5_glu_a.astype(BF16),
        "glu_b": s5_glu_b.astype(BF16), "mla_wo": mla_w_o.astype(BF16),
        "w_out": w_out.astype(BF16),
    }
    p.update(_s5_params(s5_a_re, s5_a_im, s5_log_dt, s5_b_re, s5_b_im, s5_c_re, s5_c_im, s5_d))
    return p


def kernel(x, positions, norm_gains, ffn_w_gate, ffn_w_up, ffn_w_down, w_in, ret_w_o,
           s5_a_re, s5_a_im, s5_log_dt, s5_b_re, s5_b_im, s5_c_re, s5_c_im, s5_d,
           s5_glu_a, s5_glu_b, mla_q_norm, mla_kv_norm, mla_w_uq, mla_w_ukv, mla_w_o, w_out):
    B, L, D = x.shape
    depth = norm_gains.shape[0]
    assert L % RET_ROWS == 0 and L % ATTN_TQ == 0 and L % S5_TL == 0
    for tile in (ROW_TILE, FFN_PIECE * FFN_PIECES, MERGE_PIECE * MERGE_PIECES):
        assert (B * L) % tile == 0
    assert B == SUBLANES, "the S5 scan keeps the batch on the sublane axis"

    p = _prepare(norm_gains, ffn_w_gate, ffn_w_up, ffn_w_down, w_in, ret_w_o,
                 s5_a_re, s5_a_im, s5_log_dt, s5_b_re, s5_b_im, s5_c_re, s5_c_im, s5_d,
                 s5_glu_a, s5_glu_b, mla_q_norm, mla_kv_norm, mla_w_uq, mla_w_ukv, mla_w_o, w_out)
    tabs = _rope_tables(positions)
    ret_tabs = _ret_tables()
    x2 = x.reshape(B * L, D)
    for l in range(depth):
        x2 = _ffn(x2, p, l, 0)
        rq, rk, rv, rg, u, mq, mk, mv = _proj(x2, p, l, tabs)
        a_ret = _retention(rq, rk, rv, rg, ret_tabs, B, L)
        y_s5 = _s5(u, p, l, B, L)
        o_mla = _attention(mq, mk, mv, B, L)
        x2 = _merge(x2, a_ret.reshape(B * L, -1), y_s5.reshape(B * L, -1),
                    o_mla.reshape(B * L, -1), p, l)
        x2 = _ffn(x2, p, l, 1)
    return x2.reshape(B, L, D)
```

```python
import math

import jax
import jax.numpy as jnp
from jax import lax
from jax.experimental import pallas as pl
from jax.experimental.pallas import tpu as pltpu

F32 = jnp.float32
BF16 = jnp.bfloat16

ROPE_BASE = 10000.0
NORM_EPS = 1e-6
GN_EPS = 1e-5
FFN_RES = 0.5
RET_HEADS = 4
RET_DK = 128
RET_DV = 128
RET_CHUNK = 128
MLA_HEADS = 4
MLA_NOPE = 128
MLA_ROPE = 64
MLA_DV = 128
MLA_HEAD_PAD = 256
MLA_LOG2_SCALE = (MLA_NOPE + MLA_ROPE) ** -0.5 * math.log2(math.e)

LANES = 128
SUBLANES = 8
VMEM_LIMIT = 56 * 1024 * 1024

ROW_TILE = 512
FFN_PIECE = 256
FFN_PIECES = 4
MERGE_PIECE = 256
MERGE_PIECES = 2
FFN_CHUNK = 1024
RET_ROWS = 512
S5_TL = 128
ATTN_TQ = 512
ATTN_HEADS_PER_STEP = 4
ATTN_LOOKAHEAD = 2


def _cparams(sem):
    return pltpu.CompilerParams(dimension_semantics=sem, vmem_limit_bytes=VMEM_LIMIT)


def _const_spec(shape):
    nd = len(shape)
    return pl.BlockSpec(shape, lambda *_: (0,) * nd, pipeline_mode=pl.Buffered(1))


def _slab_spec(arr, lead, block=None, col=0):
    tail = tuple(arr.shape[len(lead):]) if block is None else tuple(block)
    idx = tuple(lead) + (0,) * (len(tail) - 1) + (col,)
    return pl.BlockSpec((None,) * len(lead) + tail, lambda *_: idx,
                        pipeline_mode=pl.Buffered(1))


def _rms(x, g):
    y = x * lax.rsqrt(jnp.mean(x * x, axis=-1, keepdims=True) + NORM_EPS)
    return y * g


def _dot(a, b):
    return jnp.dot(a, b, preferred_element_type=F32)


def _rope_kernel(pos_ref, f_ref, a_ref, na_ref, b_ref, p_ref, np_ref, q_ref,
                 cr_ref, sr_ref, cm_ref, s1_ref, s2_ref):
    ang = pos_ref[...] * f_ref[...]
    c = jnp.cos(ang)
    s = jnp.sin(ang)
    hr = RET_DK // 2
    c_hi = pltpu.roll(c, hr, 1)
    s_hi = pltpu.roll(s, hr, 1)
    c_m2 = pltpu.roll(c, hr + MLA_ROPE // 2, 1)
    s_m2 = pltpu.roll(s, hr + MLA_ROPE // 2, 1)
    cr_ref[...] = c * a_ref[...] + c_hi * b_ref[...]
    sr_ref[...] = s * na_ref[...] + s_hi * b_ref[...]
    cm_ref[...] = c_hi * p_ref[...] + c_m2 * q_ref[...]
    s1_ref[...] = s_hi * np_ref[...]
    s2_ref[...] = s_m2 * q_ref[...]


def _rope_tables(positions):
    B, L = positions.shape
    T = B * L
    pos = positions.astype(F32).reshape(T, 1)
    inv_r = ROPE_BASE ** (-jnp.arange(0, RET_DK, 2, dtype=F32) / RET_DK)
    inv_m = ROPE_BASE ** (-jnp.arange(0, MLA_ROPE, 2, dtype=F32) / MLA_ROPE)
    hr = RET_DK // 2
    hm = MLA_ROPE // 2
    lane = jnp.arange(LANES)
    freq = jnp.concatenate([inv_r, inv_m, jnp.zeros((LANES - hr - hm,), F32)])
    a = (lane < hr).astype(F32)
    b = (lane >= hr).astype(F32)
    p = (lane < hm).astype(F32)
    q = ((lane >= hm) & (lane < 2 * hm)).astype(F32)
    rows = [r.reshape(1, LANES) for r in (freq, a, -a, b, p, -p, q)]
    tm = 2048
    row_spec = pl.BlockSpec((1, LANES), lambda i: (0, 0))
    tab_spec = pl.BlockSpec((tm, LANES), lambda i: (i, 0))
    tab_shape = jax.ShapeDtypeStruct((T, LANES), F32)
    return pl.pallas_call(
        _rope_kernel,
        grid=(T // tm,),
        in_specs=[pl.BlockSpec((tm, 1), lambda i: (i, 0))] + [row_spec] * len(rows),
        out_specs=[tab_spec] * 5,
        out_shape=[tab_shape] * 5,
        compiler_params=_cparams(("parallel",)),
        name="rope_tables",
    )(pos, *rows)


def _ffn_kernel(x_ref, xn_ref, gpre_ref, gpost_ref, wg_ref, wu_ref, wd_ref, o_ref, ha_ref):
    pr = ha_ref.shape[0]
    n_pieces = x_ref.shape[0] // pr
    gpre = gpre_ref[...]
    gpost = gpost_ref[...]

    @pl.when(pl.program_id(0) == 0)
    def _():
        ha_ref[...] = _rms(x_ref[:pr, :], gpre).astype(BF16)

    def swiglu(h):
        d_ff = wg_ref.shape[1]
        chunks = [(c, min(c + FFN_CHUNK, d_ff)) for c in range(0, d_ff, FFN_CHUNK)]
        y = None
        pending = None
        for c0, c1 in chunks + [(None, None)]:
            if c0 is not None:
                g = _dot(h, wg_ref[:, c0:c1])
                u = _dot(h, wu_ref[:, c0:c1])
            if pending is not None:
                a, p0, p1 = pending
                part = _dot(a, wd_ref[p0:p1, :])
                y = part if y is None else y + part
            if c0 is not None:
                pending = ((g * jax.nn.sigmoid(g) * u).astype(BF16), c0, c1)
        return y

    def finish(piece, y):
        rows = slice(piece * pr, (piece + 1) * pr)
        o_ref[rows, :] = x_ref[rows, :] + FFN_RES * _rms(y, gpost)

    y_prev = swiglu(ha_ref[...])
    for piece in range(1, n_pieces):
        y = swiglu(_rms(x_ref[piece * pr:(piece + 1) * pr, :], gpre).astype(BF16))
        finish(piece - 1, y_prev)
        y_prev = y
    ha_ref[...] = _rms(xn_ref[...], gpre).astype(BF16)
    finish(n_pieces - 1, y_prev)


def _ffn(x2, p, l, k):
    T, D = x2.shape
    half = FFN_PIECE
    tile = FFN_PIECE * FFN_PIECES
    n = T // tile
    row = pl.BlockSpec((tile, D), lambda i: (i, 0))
    nxt = pl.BlockSpec((half, D), lambda i: (jnp.minimum(i + 1, n - 1) * FFN_PIECES, 0))
    n_sub = p["gains"].shape[0] // p["ffn_wg"].shape[0]
    gi = l * n_sub + 4 * k
    return pl.pallas_call(
        _ffn_kernel,
        grid=(n,),
        in_specs=[row, nxt, _slab_spec(p["gains"], (gi,)), _slab_spec(p["gains"], (gi + 1,)),
                  _slab_spec(p["ffn_wg"], (l, k)), _slab_spec(p["ffn_wu"], (l, k)),
                  _slab_spec(p["ffn_wd"], (l, k))],
        out_specs=row,
        out_shape=jax.ShapeDtypeStruct((T, D), F32),
        scratch_shapes=[pltpu.VMEM((half, D), BF16)],
        compiler_params=_cparams(("arbitrary",)),
        name="ffn",
    )(x2, x2, p["gains"], p["gains"], p["ffn_wg"], p["ffn_wu"], p["ffn_wd"])


def _rope_ret(v, cos, sin):
    return v * cos + pltpu.roll(v, RET_DK // 2, 1) * sin


def _rope_mla(v, cos, sin1, sin2):
    hm = MLA_ROPE // 2
    return v * cos + pltpu.roll(v, LANES - hm, 1) * sin1 + pltpu.roll(v, hm, 1) * sin2


def _proj_kernel(x_ref, xn_ref, g_ref, wr_ref, wu_ref, wm_ref, qn_ref, kvn_ref,
                 wuq_ref, wuk_ref, wuv_ref, cr_ref, sr_ref, cm_ref, s1_ref, s2_ref,
                 rq_ref, rk_ref, rv_ref, rg_ref, u_ref, mq_ref, mk_ref, mv_ref, ha_ref):
    half = x_ref.shape[0] // 2
    gain = g_ref[...]
    qk_w = RET_HEADS * RET_DK
    v_w = RET_HEADS * RET_DV
    q_rank = qn_ref.shape[1]
    kv_rank = kvn_ref.shape[1]

    @pl.when(pl.program_id(0) == 0)
    def _():
        ha_ref[...] = _rms(x_ref[:half, :], gain).astype(BF16)

    def project(h):
        m = _dot(h, wm_ref[...])
        r = _dot(h, wr_ref[...])
        u = _dot(h, wu_ref[...])
        return m, r, u

    def mla_up(m):
        cq = _rms(m[:, :q_rank], qn_ref[...]).astype(BF16)
        ckv = _rms(m[:, q_rank:q_rank + kv_rank], kvn_ref[...]).astype(BF16)
        q = _dot(cq, wuq_ref[...])
        kn = _dot(ckv, wuk_ref[...])
        v = _dot(ckv, wuv_ref[...])
        return q, kn, v

    def store(rows, m, r, u, q, kn, v):
        cr = cr_ref[rows, :]
        sr = sr_ref[rows, :]
        for hd in range(RET_HEADS):
            lo = hd * RET_DK
            hcols = slice(lo, lo + RET_DK)
            qr = _rope_ret(r[:, lo:lo + RET_DK], cr, sr)
            kr = _rope_ret(r[:, qk_w + lo:qk_w + lo + RET_DK], cr, sr) * (RET_DK ** -0.5)
            rq_ref[rows, hcols] = qr.astype(BF16)
            rk_ref[rows, hcols] = kr.astype(BF16)
        rv_ref[rows, :] = r[:, 2 * qk_w:2 * qk_w + v_w].astype(BF16)
        rg_ref[rows, :] = r[:, 2 * qk_w + v_w:].astype(BF16)
        u_ref[rows, :] = u
        cm = cm_ref[rows, :]
        s1 = s1_ref[rows, :]
        s2 = s2_ref[rows, :]
        kpe = _rope_mla(m[:, q_rank + kv_rank:], cm, s1, s2).astype(BF16)
        mv_ref[rows, :] = v.astype(BF16)
        for hd in range(MLA_HEADS):
            lo = hd * MLA_HEAD_PAD
            mq_ref[rows, lo:lo + MLA_NOPE] = (q[:, lo:lo + MLA_NOPE] * MLA_LOG2_SCALE).astype(BF16)
            mq_ref[rows, lo + MLA_NOPE:lo + MLA_HEAD_PAD] = (_rope_mla(
                q[:, lo + MLA_NOPE:lo + MLA_HEAD_PAD], cm, s1, s2) * MLA_LOG2_SCALE).astype(BF16)
            mk_ref[rows, lo:lo + MLA_NOPE] = kn[:, hd * MLA_NOPE:(hd + 1) * MLA_NOPE].astype(BF16)
            mk_ref[rows, lo + MLA_NOPE:lo + MLA_HEAD_PAD] = kpe

    pa = project(ha_ref[...])
    pb = project(_rms(x_ref[half:, :], gain).astype(BF16))
    ua = mla_up(pa[0])
    ub = mla_up(pb[0])
    store(slice(0, half), *pa, *ua)
    ha_ref[...] = _rms(xn_ref[...], gain).astype(BF16)
    store(slice(half, 2 * half), *pb, *ub)


def _proj(x2, p, l, tabs):
    T, D = x2.shape
    tm = ROW_TILE
    n = T // tm
    row = lambda n: pl.BlockSpec((tm, n), lambda i: (i, 0))
    nxt = pl.BlockSpec((tm // 2, D), lambda i: (jnp.minimum(2 * i + 2, 2 * n - 2), 0))
    n_sub = p["gains"].shape[0] // p["w_in"].shape[0]
    ret_w = 2 * RET_HEADS * RET_DK + 2 * RET_HEADS * RET_DV
    s5_w = p["s5_d"].shape[-1]
    assert ret_w % s5_w == 0
    consts = [(p["gains"], _slab_spec(p["gains"], (l * n_sub + 2,))),
              (p["w_in"], _slab_spec(p["w_in"], (l,), block=(D, ret_w), col=0)),
              (p["w_in"], _slab_spec(p["w_in"], (l,), block=(D, s5_w), col=ret_w // s5_w)),
              (p["wm"], _slab_spec(p["wm"], (l,))),
              (p["qn"], _slab_spec(p["qn"], (l,))), (p["kvn"], _slab_spec(p["kvn"], (l,))),
              (p["wuq"], _slab_spec(p["wuq"], (l,))), (p["wuk"], _slab_spec(p["wuk"], (l,))),
              (p["wuv"], _slab_spec(p["wuv"], (l,)))]
    out_w = [(RET_HEADS * RET_DK, BF16)] * 2 + [
             (RET_HEADS * RET_DV, BF16), (RET_HEADS * RET_DV, BF16), (s5_w, F32),
             (MLA_HEADS * MLA_HEAD_PAD, BF16), (MLA_HEADS * MLA_HEAD_PAD, BF16),
             (MLA_HEADS * MLA_DV, BF16)]
    return pl.pallas_call(
        _proj_kernel,
        grid=(n,),
        in_specs=[row(D), nxt] + [s for _, s in consts] + [row(LANES)] * 5,
        out_specs=[row(w) for w, _ in out_w],
        out_shape=[jax.ShapeDtypeStruct((T, w), dt) for w, dt in out_w],
        scratch_shapes=[pltpu.VMEM((tm // 2, D), BF16)],
        compiler_params=_cparams(("arbitrary",)),
        name="mixer_proj",
    )(x2, x2, *[a for a, _ in consts], *tabs)


def _ret_kernel(q_ref, k_ref, v_ref, g_ref, intra_ref, qd_ref, kd_ref, cd_ref, o_ref, st_ref):
    @pl.when(pl.program_id(1) == 0)
    def _():
        st_ref[...] = jnp.zeros_like(st_ref)

    C = RET_CHUNK
    states = [st_ref[hd] for hd in range(RET_HEADS)]
    for c in range(RET_ROWS // C):
        rows = slice(c * C, (c + 1) * C)
        for hd in range(RET_HEADS):
            cols = slice(hd * RET_DK, (hd + 1) * RET_DK)
            vcols = slice(hd * RET_DV, (hd + 1) * RET_DV)
            q = q_ref[0, rows, cols]
            kt = k_ref[0, rows, cols].T
            v = v_ref[0, rows, vcols]
            scores = _dot(q, kt) * intra_ref[hd]
            inner = _dot(scores.astype(BF16), v)
            cross = _dot((q.astype(F32) * qd_ref[hd]).astype(BF16), states[hd].astype(BF16))
            kv = _dot((kt.astype(F32) * kd_ref[hd]).astype(BF16), v)
            states[hd] = cd_ref[hd] * states[hd] + kv
            o = inner + cross
            mu = jnp.mean(o, axis=-1, keepdims=True)
            var = jnp.mean(jnp.square(o - mu), axis=-1, keepdims=True)
            on = (o - mu) * lax.rsqrt(var + GN_EPS)
            g = g_ref[0, rows, vcols].astype(F32)
            o_ref[0, rows, vcols] = (g * jax.nn.sigmoid(g) * on).astype(BF16)
    for hd in range(RET_HEADS):
        st_ref[hd] = states[hd]


def _ret_tables():
    C = RET_CHUNK
    log_gamma = jnp.log1p(-jnp.exp2(-5.0 - jnp.arange(RET_HEADS, dtype=F32)))
    pos = jnp.arange(C, dtype=F32)
    rel = pos[:, None] - pos[None, :]
    intra = jnp.where(rel[None] >= 0.0,
                      jnp.exp(jnp.maximum(rel, 0.0)[None] * log_gamma[:, None, None]), 0.0)
    k_decay = jnp.exp((C - 1.0 - pos)[:, None] * log_gamma[None, :])
    q_decay = jnp.exp((pos + 1.0)[:, None] * log_gamma[None, :])
    chunk_decay = jnp.exp(C * log_gamma)
    qd = jnp.broadcast_to(q_decay.T[:, :, None], (RET_HEADS, C, RET_DK))
    kd = jnp.broadcast_to(k_decay.T[:, None, :], (RET_HEADS, RET_DK, C))
    cd = jnp.broadcast_to(chunk_decay[:, None, None], (RET_HEADS, 1, RET_DV))
    return intra, qd, kd, cd


def _retention(rq, rk, rv, rg, tables, B, L):
    W = rq.shape[-1]
    shp = (B, L, W)
    blk = pl.BlockSpec((1, RET_ROWS, W), lambda b, i: (b, i, 0))
    seq = [a.reshape(shp) for a in (rq, rk, rv, rg)]
    return pl.pallas_call(
        _ret_kernel,
        grid=(B, L // RET_ROWS),
        in_specs=[blk] * len(seq) + [_const_spec(t.shape) for t in tables],
        out_specs=blk,
        out_shape=jax.ShapeDtypeStruct(shp, BF16),
        scratch_shapes=[pltpu.VMEM((RET_HEADS, RET_DK, RET_DV), F32)],
        compiler_params=_cparams(("parallel", "arbitrary")),
        name="retention",
    )(*seq, *tables)


def _s5_zoh_kernel(are_ref, aim_ref, ldt_ref, bre_ref, bim_ref,
                   abr_ref, abi_ref, bbr_ref, bbi_ref):
    a_re = are_ref[...]
    a_im = aim_ref[...]
    dt = jnp.exp(ldt_ref[...])
    mag = jnp.exp(a_re * dt)
    abar_re = mag * jnp.cos(a_im * dt)
    abar_im = mag * jnp.sin(a_im * dt)
    den = a_re * a_re + a_im * a_im
    nr = abar_re - 1.0
    f_re = (nr * a_re + abar_im * a_im) / den
    f_im = (abar_im * a_re - nr * a_im) / den
    b_re = bre_ref[...]
    b_im = bim_ref[...]
    abr_ref[...] = abar_re
    abi_ref[...] = abar_im
    bbr_ref[...] = f_re * b_re - f_im * b_im
    bbi_ref[...] = f_re * b_im + f_im * b_re


def _s5_params(a_re, a_im, log_dt, b_re, b_im, c_re, c_im, d):
    Dp, G, P = a_re.shape
    H = b_re.shape[-1]
    R = Dp * G * H
    rep = lambda a: jnp.broadcast_to(a[:, :, None, :], (Dp, G, H, a.shape[-1])).reshape(R, -1)
    bt = lambda b: jnp.swapaxes(b, 2, 3).reshape(R, P)
    args = [rep(a_re), rep(a_im), rep(log_dt[:, :, None]), bt(b_re), bt(b_im)]
    out = jax.ShapeDtypeStruct((R, P), F32)
    abr, abi, bbr, bbi = pl.pallas_call(
        _s5_zoh_kernel, out_shape=[out] * 4, name="s5_discretise",
        compiler_params=pltpu.CompilerParams(vmem_limit_bytes=VMEM_LIMIT),
    )(*args)
    abar_re = abr.reshape(Dp, G, H, P)[:, :, 0, :].reshape(Dp, 1, G * P)
    abar_im = abi.reshape(Dp, G, H, P)[:, :, 0, :].reshape(Dp, 1, G * P)
    gpb = LANES // H
    nblk = G // gpb
    group_of = lambda n, per: jnp.arange(n) // per

    def block_diag(rows2d, per_row, per_col):
        n_rows, n_cols = rows2d.shape[-2], rows2d.shape[-1] * gpb
        keep = group_of(n_rows, per_row)[:, None] == group_of(n_cols, per_col)[None, :]
        return jnp.where(keep, jnp.tile(rows2d, (1, 1, 1, gpb)), 0.0).astype(BF16)

    def pack_in(bb):
        return block_diag(bb.reshape(Dp, nblk, gpb * H, P), H, P)

    def pack_out(cc):
        return block_diag(jnp.swapaxes(cc, 2, 3).reshape(Dp, nblk, gpb * P, H), P, H)

    return {"s5_are": abar_re, "s5_aim": abar_im,
            "s5_wbr": pack_in(bbr), "s5_wbi": pack_in(bbi),
            "s5_wcr": pack_out(c_re), "s5_wci": pack_out(c_im),
            "s5_d": d.reshape(Dp, 1, G * H)}


def _s5_kernel(u_ref, are_ref, aim_ref, wbr_ref, wbi_ref, wcr_ref, wci_ref, d_ref, y_ref,
               sre_ref, sim_ref, *block_refs):
    B, tl, W = u_ref.shape
    nblk = wbr_ref.shape[0]
    in_w = wbr_ref.shape[1]
    st_w = wbr_ref.shape[2]
    per_block = len(block_refs) // nblk
    blocks = [block_refs[kb * per_block:(kb + 1) * per_block] for kb in range(nblk)]

    @pl.when(pl.program_id(0) == 0)
    def _():
        sre_ref[...] = jnp.zeros_like(sre_ref)
        sim_ref[...] = jnp.zeros_like(sim_ref)

    assert in_w == LANES and nblk * LANES == W

    def gather(kb):
        u2_ref, ut_ref = blocks[kb][0], blocks[kb][1]
        for b in range(B):
            u2_ref[b * tl:(b + 1) * tl, :] = u_ref[b, :, kb * LANES:(kb + 1) * LANES]
        for t in range(tl):
            ut_ref[t * B:(t + 1) * B, :] = u2_ref[pl.ds(t, B, stride=tl), :]

    def emit(kb):
        yt_ref = blocks[kb][4]
        for b in range(B):
            y_ref[b, :, kb * LANES:(kb + 1) * LANES] = yt_ref[pl.ds(b, tl, stride=B), :]

    def drive(kb):
        _, ut_ref, bre_ref, bim_ref, _ = blocks[kb]
        lhs = ut_ref[...].astype(BF16)
        bre_ref[...] = _dot(lhs, wbr_ref[kb])
        bim_ref[...] = _dot(lhs, wbi_ref[kb])

    def scan(kb):
        _, _, bre_ref, bim_ref, _ = blocks[kb]
        sl = slice(kb * st_w, (kb + 1) * st_w)
        ar = jnp.broadcast_to(are_ref[:, sl], (B, st_w))
        ai = jnp.broadcast_to(aim_ref[:, sl], (B, st_w))
        sr = sre_ref[:, sl]
        si = sim_ref[:, sl]
        for t in range(tl):
            rows = slice(t * B, (t + 1) * B)
            sr, si = (ar * sr - ai * si + bre_ref[rows, :],
                      ar * si + ai * sr + bim_ref[rows, :])
            bre_ref[rows, :] = sr
            bim_ref[rows, :] = si
        sre_ref[:, sl] = sr
        sim_ref[:, sl] = si

    def project(kb):
        _, ut_ref, bre_ref, bim_ref, yt_ref = blocks[kb]
        y = (_dot(bre_ref[...].astype(BF16), wcr_ref[kb])
             - _dot(bim_ref[...].astype(BF16), wci_ref[kb]))
        yt_ref[...] = jax.nn.gelu(y + d_ref[:, kb * in_w:(kb + 1) * in_w] * ut_ref[...])

    gather(0)
    drive(0)
    for kb in range(nblk):
        if kb + 1 < nblk:
            gather(kb + 1)
            drive(kb + 1)
        scan(kb)
        if kb >= 1:
            project(kb - 1)
            emit(kb - 1)
    project(nblk - 1)
    emit(nblk - 1)


def _s5(u, p, l, B, L):
    W = u.shape[-1]
    tl = S5_TL
    n_state = p["s5_are"].shape[-1]
    nblk, in_w, st_w = p["s5_wbr"].shape[1:]
    blk = pl.BlockSpec((B, tl, W), lambda i: (0, i, 0))
    names = ["s5_are", "s5_aim", "s5_wbr", "s5_wbi", "s5_wcr", "s5_wci", "s5_d"]
    rows = B * tl
    per_block = [pltpu.VMEM((rows, in_w), F32), pltpu.VMEM((rows, in_w), F32),
                 pltpu.VMEM((rows, st_w), F32), pltpu.VMEM((rows, st_w), F32),
                 pltpu.VMEM((rows, in_w), F32)]
    return pl.pallas_call(
        _s5_kernel,
        grid=(L // tl,),
        in_specs=[blk] + [_slab_spec(p[n], (l,)) for n in names],
        out_specs=blk,
        out_shape=jax.ShapeDtypeStruct((B, L, W), F32),
        scratch_shapes=[pltpu.VMEM((B, n_state), F32), pltpu.VMEM((B, n_state), F32)]
                       + per_block * nblk,
        compiler_params=_cparams(("arbitrary",)),
        name="s5_scan",
    )(u.reshape(B, L, W), *[p[n] for n in names])


def _attn_kernel(q_ref, k_ref, v_ref, bias_ref, o_ref, s_ref, m_ref, mprev_ref, l_ref, acc_ref,
                 vt_ref, qt_ref):
    tq = q_ref.shape[1]
    nh = m_ref.shape[0]
    qi = pl.program_id(2)

    @pl.when(qi == 0)
    def _():
        for h in range(nh):
            for blk in range(v_ref.shape[1] // tq):
                rows = slice(blk * tq, (blk + 1) * tq)
                vt_ref[h, :, rows] = v_ref[0, rows, h * MLA_DV:(h + 1) * MLA_DV].T

    m_ref[...] = jnp.full(m_ref.shape, -1e30, F32)
    l_ref[...] = jnp.zeros_like(l_ref)
    acc_ref[...] = jnp.zeros_like(acc_ref)
    for h in range(nh):
        qt_ref[h] = q_ref[0, :, h * MLA_HEAD_PAD:(h + 1) * MLA_HEAD_PAD].T

    def scores(j, h, masked):
        r0 = pl.multiple_of(j * tq, tq)
        qk = slice(h * MLA_HEAD_PAD, (h + 1) * MLA_HEAD_PAD)
        st = _dot(k_ref[0, pl.ds(r0, tq), qk], qt_ref[h])
        if masked:
            st = st + bias_ref[...]
        s_ref[h] = st
        m_old = m_ref[h]
        mprev_ref[h] = m_old
        m_ref[h] = jnp.maximum(m_old, jnp.max(st, axis=0, keepdims=True))

    def accumulate(j, h):
        r0 = pl.multiple_of(j * tq, tq)
        m_new = m_ref[h]
        alpha = jnp.exp2(mprev_ref[h] - m_new)
        pt = jnp.exp2(s_ref[h] - m_new)
        l_ref[h] = alpha * l_ref[h] + jnp.sum(pt, axis=0, keepdims=True)
        acc_ref[h] = alpha * acc_ref[h] + _dot(vt_ref[h, :, pl.ds(r0, tq)],
                                               pt.astype(BF16))

    la = ATTN_LOOKAHEAD
    assert 0 < la < nh
    for h in range(nh):
        scores(qi, h, True)
        if h >= la:
            accumulate(qi, h - la)

    def one_block(j):
        prev = jnp.where(j == 0, qi, j - 1)
        for h in range(nh):
            scores(j, h, False)
            if h >= la:
                accumulate(j, h - la)
            else:
                accumulate(prev, h - la + nh)

    @pl.when(qi % 2 == 1)
    def _():
        one_block(0)

    @pl.when(qi % 4 >= 2)
    def _():
        j = qi % 2
        one_block(j)
        one_block(j + 1)

    def body(t, carry):
        j = qi % 4 + 4 * t
        for d in range(4):
            one_block(j + d)
        return carry
    lax.fori_loop(0, qi // 4, body, 0)
    last = jnp.maximum(qi - 1, 0)
    for h in range(nh - la, nh):
        accumulate(last, h)
    for h in range(nh):
        o_ref[0, :, h * MLA_DV:(h + 1) * MLA_DV] = (acc_ref[h] / l_ref[h]).T.astype(BF16)


def _attention(mq, mk, mv, B, L):
    tq = ATTN_TQ
    nh = ATTN_HEADS_PER_STEP
    qw = nh * MLA_HEAD_PAD
    vw = nh * MLA_DV
    kidx = lax.broadcasted_iota(jnp.int32, (tq, tq), 0)
    qidx = lax.broadcasted_iota(jnp.int32, (tq, tq), 1)
    bias = jnp.where(kidx <= qidx, 0.0, -jnp.inf).astype(F32)
    return pl.pallas_call(
        _attn_kernel,
        grid=(B, MLA_HEADS // nh, L // tq),
        in_specs=[pl.BlockSpec((1, tq, qw), lambda b, h, i: (b, i, h)),
                  pl.BlockSpec((1, L, qw), lambda b, h, i: (b, 0, h)),
                  pl.BlockSpec((1, L, vw), lambda b, h, i: (b, 0, h)),
                  _const_spec(bias.shape)],
        out_specs=pl.BlockSpec((1, tq, vw), lambda b, h, i: (b, i, h)),
        out_shape=jax.ShapeDtypeStruct((B, L, MLA_HEADS * MLA_DV), BF16),
        scratch_shapes=[pltpu.VMEM((nh, tq, tq), F32), pltpu.VMEM((nh, 1, tq), F32),
                        pltpu.VMEM((nh, 1, tq), F32), pltpu.VMEM((nh, 1, tq), F32),
                        pltpu.VMEM((nh, MLA_DV, tq), F32), pltpu.VMEM((nh, MLA_DV, L), BF16),
                        pltpu.VMEM((nh, MLA_HEAD_PAD, tq), BF16)],
        compiler_params=_cparams(("parallel", "parallel", "arbitrary")),
        name="mla_attention",
    )(mq.reshape(B, L, -1), mk.reshape(B, L, -1), mv.reshape(B, L, -1), bias)


def _merge_kernel(x_ref, xn_ref, gin_ref, gout_ref, a_ref, y_ref, o_ref_in, wg_ref, wro_ref,
                  wga_ref, wgb_ref, wmo_ref, wout_ref, out_ref, ha_ref):
    pr = ha_ref.shape[0]
    n_pieces = x_ref.shape[0] // pr
    D = x_ref.shape[1]
    gin = gin_ref[...]
    gout = gout_ref[...]

    @pl.when(pl.program_id(0) == 0)
    def _():
        ha_ref[...] = _rms(x_ref[:pr, :], gin).astype(BF16)

    def branches(h, piece):
        rows = slice(piece * pr, (piece + 1) * pr)
        ys = y_ref[rows, :].astype(BF16)
        return (_dot(h, wg_ref[:, 0:D]), _dot(a_ref[rows, :], wro_ref[...]),
                _dot(h, wg_ref[:, D:2 * D]), _dot(ys, wga_ref[...]), _dot(ys, wgb_ref[...]),
                _dot(h, wg_ref[:, 2 * D:3 * D]), _dot(o_ref_in[rows, :], wmo_ref[...]))

    def combine(g0, y_ret, g1, ya, yb, g2, y_mla):
        merged = jax.nn.sigmoid(g0) * y_ret
        merged = merged + jax.nn.sigmoid(g1) * (ya * jax.nn.sigmoid(yb))
        merged = merged + jax.nn.sigmoid(g2) * y_mla
        return merged.astype(BF16)

    def finish(piece, o):
        rows = slice(piece * pr, (piece + 1) * pr)
        out_ref[rows, :] = x_ref[rows, :] + _rms(o, gout)

    b_prev = branches(ha_ref[...], 0)
    o_prev = None
    for piece in range(1, n_pieces):
        b = branches(_rms(x_ref[piece * pr:(piece + 1) * pr, :], gin).astype(BF16), piece)
        o = _dot(combine(*b_prev), wout_ref[...])
        if o_prev is not None:
            finish(piece - 2, o_prev)
        b_prev, o_prev = b, o
    o = _dot(combine(*b_prev), wout_ref[...])
    if o_prev is not None:
        finish(n_pieces - 2, o_prev)
    ha_ref[...] = _rms(xn_ref[...], gin).astype(BF16)
    finish(n_pieces - 1, o)


def _merge(x2, a_ret, y_s5, o_mla, p, l):
    T, D = x2.shape
    tm = MERGE_PIECE * MERGE_PIECES
    n = T // tm
    row = lambda w: pl.BlockSpec((tm, w), lambda i: (i, 0))
    nxt = pl.BlockSpec((MERGE_PIECE, D),
                       lambda i: (jnp.minimum(i + 1, n - 1) * MERGE_PIECES, 0))
    n_sub = p["gains"].shape[0] // p["wgates"].shape[0]
    names = ["wgates", "ret_wo", "glu_a", "glu_b", "mla_wo", "w_out"]
    return pl.pallas_call(
        _merge_kernel,
        grid=(n,),
        in_specs=[row(D), nxt, _slab_spec(p["gains"], (l * n_sub + 2,)),
                  _slab_spec(p["gains"], (l * n_sub + 3,)),
                  row(a_ret.shape[1]), row(y_s5.shape[1]), row(o_mla.shape[1])]
                 + [_slab_spec(p[n], (l,)) for n in names],
        out_specs=row(D),
        out_shape=jax.ShapeDtypeStruct((T, D), F32),
        scratch_shapes=[pltpu.VMEM((MERGE_PIECE, D), BF16)],
        compiler_params=_cparams(("arbitrary",)),
        name="gated_merge",
    )(x2, x2, p["gains"], p["gains"], a_ret, y_s5, o_mla, *[p[n] for n in names])


def _prepare(norm_gains, ffn_w_gate, ffn_w_up, ffn_w_down, w_in, ret_w_o,
             s5_a_re, s5_a_im, s5_log_dt, s5_b_re, s5_b_im, s5_c_re, s5_c_im, s5_d,
             s5_glu_a, s5_glu_b, mla_q_norm, mla_kv_norm, mla_w_uq, mla_w_ukv, mla_w_o, w_out):
    depth, n_sub, D = norm_gains.shape
    q_rank = mla_q_norm.shape[-1]
    kv_rank = mla_kv_norm.shape[-1]
    s5_w = s5_d.shape[-1]
    ret_w = 2 * RET_HEADS * RET_DK + 2 * RET_HEADS * RET_DV
    m_lo = ret_w + s5_w
    m_w = q_rank + kv_rank + MLA_ROPE
    assert w_in.shape[-1] == m_lo + m_w + 3 * D
    hd_w = MLA_NOPE + MLA_ROPE
    wuq = jnp.pad(mla_w_uq.reshape(depth, q_rank, MLA_HEADS, hd_w),
                  ((0, 0), (0, 0), (0, 0), (0, MLA_HEAD_PAD - hd_w)))
    wukv = mla_w_ukv.reshape(depth, kv_rank, MLA_HEADS, MLA_NOPE + MLA_DV)
    p = {
        "gains": norm_gains.reshape(depth * n_sub, 1, D),
        "ffn_wg": ffn_w_gate.astype(BF16), "ffn_wu": ffn_w_up.astype(BF16),
        "ffn_wd": ffn_w_down.astype(BF16),
        "w_in": w_in.astype(BF16),
        "wm": jnp.pad(w_in[:, :, m_lo:m_lo + m_w],
                      ((0, 0), (0, 0), (0, LANES - MLA_ROPE))).astype(BF16),
        "wgates": w_in[:, :, m_lo + m_w:].astype(BF16),
        "qn": mla_q_norm.reshape(depth, 1, q_rank), "kvn": mla_kv_norm.reshape(depth, 1, kv_rank),
        "wuq": wuq.reshape(depth, q_rank, MLA_HEADS * MLA_HEAD_PAD).astype(BF16),
        "wuk": wukv[..., :MLA_NOPE].reshape(depth, kv_rank, MLA_HEADS * MLA_NOPE).astype(BF16),
        "wuv": wukv[..., MLA_NOPE:].reshape(depth, kv_rank, MLA_HEADS * MLA_DV).astype(BF16),
        "ret_wo": ret_w_o.astype(BF16), "glu_a": s5_glu_a.astype(BF16),
        "glu_b": s5_glu_b.astype(BF16), "mla_wo": mla_w_o.astype(BF16),
        "w_out": w_out.astype(BF16),
    }
    p.update(_s5_params(s5_a_re, s5_a_im, s5_log_dt, s5_b_re, s5_b_im, s5_c_re, s5_c_im, s5_d))
    return p


def kernel(x, positions, norm_gains, ffn_w_gate, ffn_w_up, ffn_w_down, w_in, ret_w_o,
           s5_a_re, s5_a_im, s5_log_dt, s5_b_re, s5_b_im, s5_c_re, s5_c_im, s5_d,
           s5_glu_a, s5_glu_b, mla_q_norm, mla_kv_norm, mla_w_uq, mla_w_ukv, mla_w_o, w_out):
    B, L, D = x.shape
    depth = norm_gains.shape[0]
    assert L % RET_ROWS == 0 and L % ATTN_TQ == 0 and L % S5_TL == 0
    for tile in (ROW_TILE, FFN_PIECE * FFN_PIECES, MERGE_PIECE * MERGE_PIECES):
        assert (B * L) % tile == 0
    assert B == SUBLANES, "the S5 scan keeps the batch on the sublane axis"

    p = _prepare(norm_gains, ffn_w_gate, ffn_w_up, ffn_w_down, w_in, ret_w_o,
                 s5_a_re, s5_a_im, s5_log_dt, s5_b_re, s5_b_im, s5_c_re, s5_c_im, s5_d,
                 s5_glu_a, s5_glu_b, mla_q_norm, mla_kv_norm, mla_w_uq, mla_w_ukv, mla_w_o, w_out)
    tabs = _rope_tables(positions)
    ret_tabs = _ret_tables()
    x2 = x.reshape(B * L, D)
    for l in range(depth):
        x2 = _ffn(x2, p, l, 0)
        rq, rk, rv, rg, u, mq, mk, mv = _proj(x2, p, l, tabs)
        a_ret = _retention(rq, rk, rv, rg, ret_tabs, B, L)
        y_s5 = _s5(u, p, l, B, L)
        o_mla = _attention(mq, mk, mv, B, L)
        x2 = _merge(x2, a_ret.reshape(B * L, -1), y_s5.reshape(B * L, -1),
                    o_mla.reshape(B * L, -1), p, l)
        x2 = _ffn(x2, p, l, 1)
    return x2.reshape(B, L, D)
```

```python
import math

import jax
import jax.numpy as jnp
from jax import lax
from jax.experimental import pallas as pl
from jax.experimental.pallas import tpu as pltpu

F32 = jnp.float32
BF16 = jnp.bfloat16

ROPE_BASE = 10000.0
NORM_EPS = 1e-6
GN_EPS = 1e-5
FFN_RES = 0.5
RET_HEADS = 4
RET_DK = 128
RET_DV = 128
RET_CHUNK = 128
MLA_HEADS = 4
MLA_NOPE = 128
MLA_ROPE = 64
MLA_DV = 128
MLA_HEAD_PAD = 256
MLA_LOG2_SCALE = (MLA_NOPE + MLA_ROPE) ** -0.5 * math.log2(math.e)

LANES = 128
SUBLANES = 8
VMEM_LIMIT = 56 * 1024 * 1024

ROW_TILE = 512
FFN_PIECE = 256
FFN_PIECES = 4
MERGE_PIECE = 256
MERGE_PIECES = 2
FFN_CHUNK = 1024
RET_ROWS = 512
S5_TL = 128
ATTN_TQ = 512
ATTN_HEADS_PER_STEP = 4
ATTN_LOOKAHEAD = 2


def _cparams(sem):
    return pltpu.CompilerParams(dimension_semantics=sem, vmem_limit_bytes=VMEM_LIMIT)


def _const_spec(shape):
    nd = len(shape)
    return pl.BlockSpec(shape, lambda *_: (0,) * nd, pipeline_mode=pl.Buffered(1))


def _slab_spec(arr, lead, block=None, col=0):
    tail = tuple(arr.shape[len(lead):]) if block is None else tuple(block)
    idx = tuple(lead) + (0,) * (len(tail) - 1) + (col,)
    return pl.BlockSpec((None,) * len(lead) + tail, lambda *_: idx,
                        pipeline_mode=pl.Buffered(1))


def _rms(x, g):
    y = x * lax.rsqrt(jnp.mean(x * x, axis=-1, keepdims=True) + NORM_EPS)
    return y * g


def _dot(a, b):
    return jnp.dot(a, b, preferred_element_type=F32)


def _rope_kernel(pos_ref, f_ref, a_ref, na_ref, b_ref, p_ref, np_ref, q_ref,
                 cr_ref, sr_ref, cm_ref, s1_ref, s2_ref):
    ang = pos_ref[...] * f_ref[...]
    c = jnp.cos(ang)
    s = jnp.sin(ang)
    hr = RET_DK // 2
    c_hi = pltpu.roll(c, hr, 1)
    s_hi = pltpu.roll(s, hr, 1)
    c_m2 = pltpu.roll(c, hr + MLA_ROPE // 2, 1)
    s_m2 = pltpu.roll(s, hr + MLA_ROPE // 2, 1)
    cr_ref[...] = c * a_ref[...] + c_hi * b_ref[...]
    sr_ref[...] = s * na_ref[...] + s_hi * b_ref[...]
    cm_ref[...] = c_hi * p_ref[...] + c_m2 * q_ref[...]
    s1_ref[...] = s_hi * np_ref[...]
    s2_ref[...] = s_m2 * q_ref[...]


def _rope_tables(positions):
    B, L = positions.shape
    T = B * L
    pos = positions.astype(F32).reshape(T, 1)
    inv_r = ROPE_BASE ** (-jnp.arange(0, RET_DK, 2, dtype=F32) / RET_DK)
    inv_m = ROPE_BASE ** (-jnp.arange(0, MLA_ROPE, 2, dtype=F32) / MLA_ROPE)
    hr = RET_DK // 2
    hm = MLA_ROPE // 2
    lane = jnp.arange(LANES)
    freq = jnp.concatenate([inv_r, inv_m, jnp.zeros((LANES - hr - hm,), F32)])
    a = (lane < hr).astype(F32)
    b = (lane >= hr).astype(F32)
    p = (lane < hm).astype(F32)
    q = ((lane >= hm) & (lane < 2 * hm)).astype(F32)
    rows = [r.reshape(1, LANES) for r in (freq, a, -a, b, p, -p, q)]
    tm = 2048
    row_spec = pl.BlockSpec((1, LANES), lambda i: (0, 0))
    tab_spec = pl.BlockSpec((tm, LANES), lambda i: (i, 0))
    tab_shape = jax.ShapeDtypeStruct((T, LANES), F32)
    return pl.pallas_call(
        _rope_kernel,
        grid=(T // tm,),
        in_specs=[pl.BlockSpec((tm, 1), lambda i: (i, 0))] + [row_spec] * len(rows),
        out_specs=[tab_spec] * 5,
        out_shape=[tab_shape] * 5,
        compiler_params=_cparams(("parallel",)),
        name="rope_tables",
    )(pos, *rows)


def _ffn_kernel(x_ref, xn_ref, gpre_ref, gpost_ref, wg_ref, wu_ref, wd_ref, o_ref, ha_ref):
    pr = ha_ref.shape[0]
    n_pieces = x_ref.shape[0] // pr
    gpre = gpre_ref[...]
    gpost = gpost_ref[...]

    @pl.when(pl.program_id(0) == 0)
    def _():
        ha_ref[...] = _rms(x_ref[:pr, :], gpre).astype(BF16)

    def swiglu(h):
        d_ff = wg_ref.shape[1]
        chunks = [(c, min(c + FFN_CHUNK, d_ff)) for c in range(0, d_ff, FFN_CHUNK)]
        y = None
        pending = None
        for c0, c1 in chunks + [(None, None)]:
            if c0 is not None:
                g = _dot(h, wg_ref[:, c0:c1])
                u = _dot(h, wu_ref[:, c0:c1])
            if pending is not None:
                a, p0, p1 = pending
                part = _dot(a, wd_ref[p0:p1, :])
                y = part if y is None else y + part
            if c0 is not None:
                pending = ((g * jax.nn.sigmoid(g) * u).astype(BF16), c0, c1)
        return y

    def finish(piece, y):
        rows = slice(piece * pr, (piece + 1) * pr)
        o_ref[rows, :] = x_ref[rows, :] + FFN_RES * _rms(y, gpost)

    y_prev = swiglu(ha_ref[...])
    for piece in range(1, n_pieces):
        y = swiglu(_rms(x_ref[piece * pr:(piece + 1) * pr, :], gpre).astype(BF16))
        finish(piece - 1, y_prev)
        y_prev = y
    ha_ref[...] = _rms(xn_ref[...], gpre).astype(BF16)
    finish(n_pieces - 1, y_prev)


def _ffn(x2, p, l, k):
    T, D = x2.shape
    half = FFN_PIECE
    tile = FFN_PIECE * FFN_PIECES
    n = T // tile
    row = pl.BlockSpec((tile, D), lambda i: (i, 0))
    nxt = pl.BlockSpec((half, D), lambda i: (jnp.minimum(i + 1, n - 1) * FFN_PIECES, 0))
    n_sub = p["gains"].shape[0] // p["ffn_wg"].shape[0]
    gi = l * n_sub + 4 * k
    return pl.pallas_call(
        _ffn_kernel,
        grid=(n,),
        in_specs=[row, nxt, _slab_spec(p["gains"], (gi,)), _slab_spec(p["gains"], (gi + 1,)),
                  _slab_spec(p["ffn_wg"], (l, k)), _slab_spec(p["ffn_wu"], (l, k)),
                  _slab_spec(p["ffn_wd"], (l, k))],
        out_specs=row,
        out_shape=jax.ShapeDtypeStruct((T, D), F32),
        scratch_shapes=[pltpu.VMEM((half, D), BF16)],
        compiler_params=_cparams(("arbitrary",)),
        name="ffn",
    )(x2, x2, p["gains"], p["gains"], p["ffn_wg"], p["ffn_wu"], p["ffn_wd"])


def _rope_ret(v, cos, sin):
    return v * cos + pltpu.roll(v, RET_DK // 2, 1) * sin


def _rope_mla(v, cos, sin1, sin2):
    hm = MLA_ROPE // 2
    return v * cos + pltpu.roll(v, LANES - hm, 1) * sin1 + pltpu.roll(v, hm, 1) * sin2


def _proj_kernel(x_ref, xn_ref, g_ref, wr_ref, wu_ref, wm_ref, qn_ref, kvn_ref,
                 wuq_ref, wuk_ref, wuv_ref, cr_ref, sr_ref, cm_ref, s1_ref, s2_ref,
                 rq_ref, rk_ref, rv_ref, rg_ref, u_ref, mq_ref, mk_ref, mv_ref, ha_ref):
    half = x_ref.shape[0] // 2
    gain = g_ref[...]
    qk_w = RET_HEADS * RET_DK
    v_w = RET_HEADS * RET_DV
    q_rank = qn_ref.shape[1]
    kv_rank = kvn_ref.shape[1]

    @pl.when(pl.program_id(0) == 0)
    def _():
        ha_ref[...] = _rms(x_ref[:half, :], gain).astype(BF16)

    def project(h):
        m = _dot(h, wm_ref[...])
        r = _dot(h, wr_ref[...])
        u = _dot(h, wu_ref[...])
        return m, r, u

    def mla_up(m):
        cq = _rms(m[:, :q_rank], qn_ref[...]).astype(BF16)
        ckv = _rms(m[:, q_rank:q_rank + kv_rank], kvn_ref[...]).astype(BF16)
        q = _dot(cq, wuq_ref[...])
        kn = _dot(ckv, wuk_ref[...])
        v = _dot(ckv, wuv_ref[...])
        return q, kn, v

    def store(rows, m, r, u, q, kn, v):
        cr = cr_ref[rows, :]
        sr = sr_ref[rows, :]
        for hd in range(RET_HEADS):
            lo = hd * RET_DK
            hcols = slice(lo, lo + RET_DK)
            qr = _rope_ret(r[:, lo:lo + RET_DK], cr, sr)
            kr = _rope_ret(r[:, qk_w + lo:qk_w + lo + RET_DK], cr, sr) * (RET_DK ** -0.5)
            rq_ref[rows, hcols] = qr.astype(BF16)
            rk_ref[rows, hcols] = kr.astype(BF16)
        rv_ref[rows, :] = r[:, 2 * qk_w:2 * qk_w + v_w].astype(BF16)
        rg_ref[rows, :] = r[:, 2 * qk_w + v_w:].astype(BF16)
        u_ref[rows, :] = u
        cm = cm_ref[rows, :]
        s1 = s1_ref[rows, :]
        s2 = s2_ref[rows, :]
        kpe = _rope_mla(m[:, q_rank + kv_rank:], cm, s1, s2).astype(BF16)
        mv_ref[rows, :] = v.astype(BF16)
        for hd in range(MLA_HEADS):
            lo = hd * MLA_HEAD_PAD
            mq_ref[rows, lo:lo + MLA_NOPE] = (q[:, lo:lo + MLA_NOPE] * MLA_LOG2_SCALE).astype(BF16)
            mq_ref[rows, lo + MLA_NOPE:lo + MLA_HEAD_PAD] = (_rope_mla(
                q[:, lo + MLA_NOPE:lo + MLA_HEAD_PAD], cm, s1, s2) * MLA_LOG2_SCALE).astype(BF16)
            mk_ref[rows, lo:lo + MLA_NOPE] = kn[:, hd * MLA_NOPE:(hd + 1) * MLA_NOPE].astype(BF16)
            mk_ref[rows, lo + MLA_NOPE:lo + MLA_HEAD_PAD] = kpe

    pa = project(ha_ref[...])
    pb = project(_rms(x_ref[half:, :], gain).astype(BF16))
    ua = mla_up(pa[0])
    ub = mla_up(pb[0])
    store(slice(0, half), *pa, *ua)
    ha_ref[...] = _rms(xn_ref[...], gain).astype(BF16)
    store(slice(half, 2 * half), *pb, *ub)


def _proj(x2, p, l, tabs):
    T, D = x2.shape
    tm = ROW_TILE
    n = T // tm
    row = lambda n: pl.BlockSpec((tm, n), lambda i: (i, 0))
    nxt = pl.BlockSpec((tm // 2, D), lambda i: (jnp.minimum(2 * i + 2, 2 * n - 2), 0))
    n_sub = p["gains"].shape[0] // p["w_in"].shape[0]
    ret_w = 2 * RET_HEADS * RET_DK + 2 * RET_HEADS * RET_DV
    s5_w = p["s5_d"].shape[-1]
    assert ret_w % s5_w == 0
    consts = [(p["gains"], _slab_spec(p["gains"], (l * n_sub + 2,))),
              (p["w_in"], _slab_spec(p["w_in"], (l,), block=(D, ret_w), col=0)),
              (p["w_in"], _slab_spec(p["w_in"], (l,), block=(D, s5_w), col=ret_w // s5_w)),
              (p["wm"], _slab_spec(p["wm"], (l,))),
              (p["qn"], _slab_spec(p["qn"], (l,))), (p["kvn"], _slab_spec(p["kvn"], (l,))),
              (p["wuq"], _slab_spec(p["wuq"], (l,))), (p["wuk"], _slab_spec(p["wuk"], (l,))),
              (p["wuv"], _slab_spec(p["wuv"], (l,)))]
    out_w = [(RET_HEADS * RET_DK, BF16)] * 2 + [
             (RET_HEADS * RET_DV, BF16), (RET_HEADS * RET_DV, BF16), (s5_w, F32),
             (MLA_HEADS * MLA_HEAD_PAD, BF16), (MLA_HEADS * MLA_HEAD_PAD, BF16),
             (MLA_HEADS * MLA_DV, BF16)]
    return pl.pallas_call(
        _proj_kernel,
        grid=(n,),
        in_specs=[row(D), nxt] + [s for _, s in consts] + [row(LANES)] * 5,
        out_specs=[row(w) for w, _ in out_w],
        out_shape=[jax.ShapeDtypeStruct((T, w), dt) for w, dt in out_w],
        scratch_shapes=[pltpu.VMEM((tm // 2, D), BF16)],
        compiler_params=_cparams(("arbitrary",)),
        name="mixer_proj",
    )(x2, x2, *[a for a, _ in consts], *tabs)


def _ret_kernel(q_ref, k_ref, v_ref, g_ref, intra_ref, qd_ref, kd_ref, cd_ref, o_ref, st_ref):
    @pl.when(pl.program_id(1) == 0)
    def _():
        st_ref[...] = jnp.zeros_like(st_ref)

    C = RET_CHUNK
    states = [st_ref[hd] for hd in range(RET_HEADS)]
    for c in range(RET_ROWS // C):
        rows = slice(c * C, (c + 1) * C)
        for hd in range(RET_HEADS):
            cols = slice(hd * RET_DK, (hd + 1) * RET_DK)
            vcols = slice(hd * RET_DV, (hd + 1) * RET_DV)
            q = q_ref[0, rows, cols]
            kt = k_ref[0, rows, cols].T
            v = v_ref[0, rows, vcols]
            scores = _dot(q, kt) * intra_ref[hd]
            inner = _dot(scores.astype(BF16), v)
            cross = _dot((q.astype(F32) * qd_ref[hd]).astype(BF16), states[hd].astype(BF16))
            kv = _dot((kt.astype(F32) * kd_ref[hd]).astype(BF16), v)
            states[hd] = cd_ref[hd] * states[hd] + kv
            o = inner + cross
            mu = jnp.mean(o, axis=-1, keepdims=True)
            var = jnp.mean(jnp.square(o - mu), axis=-1, keepdims=True)
            on = (o - mu) * lax.rsqrt(var + GN_EPS)
            g = g_ref[0, rows, vcols].astype(F32)
            o_ref[0, rows, vcols] = (g * jax.nn.sigmoid(g) * on).astype(BF16)
    for hd in range(RET_HEADS):
        st_ref[hd] = states[hd]


def _ret_tables():
    C = RET_CHUNK
    log_gamma = jnp.log1p(-jnp.exp2(-5.0 - jnp.arange(RET_HEADS, dtype=F32)))
    pos = jnp.arange(C, dtype=F32)
    rel = pos[:, None] - pos[None, :]
    intra = jnp.where(rel[None] >= 0.0,
                      jnp.exp(jnp.maximum(rel, 0.0)[None] * log_gamma[:, None, None]), 0.0)
    k_decay = jnp.exp((C - 1.0 - pos)[:, None] * log_gamma[None, :])
    q_decay = jnp.exp((pos + 1.0)[:, None] * log_gamma[None, :])
    chunk_decay = jnp.exp(C * log_gamma)
    qd = jnp.broadcast_to(q_decay.T[:, :, None], (RET_HEADS, C, RET_DK))
    kd = jnp.broadcast_to(k_decay.T[:, None, :], (RET_HEADS, RET_DK, C))
    cd = jnp.broadcast_to(chunk_decay[:, None, None], (RET_HEADS, 1, RET_DV))
    return intra, qd, kd, cd


def _retention(rq, rk, rv, rg, tables, B, L):
    W = rq.shape[-1]
    shp = (B, L, W)
    blk = pl.BlockSpec((1, RET_ROWS, W), lambda b, i: (b, i, 0))
    seq = [a.reshape(shp) for a in (rq, rk, rv, rg)]
    return pl.pallas_call(
        _ret_kernel,
        grid=(B, L // RET_ROWS),
        in_specs=[blk] * len(seq) + [_const_spec(t.shape) for t in tables],
        out_specs=blk,
        out_shape=jax.ShapeDtypeStruct(shp, BF16),
        scratch_shapes=[pltpu.VMEM((RET_HEADS, RET_DK, RET_DV), F32)],
        compiler_params=_cparams(("parallel", "arbitrary")),
        name="retention",
    )(*seq, *tables)


def _s5_zoh_kernel(are_ref, aim_ref, ldt_ref, bre_ref, bim_ref,
                   abr_ref, abi_ref, bbr_ref, bbi_ref):
    a_re = are_ref[...]
    a_im = aim_ref[...]
    dt = jnp.exp(ldt_ref[...])
    mag = jnp.exp(a_re * dt)
    abar_re = mag * jnp.cos(a_im * dt)
    abar_im = mag * jnp.sin(a_im * dt)
    den = a_re * a_re + a_im * a_im
    nr = abar_re - 1.0
    f_re = (nr * a_re + abar_im * a_im) / den
    f_im = (abar_im * a_re - nr * a_im) / den
    b_re = bre_ref[...]
    b_im = bim_ref[...]
    abr_ref[...] = abar_re
    abi_ref[...] = abar_im
    bbr_ref[...] = f_re * b_re - f_im * b_im
    bbi_ref[...] = f_re * b_im + f_im * b_re


def _s5_params(a_re, a_im, log_dt, b_re, b_im, c_re, c_im, d):
    Dp, G, P = a_re.shape
    H = b_re.shape[-1]
    R = Dp * G * H
    rep = lambda a: jnp.broadcast_to(a[:, :, None, :], (Dp, G, H, a.shape[-1])).reshape(R, -1)
    bt = lambda b: jnp.swapaxes(b, 2, 3).reshape(R, P)
    args = [rep(a_re), rep(a_im), rep(log_dt[:, :, None]), bt(b_re), bt(b_im)]
    out = jax.ShapeDtypeStruct((R, P), F32)
    abr, abi, bbr, bbi = pl.pallas_call(
        _s5_zoh_kernel, out_shape=[out] * 4, name="s5_discretise",
        compiler_params=pltpu.CompilerParams(vmem_limit_bytes=VMEM_LIMIT),
    )(*args)
    abar_re = abr.reshape(Dp, G, H, P)[:, :, 0, :].reshape(Dp, 1, G * P)
    abar_im = abi.reshape(Dp, G, H, P)[:, :, 0, :].reshape(Dp, 1, G * P)
    gpb = LANES // H
    nblk = G // gpb
    group_of = lambda n, per: jnp.arange(n) // per

    def block_diag(rows2d, per_row, per_col):
        n_rows, n_cols = rows2d.shape[-2], rows2d.shape[-1] * gpb
        keep = group_of(n_rows, per_row)[:, None] == group_of(n_cols, per_col)[None, :]
        return jnp.where(keep, jnp.tile(rows2d, (1, 1, 1, gpb)), 0.0).astype(BF16)

    def pack_in(bb):
        return block_diag(bb.reshape(Dp, nblk, gpb * H, P), H, P)

    def pack_out(cc):
        return block_diag(jnp.swapaxes(cc, 2, 3).reshape(Dp, nblk, gpb * P, H), P, H)

    return {"s5_are": abar_re, "s5_aim": abar_im,
            "s5_wbr": pack_in(bbr), "s5_wbi": pack_in(bbi),
            "s5_wcr": pack_out(c_re), "s5_wci": pack_out(c_im),
            "s5_d": d.reshape(Dp, 1, G * H)}


def _s5_kernel(u_ref, are_ref, aim_ref, wbr_ref, wbi_ref, wcr_ref, wci_ref, d_ref, y_ref,
               sre_ref, sim_ref, *block_refs):
    B, tl, W = u_ref.shape
    nblk = wbr_ref.shape[0]
    in_w = wbr_ref.shape[1]
    st_w = wbr_ref.shape[2]
    per_block = len(block_refs) // nblk
    blocks = [block_refs[kb * per_block:(kb + 1) * per_block] for kb in range(nblk)]

    @pl.when(pl.program_id(0) == 0)
    def _():
        sre_ref[...] = jnp.zeros_like(sre_ref)
        sim_ref[...] = jnp.zeros_like(sim_ref)

    assert in_w == LANES and nblk * LANES == W

    def gather(kb):
        u2_ref, ut_ref = blocks[kb][0], blocks[kb][1]
        for b in range(B):
            u2_ref[b * tl:(b + 1) * tl, :] = u_ref[b, :, kb * LANES:(kb + 1) * LANES]
        for t in range(tl):
            ut_ref[t * B:(t + 1) * B, :] = u2_ref[pl.ds(t, B, stride=tl), :]

    def emit(kb):
        yt_ref = blocks[kb][4]
        for b in range(B):
            y_ref[b, :, kb * LANES:(kb + 1) * LANES] = yt_ref[pl.ds(b, tl, stride=B), :]

    def drive(kb):
        _, ut_ref, bre_ref, bim_ref, _ = blocks[kb]
        lhs = ut_ref[...].astype(BF16)
        bre_ref[...] = _dot(lhs, wbr_ref[kb])
        bim_ref[...] = _dot(lhs, wbi_ref[kb])

    def scan(kb):
        _, _, bre_ref, bim_ref, _ = blocks[kb]
        sl = slice(kb * st_w, (kb + 1) * st_w)
        ar = jnp.broadcast_to(are_ref[:, sl], (B, st_w))
        ai = jnp.broadcast_to(aim_ref[:, sl], (B, st_w))
        sr = sre_ref[:, sl]
        si = sim_ref[:, sl]
        for t in range(tl):
            rows = slice(t * B, (t + 1) * B)
            sr, si = (ar * sr - ai * si + bre_ref[rows, :],
                      ar * si + ai * sr + bim_ref[rows, :])
            bre_ref[rows, :] = sr
            bim_ref[rows, :] = si
        sre_ref[:, sl] = sr
        sim_ref[:, sl] = si

    def project(kb):
        _, ut_ref, bre_ref, bim_ref, yt_ref = blocks[kb]
        y = (_dot(bre_ref[...].astype(BF16), wcr_ref[kb])
             - _dot(bim_ref[...].astype(BF16), wci_ref[kb]))
        yt_ref[...] = jax.nn.gelu(y + d_ref[:, kb * in_w:(kb + 1) * in_w] * ut_ref[...])

    gather(0)
    drive(0)
    for kb in range(nblk):
        if kb + 1 < nblk:
            gather(kb + 1)
            drive(kb + 1)
        scan(kb)
        if kb >= 1:
            project(kb - 1)
            emit(kb - 1)
    project(nblk - 1)
    emit(nblk - 1)


def _s5(u, p, l, B, L):
    W = u.shape[-1]
    tl = S5_TL
    n_state = p["s5_are"].shape[-1]
    nblk, in_w, st_w = p["s5_wbr"].shape[1:]
    blk = pl.BlockSpec((B, tl, W), lambda i: (0, i, 0))
    names = ["s5_are", "s5_aim", "s5_wbr", "s5_wbi", "s5_wcr", "s5_wci", "s5_d"]
    rows = B * tl
    per_block = [pltpu.VMEM((rows, in_w), F32), pltpu.VMEM((rows, in_w), F32),
                 pltpu.VMEM((rows, st_w), F32), pltpu.VMEM((rows, st_w), F32),
                 pltpu.VMEM((rows, in_w), F32)]
    return pl.pallas_call(
        _s5_kernel,
        grid=(L // tl,),
        in_specs=[blk] + [_slab_spec(p[n], (l,)) for n in names],
        out_specs=blk,
        out_shape=jax.ShapeDtypeStruct((B, L, W), F32),
        scratch_shapes=[pltpu.VMEM((B, n_state), F32), pltpu.VMEM((B, n_state), F32)]
                       + per_block * nblk,
        compiler_params=_cparams(("arbitrary",)),
        name="s5_scan",
    )(u.reshape(B, L, W), *[p[n] for n in names])


def _attn_kernel(q_ref, k_ref, v_ref, bias_ref, o_ref, s_ref, m_ref, mprev_ref, l_ref, acc_ref,
                 vt_ref, qt_ref):
    tq = q_ref.shape[1]
    nh = m_ref.shape[0]
    qi = pl.program_id(2)

    @pl.when(qi == 0)
    def _():
        for h in range(nh):
            for blk in range(v_ref.shape[1] // tq):
                rows = slice(blk * tq, (blk + 1) * tq)
                vt_ref[h, :, rows] = v_ref[0, rows, h * MLA_DV:(h + 1) * MLA_DV].T

    m_ref[...] = jnp.full(m_ref.shape, -1e30, F32)
    l_ref[...] = jnp.zeros_like(l_ref)
    acc_ref[...] = jnp.zeros_like(acc_ref)
    for h in range(nh):
        qt_ref[h] = q_ref[0, :, h * MLA_HEAD_PAD:(h + 1) * MLA_HEAD_PAD].T

    def scores(j, h, masked):
        r0 = pl.multiple_of(j * tq, tq)
        qk = slice(h * MLA_HEAD_PAD, (h + 1) * MLA_HEAD_PAD)
        st = _dot(k_ref[0, pl.ds(r0, tq), qk], qt_ref[h])
        if masked:
            st = st + bias_ref[...]
        s_ref[h] = st
        m_old = m_ref[h]
        mprev_ref[h] = m_old
        m_ref[h] = jnp.maximum(m_old, jnp.max(st, axis=0, keepdims=True))

    def accumulate(j, h):
        r0 = pl.multiple_of(j * tq, tq)
        m_new = m_ref[h]
        alpha = jnp.exp2(mprev_ref[h] - m_new)
        pt = jnp.exp2(s_ref[h] - m_new)
        l_ref[h] = alpha * l_ref[h] + jnp.sum(pt, axis=0, keepdims=True)
        acc_ref[h] = alpha * acc_ref[h] + _dot(vt_ref[h, :, pl.ds(r0, tq)],
                                               pt.astype(BF16))

    la = ATTN_LOOKAHEAD
    assert 0 < la < nh
    for h in range(nh):
        scores(qi, h, True)
        if h >= la:
            accumulate(qi, h - la)

    def one_block(j):
        prev = jnp.where(j == 0, qi, j - 1)
        for h in range(nh):
            scores(j, h, False)
            if h >= la:
                accumulate(j, h - la)
            else:
                accumulate(prev, h - la + nh)

    @pl.when(qi % 2 == 1)
    def _():
        one_block(0)

    @pl.when(qi % 4 >= 2)
    def _():
        j = qi % 2
        one_block(j)
        one_block(j + 1)

    def body(t, carry):
        j = qi % 4 + 4 * t
        for d in range(4):
            one_block(j + d)
        return carry
    lax.fori_loop(0, qi // 4, body, 0)
    last = jnp.maximum(qi - 1, 0)
    for h in range(nh - la, nh):
        accumulate(last, h)
    for h in range(nh):
        o_ref[0, :, h * MLA_DV:(h + 1) * MLA_DV] = (acc_ref[h] / l_ref[h]).T.astype(BF16)


def _attention(mq, mk, mv, B, L):
    tq = ATTN_TQ
    nh = ATTN_HEADS_PER_STEP
    qw = nh * MLA_HEAD_PAD
    vw = nh * MLA_DV
    kidx = lax.broadcasted_iota(jnp.int32, (tq, tq), 0)
    qidx = lax.broadcasted_iota(jnp.int32, (tq, tq), 1)
    bias = jnp.where(kidx <= qidx, 0.0, -jnp.inf).astype(F32)
    return pl.pallas_call(
        _attn_kernel,
        grid=(B, MLA_HEADS // nh, L // tq),
        in_specs=[pl.BlockSpec((1, tq, qw), lambda b, h, i: (b, i, h)),
                  pl.BlockSpec((1, L, qw), lambda b, h, i: (b, 0, h)),
                  pl.BlockSpec((1, L, vw), lambda b, h, i: (b, 0, h)),
                  _const_spec(bias.shape)],
        out_specs=pl.BlockSpec((1, tq, vw), lambda b, h, i: (b, i, h)),
        out_shape=jax.ShapeDtypeStruct((B, L, MLA_HEADS * MLA_DV), BF16),
        scratch_shapes=[pltpu.VMEM((nh, tq, tq), F32), pltpu.VMEM((nh, 1, tq), F32),
                        pltpu.VMEM((nh, 1, tq), F32), pltpu.VMEM((nh, 1, tq), F32),
                        pltpu.VMEM((nh, MLA_DV, tq), F32), pltpu.VMEM((nh, MLA_DV, L), BF16),
                        pltpu.VMEM((nh, MLA_HEAD_PAD, tq), BF16)],
        compiler_params=_cparams(("parallel", "parallel", "arbitrary")),
        name="mla_attention",
    )(mq.reshape(B, L, -1), mk.reshape(B, L, -1), mv.reshape(B, L, -1), bias)


def _merge_kernel(x_ref, xn_ref, gin_ref, gout_ref, a_ref, y_ref, o_ref_in, wg_ref, wro_ref,
                  wga_ref, wgb_ref, wmo_ref, wout_ref, out_ref, ha_ref):
    pr = ha_ref.shape[0]
    n_pieces = x_ref.shape[0] // pr
    D = x_ref.shape[1]
    gin = gin_ref[...]
    gout = gout_ref[...]

    @pl.when(pl.program_id(0) == 0)
    def _():
        ha_ref[...] = _rms(x_ref[:pr, :], gin).astype(BF16)

    def branches(h, piece):
        rows = slice(piece * pr, (piece + 1) * pr)
        ys = y_ref[rows, :].astype(BF16)
        return (_dot(h, wg_ref[:, 0:D]), _dot(a_ref[rows, :], wro_ref[...]),
                _dot(h, wg_ref[:, D:2 * D]), _dot(ys, wga_ref[...]), _dot(ys, wgb_ref[...]),
                _dot(h, wg_ref[:, 2 * D:3 * D]), _dot(o_ref_in[rows, :], wmo_ref[...]))

    def combine(g0, y_ret, g1, ya, yb, g2, y_mla):
        merged = jax.nn.sigmoid(g0) * y_ret
        merged = merged + jax.nn.sigmoid(g1) * (ya * jax.nn.sigmoid(yb))
        merged = merged + jax.nn.sigmoid(g2) * y_mla
        return merged.astype(BF16)

    def finish(piece, o):
        rows = slice(piece * pr, (piece + 1) * pr)
        out_ref[rows, :] = x_ref[rows, :] + _rms(o, gout)

    b_prev = branches(ha_ref[...], 0)
    o_prev = None
    for piece in range(1, n_pieces):
        b = branches(_rms(x_ref[piece * pr:(piece + 1) * pr, :], gin).astype(BF16), piece)
        o = _dot(combine(*b_prev), wout_ref[...])
        if o_prev is not None:
            finish(piece - 2, o_prev)
        b_prev, o_prev = b, o
    o = _dot(combine(*b_prev), wout_ref[...])
    if o_prev is not None:
        finish(n_pieces - 2, o_prev)
    ha_ref[...] = _rms(xn_ref[...], gin).astype(BF16)
    finish(n_pieces - 1, o)


def _merge(x2, a_ret, y_s5, o_mla, p, l):
    T, D = x2.shape
    tm = MERGE_PIECE * MERGE_PIECES
    n = T // tm
    row = lambda w: pl.BlockSpec((tm, w), lambda i: (i, 0))
    nxt = pl.BlockSpec((MERGE_PIECE, D),
                       lambda i: (jnp.minimum(i + 1, n - 1) * MERGE_PIECES, 0))
    n_sub = p["gains"].shape[0] // p["wgates"].shape[0]
    names = ["wgates", "ret_wo", "glu_a", "glu_b", "mla_wo", "w_out"]
    return pl.pallas_call(
        _merge_kernel,
        grid=(n,),
        in_specs=[row(D), nxt, _slab_spec(p["gains"], (l * n_sub + 2,)),
                  _slab_spec(p["gains"], (l * n_sub + 3,)),
                  row(a_ret.shape[1]), row(y_s5.shape[1]), row(o_mla.shape[1])]
                 + [_slab_spec(p[n], (l,)) for n in names],
        out_specs=row(D),
        out_shape=jax.ShapeDtypeStruct((T, D), F32),
        scratch_shapes=[pltpu.VMEM((MERGE_PIECE, D), BF16)],
        compiler_params=_cparams(("arbitrary",)),
        name="gated_merge",
    )(x2, x2, p["gains"], p["gains"], a_ret, y_s5, o_mla, *[p[n] for n in names])


def _prepare(norm_gains, ffn_w_gate, ffn_w_up, ffn_w_down, w_in, ret_w_o,
             s5_a_re, s5_a_im, s5_log_dt, s5_b_re, s5_b_im, s5_c_re, s5_c_im, s5_d,
             s5_glu_a, s5_glu_b, mla_q_norm, mla_kv_norm, mla_w_uq, mla_w_ukv, mla_w_o, w_out):
    depth, n_sub, D = norm_gains.shape
    q_rank = mla_q_norm.shape[-1]
    kv_rank = mla_kv_norm.shape[-1]
    s5_w = s5_d.shape[-1]
    ret_w = 2 * RET_HEADS * RET_DK + 2 * RET_HEADS * RET_DV
    m_lo = ret_w + s5_w
    m_w = q_rank + kv_rank + MLA_ROPE
    assert w_in.shape[-1] == m_lo + m_w + 3 * D
    hd_w = MLA_NOPE + MLA_ROPE
    wuq = jnp.pad(mla_w_uq.reshape(depth, q_rank, MLA_HEADS, hd_w),
                  ((0, 0), (0, 0), (0, 0), (0, MLA_HEAD_PAD - hd_w)))
    wukv = mla_w_ukv.reshape(depth, kv_rank, MLA_HEADS, MLA_NOPE + MLA_DV)
    p = {
        "gains": norm_gains.reshape(depth * n_sub, 1, D),
        "ffn_wg": ffn_w_gate.astype(BF16), "ffn_wu": ffn_w_up.astype(BF16),
        "ffn_wd": ffn_w_down.astype(BF16),
        "w_in": w_in[:, :, :m_lo].astype(BF16),
        "wm": jnp.pad(w_in[:, :, m_lo:m_lo + m_w],
                      ((0, 0), (0, 0), (0, LANES - MLA_ROPE))).astype(BF16),
        "wgates": w_in[:, :, m_lo + m_w:].astype(BF16),
        "qn": mla_q_norm.reshape(depth, 1, q_rank), "kvn": mla_kv_norm.reshape(depth, 1, kv_rank),
        "wuq": wuq.reshape(depth, q_rank, MLA_HEADS * MLA_HEAD_PAD).astype(BF16),
        "wuk": wukv[..., :MLA_NOPE].reshape(depth, kv_rank, MLA_HEADS * MLA_NOPE).astype(BF16),
        "wuv": wukv[..., MLA_NOPE:].reshape(depth, kv_rank, MLA_HEADS * MLA_DV).astype(BF16),
        "ret_wo": ret_w_o.astype(BF16), "glu_a": s5_glu_a.astype(BF16),
        "glu_b": s5_glu_b.astype(BF16), "mla_wo": mla_w_o.astype(BF16),
        "w_out": w_out.astype(BF16),
    }
    p.update(_s5_params(s5_a_re, s5_a_im, s5_log_dt, s5_b_re, s5_b_im, s5_c_re, s5_c_im, s5_d))
    return p


def kernel(x, positions, norm_gains, ffn_w_gate, ffn_w_up, ffn_w_down, w_in, ret_w_o,
           s5_a_re, s5_a_im, s5_log_dt, s5_b_re, s5_b_im, s5_c_re, s5_c_im, s5_d,
           s5_glu_a, s5_glu_b, mla_q_norm, mla_kv_norm, mla_w_uq, mla_w_ukv, mla_w_o, w_out):
    B, L, D = x.shape
    depth = norm_gains.shape[0]
    assert L % RET_ROWS == 0 and L % ATTN_TQ == 0 and L % S5_TL == 0
    for tile in (ROW_TILE, FFN_PIECE * FFN_PIECES, MERGE_PIECE * MERGE_PIECES):
        assert (B * L) % tile == 0
    assert B == SUBLANES, "the S5 scan keeps the batch on the sublane axis"

    p = _prepare(norm_gains, ffn_w_gate, ffn_w_up, ffn_w_down, w_in, ret_w_o,
                 s5_a_re, s5_a_im, s5_log_dt, s5_b_re, s5_b_im, s5_c_re, s5_c_im, s5_d,
                 s5_glu_a, s5_glu_b, mla_q_norm, mla_kv_norm, mla_w_uq, mla_w_ukv, mla_w_o, w_out)
    tabs = _rope_tables(positions)
    ret_tabs = _ret_tables()
    x2 = x.reshape(B * L, D)
    for l in range(depth):
        x2 = _ffn(x2, p, l, 0)
        rq, rk, rv, rg, u, mq, mk, mv = _proj(x2, p, l, tabs)
        a_ret = _retention(rq, rk, rv, rg, ret_tabs, B, L)
        y_s5 = _s5(u, p, l, B, L)
        o_mla = _attention(mq, mk, mv, B, L)
        x2 = _merge(x2, a_ret.reshape(B * L, -1), y_s5.reshape(B * L, -1),
                    o_mla.reshape(B * L, -1), p, l)
        x2 = _ffn(x2, p, l, 1)
    return x2.reshape(B, L, D)
```

```python
import math

import jax
import jax.numpy as jnp
from jax import lax
from jax.experimental import pallas as pl
from jax.experimental.pallas import tpu as pltpu

F32 = jnp.float32
BF16 = jnp.bfloat16

ROPE_BASE = 10000.0
NORM_EPS = 1e-6
GN_EPS = 1e-5
FFN_RES = 0.5
RET_HEADS = 4
RET_DK = 128
RET_DV = 128
RET_CHUNK = 128
MLA_HEADS = 4
MLA_NOPE = 128
MLA_ROPE = 64
MLA_DV = 128
MLA_HEAD_PAD = 256
MLA_LOG2_SCALE = (MLA_NOPE + MLA_ROPE) ** -0.5 * math.log2(math.e)

LANES = 128
SUBLANES = 8
VMEM_LIMIT = 56 * 1024 * 1024

ROW_TILE = 512
FFN_PIECE = 256
FFN_PIECES = 4
MERGE_PIECE = 256
MERGE_PIECES = 2
FFN_CHUNK = 1024
RET_ROWS = 512
S5_TL = 128
ATTN_TQ = 512
ATTN_HEADS_PER_STEP = 4
ATTN_LOOKAHEAD = 2


def _cparams(sem):
    return pltpu.CompilerParams(dimension_semantics=sem, vmem_limit_bytes=VMEM_LIMIT)


def _const_spec(shape):
    nd = len(shape)
    return pl.BlockSpec(shape, lambda *_: (0,) * nd, pipeline_mode=pl.Buffered(1))


def _slab_spec(arr, lead, block=None, col=0):
    tail = tuple(arr.shape[len(lead):]) if block is None else tuple(block)
    idx = tuple(lead) + (0,) * (len(tail) - 1) + (col,)
    return pl.BlockSpec((None,) * len(lead) + tail, lambda *_: idx,
                        pipeline_mode=pl.Buffered(1))


def _rms(x, g):
    y = x * lax.rsqrt(jnp.mean(x * x, axis=-1, keepdims=True) + NORM_EPS)
    return y * g


def _dot(a, b):
    return jnp.dot(a, b, preferred_element_type=F32)


def _rope_kernel(pos_ref, f_ref, a_ref, na_ref, b_ref, p_ref, np_ref, q_ref,
                 cr_ref, sr_ref, cm_ref, s1_ref, s2_ref):
    ang = pos_ref[...] * f_ref[...]
    c = jnp.cos(ang)
    s = jnp.sin(ang)
    hr = RET_DK // 2
    c_hi = pltpu.roll(c, hr, 1)
    s_hi = pltpu.roll(s, hr, 1)
    c_m2 = pltpu.roll(c, hr + MLA_ROPE // 2, 1)
    s_m2 = pltpu.roll(s, hr + MLA_ROPE // 2, 1)
    cr_ref[...] = c * a_ref[...] + c_hi * b_ref[...]
    sr_ref[...] = s * na_ref[...] + s_hi * b_ref[...]
    cm_ref[...] = c_hi * p_ref[...] + c_m2 * q_ref[...]
    s1_ref[...] = s_hi * np_ref[...]
    s2_ref[...] = s_m2 * q_ref[...]


def _rope_tables(positions):
    B, L = positions.shape
    T = B * L
    pos = positions.astype(F32).reshape(T, 1)
    inv_r = ROPE_BASE ** (-jnp.arange(0, RET_DK, 2, dtype=F32) / RET_DK)
    inv_m = ROPE_BASE ** (-jnp.arange(0, MLA_ROPE, 2, dtype=F32) / MLA_ROPE)
    hr = RET_DK // 2
    hm = MLA_ROPE // 2
    lane = jnp.arange(LANES)
    freq = jnp.concatenate([inv_r, inv_m, jnp.zeros((LANES - hr - hm,), F32)])
    a = (lane < hr).astype(F32)
    b = (lane >= hr).astype(F32)
    p = (lane < hm).astype(F32)
    q = ((lane >= hm) & (lane < 2 * hm)).astype(F32)
    rows = [r.reshape(1, LANES) for r in (freq, a, -a, b, p, -p, q)]
    tm = 2048
    row_spec = pl.BlockSpec((1, LANES), lambda i: (0, 0))
    tab_spec = pl.BlockSpec((tm, LANES), lambda i: (i, 0))
    tab_shape = jax.ShapeDtypeStruct((T, LANES), F32)
    return pl.pallas_call(
        _rope_kernel,
        grid=(T // tm,),
        in_specs=[pl.BlockSpec((tm, 1), lambda i: (i, 0))] + [row_spec] * len(rows),
        out_specs=[tab_spec] * 5,
        out_shape=[tab_shape] * 5,
        compiler_params=_cparams(("parallel",)),
        name="rope_tables",
    )(pos, *rows)


def _ffn_kernel(x_ref, xn_ref, gpre_ref, gpost_ref, wg_ref, wu_ref, wd_ref, o_ref, ha_ref):
    pr = ha_ref.shape[0]
    n_pieces = x_ref.shape[0] // pr
    gpre = gpre_ref[...]
    gpost = gpost_ref[...]

    @pl.when(pl.program_id(0) == 0)
    def _():
        ha_ref[...] = _rms(x_ref[:pr, :], gpre).astype(BF16)

    def swiglu(h):
        d_ff = wg_ref.shape[1]
        chunks = [(c, min(c + FFN_CHUNK, d_ff)) for c in range(0, d_ff, FFN_CHUNK)]
        y = None
        pending = None
        for c0, c1 in chunks + [(None, None)]:
            if c0 is not None:
                g = _dot(h, wg_ref[:, c0:c1])
                u = _dot(h, wu_ref[:, c0:c1])
            if pending is not None:
                a, p0, p1 = pending
                part = _dot(a, wd_ref[p0:p1, :])
                y = part if y is None else y + part
            if c0 is not None:
                pending = ((g * jax.nn.sigmoid(g) * u).astype(BF16), c0, c1)
        return y

    def finish(piece, y):
        rows = slice(piece * pr, (piece + 1) * pr)
        o_ref[rows, :] = x_ref[rows, :] + FFN_RES * _rms(y, gpost)

    y_prev = swiglu(ha_ref[...])
    for piece in range(1, n_pieces):
        y = swiglu(_rms(x_ref[piece * pr:(piece + 1) * pr, :], gpre).astype(BF16))
        finish(piece - 1, y_prev)
        y_prev = y
    ha_ref[...] = _rms(xn_ref[...], gpre).astype(BF16)
    finish(n_pieces - 1, y_prev)


def _ffn(x2, p, l, k):
    T, D = x2.shape
    half = FFN_PIECE
    tile = FFN_PIECE * FFN_PIECES
    n = T // tile
    row = pl.BlockSpec((tile, D), lambda i: (i, 0))
    nxt = pl.BlockSpec((half, D), lambda i: (jnp.minimum(i + 1, n - 1) * FFN_PIECES, 0))
    n_sub = p["gains"].shape[0] // p["ffn_wg"].shape[0]
    gi = l * n_sub + 4 * k
    return pl.pallas_call(
        _ffn_kernel,
        grid=(n,),
        in_specs=[row, nxt, _slab_spec(p["gains"], (gi,)), _slab_spec(p["gains"], (gi + 1,)),
                  _slab_spec(p["ffn_wg"], (l, k)), _slab_spec(p["ffn_wu"], (l, k)),
                  _slab_spec(p["ffn_wd"], (l, k))],
        out_specs=row,
        out_shape=jax.ShapeDtypeStruct((T, D), F32),
        scratch_shapes=[pltpu.VMEM((half, D), BF16)],
        compiler_params=_cparams(("arbitrary",)),
        name="ffn",
    )(x2, x2, p["gains"], p["gains"], p["ffn_wg"], p["ffn_wu"], p["ffn_wd"])


def _rope_ret(v, cos, sin):
    return v * cos + pltpu.roll(v, RET_DK // 2, 1) * sin


def _rope_mla(v, cos, sin1, sin2):
    hm = MLA_ROPE // 2
    return v * cos + pltpu.roll(v, LANES - hm, 1) * sin1 + pltpu.roll(v, hm, 1) * sin2


def _proj_kernel(x_ref, xn_ref, g_ref, wr_ref, wu_ref, wm_ref, qn_ref, kvn_ref,
                 wuq_ref, wuk_ref, wuv_ref, cr_ref, sr_ref, cm_ref, s1_ref, s2_ref,
                 rq_ref, rk_ref, rv_ref, rg_ref, u_ref, mq_ref, mk_ref, mv_ref, ha_ref):
    half = x_ref.shape[0] // 2
    gain = g_ref[...]
    qk_w = RET_HEADS * RET_DK
    v_w = RET_HEADS * RET_DV
    q_rank = qn_ref.shape[1]
    kv_rank = kvn_ref.shape[1]

    @pl.when(pl.program_id(0) == 0)
    def _():
        ha_ref[...] = _rms(x_ref[:half, :], gain).astype(BF16)

    def project(h):
        m = _dot(h, wm_ref[...])
        r = _dot(h, wr_ref[...])
        u = _dot(h, wu_ref[...])
        return m, r, u

    def mla_up(m):
        cq = _rms(m[:, :q_rank], qn_ref[...]).astype(BF16)
        ckv = _rms(m[:, q_rank:q_rank + kv_rank], kvn_ref[...]).astype(BF16)
        q = _dot(cq, wuq_ref[...])
        kn = _dot(ckv, wuk_ref[...])
        v = _dot(ckv, wuv_ref[...])
        return q, kn, v

    def store(rows, m, r, u, q, kn, v):
        cr = cr_ref[rows, :]
        sr = sr_ref[rows, :]
        for hd in range(RET_HEADS):
            lo = hd * RET_DK
            hcols = slice(lo, lo + RET_DK)
            qr = _rope_ret(r[:, lo:lo + RET_DK], cr, sr)
            kr = _rope_ret(r[:, qk_w + lo:qk_w + lo + RET_DK], cr, sr) * (RET_DK ** -0.5)
            rq_ref[rows, hcols] = qr.astype(BF16)
            rk_ref[rows, hcols] = kr.astype(BF16)
        rv_ref[rows, :] = r[:, 2 * qk_w:2 * qk_w + v_w].astype(BF16)
        rg_ref[rows, :] = r[:, 2 * qk_w + v_w:].astype(BF16)
        u_ref[rows, :] = u
        cm = cm_ref[rows, :]
        s1 = s1_ref[rows, :]
        s2 = s2_ref[rows, :]
        kpe = _rope_mla(m[:, q_rank + kv_rank:], cm, s1, s2).astype(BF16)
        mv_ref[rows, :] = v.astype(BF16)
        for hd in range(MLA_HEADS):
            lo = hd * MLA_HEAD_PAD
            mq_ref[rows, lo:lo + MLA_NOPE] = (q[:, lo:lo + MLA_NOPE] * MLA_LOG2_SCALE).astype(BF16)
            mq_ref[rows, lo + MLA_NOPE:lo + MLA_HEAD_PAD] = (_rope_mla(
                q[:, lo + MLA_NOPE:lo + MLA_HEAD_PAD], cm, s1, s2) * MLA_LOG2_SCALE).astype(BF16)
            mk_ref[rows, lo:lo + MLA_NOPE] = kn[:, hd * MLA_NOPE:(hd + 1) * MLA_NOPE].astype(BF16)
            mk_ref[rows, lo + MLA_NOPE:lo + MLA_HEAD_PAD] = kpe

    pa = project(ha_ref[...])
    pb = project(_rms(x_ref[half:, :], gain).astype(BF16))
    ua = mla_up(pa[0])
    ub = mla_up(pb[0])
    store(slice(0, half), *pa, *ua)
    ha_ref[...] = _rms(xn_ref[...], gain).astype(BF16)
    store(slice(half, 2 * half), *pb, *ub)


def _proj(x2, p, l, tabs):
    T, D = x2.shape
    tm = ROW_TILE
    n = T // tm
    row = lambda n: pl.BlockSpec((tm, n), lambda i: (i, 0))
    nxt = pl.BlockSpec((tm // 2, D), lambda i: (jnp.minimum(2 * i + 2, 2 * n - 2), 0))
    n_sub = p["gains"].shape[0] // p["w_in"].shape[0]
    ret_w = 2 * RET_HEADS * RET_DK + 2 * RET_HEADS * RET_DV
    s5_w = p["s5_d"].shape[-1]
    assert ret_w % s5_w == 0
    consts = [(p["gains"], _slab_spec(p["gains"], (l * n_sub + 2,))),
              (p["w_in"], _slab_spec(p["w_in"], (l,), block=(D, ret_w), col=0)),
              (p["w_in"], _slab_spec(p["w_in"], (l,), block=(D, s5_w), col=ret_w // s5_w)),
              (p["wm"], _slab_spec(p["wm"], (l,))),
              (p["qn"], _slab_spec(p["qn"], (l,))), (p["kvn"], _slab_spec(p["kvn"], (l,))),
              (p["wuq"], _slab_spec(p["wuq"], (l,))), (p["wuk"], _slab_spec(p["wuk"], (l,))),
              (p["wuv"], _slab_spec(p["wuv"], (l,)))]
    out_w = [(RET_HEADS * RET_DK, BF16)] * 2 + [
             (RET_HEADS * RET_DV, BF16), (RET_HEADS * RET_DV, BF16), (s5_w, F32),
             (MLA_HEADS * MLA_HEAD_PAD, BF16), (MLA_HEADS * MLA_HEAD_PAD, BF16),
             (MLA_HEADS * MLA_DV, BF16)]
    return pl.pallas_call(
        _proj_kernel,
        grid=(n,),
        in_specs=[row(D), nxt] + [s for _, s in consts] + [row(LANES)] * 5,
        out_specs=[row(w) for w, _ in out_w],
        out_shape=[jax.ShapeDtypeStruct((T, w), dt) for w, dt in out_w],
        scratch_shapes=[pltpu.VMEM((tm // 2, D), BF16)],
        compiler_params=_cparams(("arbitrary",)),
        name="mixer_proj",
    )(x2, x2, *[a for a, _ in consts], *tabs)


def _ret_kernel(q_ref, k_ref, v_ref, g_ref, intra_ref, qd_ref, kd_ref, cd_ref, o_ref, st_ref):
    @pl.when(pl.program_id(1) == 0)
    def _():
        st_ref[...] = jnp.zeros_like(st_ref)

    C = RET_CHUNK
    states = [st_ref[hd] for hd in range(RET_HEADS)]
    for c in range(RET_ROWS // C):
        rows = slice(c * C, (c + 1) * C)
        for hd in range(RET_HEADS):
            cols = slice(hd * RET_DK, (hd + 1) * RET_DK)
            vcols = slice(hd * RET_DV, (hd + 1) * RET_DV)
            q = q_ref[0, rows, cols]
            kt = k_ref[0, rows, cols].T
            v = v_ref[0, rows, vcols]
            scores = _dot(q, kt) * intra_ref[hd]
            inner = _dot(scores.astype(BF16), v)
            cross = _dot((q.astype(F32) * qd_ref[hd]).astype(BF16), states[hd].astype(BF16))
            kv = _dot((kt.astype(F32) * kd_ref[hd]).astype(BF16), v)
            states[hd] = cd_ref[hd] * states[hd] + kv
            o = inner + cross
            mu = jnp.mean(o, axis=-1, keepdims=True)
            var = jnp.mean(jnp.square(o - mu), axis=-1, keepdims=True)
            on = (o - mu) * lax.rsqrt(var + GN_EPS)
            g = g_ref[0, rows, vcols].astype(F32)
            o_ref[0, rows, vcols] = (g * jax.nn.sigmoid(g) * on).astype(BF16)
    for hd in range(RET_HEADS):
        st_ref[hd] = states[hd]


def _ret_tables():
    C = RET_CHUNK
    log_gamma = jnp.log1p(-jnp.exp2(-5.0 - jnp.arange(RET_HEADS, dtype=F32)))
    pos = jnp.arange(C, dtype=F32)
    rel = pos[:, None] - pos[None, :]
    intra = jnp.where(rel[None] >= 0.0,
                      jnp.exp(jnp.maximum(rel, 0.0)[None] * log_gamma[:, None, None]), 0.0)
    k_decay = jnp.exp((C - 1.0 - pos)[:, None] * log_gamma[None, :])
    q_decay = jnp.exp((pos + 1.0)[:, None] * log_gamma[None, :])
    chunk_decay = jnp.exp(C * log_gamma)
    qd = jnp.broadcast_to(q_decay.T[:, :, None], (RET_HEADS, C, RET_DK))
    kd = jnp.broadcast_to(k_decay.T[:, None, :], (RET_HEADS, RET_DK, C))
    cd = jnp.broadcast_to(chunk_decay[:, None, None], (RET_HEADS, 1, RET_DV))
    return intra, qd, kd, cd


def _retention(rq, rk, rv, rg, tables, B, L):
    W = rq.shape[-1]
    shp = (B, L, W)
    blk = pl.BlockSpec((1, RET_ROWS, W), lambda b, i: (b, i, 0))
    seq = [a.reshape(shp) for a in (rq, rk, rv, rg)]
    return pl.pallas_call(
        _ret_kernel,
        grid=(B, L // RET_ROWS),
        in_specs=[blk] * len(seq) + [_const_spec(t.shape) for t in tables],
        out_specs=blk,
        out_shape=jax.ShapeDtypeStruct(shp, BF16),
        scratch_shapes=[pltpu.VMEM((RET_HEADS, RET_DK, RET_DV), F32)],
        compiler_params=_cparams(("parallel", "arbitrary")),
        name="retention",
    )(*seq, *tables)


def _s5_zoh_kernel(are_ref, aim_ref, ldt_ref, bre_ref, bim_ref,
                   abr_ref, abi_ref, bbr_ref, bbi_ref):
    a_re = are_ref[...]
    a_im = aim_ref[...]
    dt = jnp.exp(ldt_ref[...])
    mag = jnp.exp(a_re * dt)
    abar_re = mag * jnp.cos(a_im * dt)
    abar_im = mag * jnp.sin(a_im * dt)
    den = a_re * a_re + a_im * a_im
    nr = abar_re - 1.0
    f_re = (nr * a_re + abar_im * a_im) / den
    f_im = (abar_im * a_re - nr * a_im) / den
    b_re = bre_ref[...]
    b_im = bim_ref[...]
    abr_ref[...] = abar_re
    abi_ref[...] = abar_im
    bbr_ref[...] = f_re * b_re - f_im * b_im
    bbi_ref[...] = f_re * b_im + f_im * b_re


def _s5_params(a_re, a_im, log_dt, b_re, b_im, c_re, c_im, d):
    Dp, G, P = a_re.shape
    H = b_re.shape[-1]
    R = Dp * G * H
    rep = lambda a: jnp.broadcast_to(a[:, :, None, :], (Dp, G, H, a.shape[-1])).reshape(R, -1)
    bt = lambda b: jnp.swapaxes(b, 2, 3).reshape(R, P)
    args = [rep(a_re), rep(a_im), rep(log_dt[:, :, None]), bt(b_re), bt(b_im)]
    out = jax.ShapeDtypeStruct((R, P), F32)
    abr, abi, bbr, bbi = pl.pallas_call(
        _s5_zoh_kernel, out_shape=[out] * 4, name="s5_discretise",
        compiler_params=pltpu.CompilerParams(vmem_limit_bytes=VMEM_LIMIT),
    )(*args)
    abar_re = abr.reshape(Dp, G, H, P)[:, :, 0, :].reshape(Dp, 1, G * P)
    abar_im = abi.reshape(Dp, G, H, P)[:, :, 0, :].reshape(Dp, 1, G * P)
    gpb = LANES // H
    nblk = G // gpb
    group_of = lambda n, per: jnp.arange(n) // per

    def block_diag(rows2d, per_row, per_col):
        n_rows, n_cols = rows2d.shape[-2], rows2d.shape[-1] * gpb
        keep = group_of(n_rows, per_row)[:, None] == group_of(n_cols, per_col)[None, :]
        return jnp.where(keep, jnp.tile(rows2d, (1, 1, 1, gpb)), 0.0).astype(BF16)

    def pack_in(bb):
        return block_diag(bb.reshape(Dp, nblk, gpb * H, P), H, P)

    def pack_out(cc):
        return block_diag(jnp.swapaxes(cc, 2, 3).reshape(Dp, nblk, gpb * P, H), P, H)

    return {"s5_are": abar_re, "s5_aim": abar_im,
            "s5_wbr": pack_in(bbr), "s5_wbi": pack_in(bbi),
            "s5_wcr": pack_out(c_re), "s5_wci": pack_out(c_im),
            "s5_d": d.reshape(Dp, 1, G * H)}


def _s5_kernel(u_ref, are_ref, aim_ref, wbr_ref, wbi_ref, wcr_ref, wci_ref, d_ref, y_ref,
               sre_ref, sim_ref, *block_refs):
    B, tl, W = u_ref.shape
    nblk = wbr_ref.shape[0]
    in_w = wbr_ref.shape[1]
    st_w = wbr_ref.shape[2]
    per_block = len(block_refs) // nblk
    blocks = [block_refs[kb * per_block:(kb + 1) * per_block] for kb in range(nblk)]

    @pl.when(pl.program_id(0) == 0)
    def _():
        sre_ref[...] = jnp.zeros_like(sre_ref)
        sim_ref[...] = jnp.zeros_like(sim_ref)

    assert in_w == LANES and nblk * LANES == W

    def gather(kb):
        u2_ref, ut_ref = blocks[kb][0], blocks[kb][1]
        for b in range(B):
            u2_ref[b * tl:(b + 1) * tl, :] = u_ref[b, :, kb * LANES:(kb + 1) * LANES]
        for t in range(tl):
            ut_ref[t * B:(t + 1) * B, :] = u2_ref[pl.ds(t, B, stride=tl), :]

    def emit(kb):
        yt_ref = blocks[kb][4]
        for b in range(B):
            y_ref[b, :, kb * LANES:(kb + 1) * LANES] = yt_ref[pl.ds(b, tl, stride=B), :]

    def drive(kb):
        _, ut_ref, bre_ref, bim_ref, _ = blocks[kb]
        lhs = ut_ref[...].astype(BF16)
        bre_ref[...] = _dot(lhs, wbr_ref[kb])
        bim_ref[...] = _dot(lhs, wbi_ref[kb])

    def scan(kb):
        _, _, bre_ref, bim_ref, _ = blocks[kb]
        sl = slice(kb * st_w, (kb + 1) * st_w)
        ar = jnp.broadcast_to(are_ref[:, sl], (B, st_w))
        ai = jnp.broadcast_to(aim_ref[:, sl], (B, st_w))
        sr = sre_ref[:, sl]
        si = sim_ref[:, sl]
        for t in range(tl):
            rows = slice(t * B, (t + 1) * B)
            sr, si = (ar * sr - ai * si + bre_ref[rows, :],
                      ar * si + ai * sr + bim_ref[rows, :])
            bre_ref[rows, :] = sr
            bim_ref[rows, :] = si
        sre_ref[:, sl] = sr
        sim_ref[:, sl] = si

    def project(kb):
        _, ut_ref, bre_ref, bim_ref, yt_ref = blocks[kb]
        y = (_dot(bre_ref[...].astype(BF16), wcr_ref[kb])
             - _dot(bim_ref[...].astype(BF16), wci_ref[kb]))
        yt_ref[...] = jax.nn.gelu(y + d_ref[:, kb * in_w:(kb + 1) * in_w] * ut_ref[...])

    gather(0)
    drive(0)
    for kb in range(nblk):
        if kb + 1 < nblk:
            gather(kb + 1)
            drive(kb + 1)
        scan(kb)
        if kb >= 1:
            project(kb - 1)
            emit(kb - 1)
    project(nblk - 1)
    emit(nblk - 1)


def _s5(u, p, l, B, L):
    W = u.shape[-1]
    tl = S5_TL
    n_state = p["s5_are"].shape[-1]
    nblk, in_w, st_w = p["s5_wbr"].shape[1:]
    blk = pl.BlockSpec((B, tl, W), lambda i: (0, i, 0))
    names = ["s5_are", "s5_aim", "s5_wbr", "s5_wbi", "s5_wcr", "s5_wci", "s5_d"]
    rows = B * tl
    per_block = [pltpu.VMEM((rows, in_w), F32), pltpu.VMEM((rows, in_w), F32),
                 pltpu.VMEM((rows, st_w), F32), pltpu.VMEM((rows, st_w), F32),
                 pltpu.VMEM((rows, in_w), F32)]
    return pl.pallas_call(
        _s5_kernel,
        grid=(L // tl,),
        in_specs=[blk] + [_slab_spec(p[n], (l,)) for n in names],
        out_specs=blk,
        out_shape=jax.ShapeDtypeStruct((B, L, W), F32),
        scratch_shapes=[pltpu.VMEM((B, n_state), F32), pltpu.VMEM((B, n_state), F32)]
                       + per_block * nblk,
        compiler_params=_cparams(("arbitrary",)),
        name="s5_scan",
    )(u.reshape(B, L, W), *[p[n] for n in names])


def _attn_kernel(q_ref, k_ref, v_ref, bias_ref, o_ref, s_ref, m_ref, mprev_ref, l_ref, acc_ref,
                 vt_ref, qt_ref):
    tq = q_ref.shape[1]
    nh = m_ref.shape[0]
    qi = pl.program_id(2)

    @pl.when(qi == 0)
    def _():
        for h in range(nh):
            for blk in range(v_ref.shape[1] // tq):
                rows = slice(blk * tq, (blk + 1) * tq)
                vt_ref[h, :, rows] = v_ref[0, rows, h * MLA_DV:(h + 1) * MLA_DV].T

    m_ref[...] = jnp.full(m_ref.shape, -jnp.inf, F32)
    l_ref[...] = jnp.zeros_like(l_ref)
    acc_ref[...] = jnp.zeros_like(acc_ref)
    for h in range(nh):
        qt_ref[h] = q_ref[0, :, h * MLA_HEAD_PAD:(h + 1) * MLA_HEAD_PAD].T

    def scores(j, h, masked):
        r0 = pl.multiple_of(j * tq, tq)
        qk = slice(h * MLA_HEAD_PAD, (h + 1) * MLA_HEAD_PAD)
        st = _dot(k_ref[0, pl.ds(r0, tq), qk], qt_ref[h])
        if masked:
            st = st + bias_ref[...]
        s_ref[h] = st
        m_old = m_ref[h]
        mprev_ref[h] = m_old
        m_ref[h] = jnp.maximum(m_old, jnp.max(st, axis=0, keepdims=True))

    def accumulate(j, h):
        r0 = pl.multiple_of(j * tq, tq)
        m_new = m_ref[h]
        alpha = jnp.exp2(mprev_ref[h] - m_new)
        pt = jnp.exp2(s_ref[h] - m_new)
        l_ref[h] = alpha * l_ref[h] + jnp.sum(pt, axis=0, keepdims=True)
        acc_ref[h] = alpha * acc_ref[h] + _dot(vt_ref[h, :, pl.ds(r0, tq)],
                                               pt.astype(BF16))

    la = ATTN_LOOKAHEAD
    assert 0 < la < nh
    for h in range(nh):
        scores(qi, h, True)
        if h >= la:
            accumulate(qi, h - la)

    def one_block(j):
        prev = jnp.where(j == 0, qi, j - 1)
        for h in range(nh):
            scores(j, h, False)
            if h >= la:
                accumulate(j, h - la)
            else:
                accumulate(prev, h - la + nh)

    @pl.when(qi % 2 == 1)
    def _():
        one_block(0)

    @pl.when(qi % 4 >= 2)
    def _():
        j = qi % 2
        one_block(j)
        one_block(j + 1)

    def body(t, carry):
        j = qi % 4 + 4 * t
        for d in range(4):
            one_block(j + d)
        return carry
    lax.fori_loop(0, qi // 4, body, 0)
    last = jnp.maximum(qi - 1, 0)
    for h in range(nh - la, nh):
        accumulate(last, h)
    for h in range(nh):
        o_ref[0, :, h * MLA_DV:(h + 1) * MLA_DV] = (acc_ref[h] / l_ref[h]).T.astype(BF16)


def _attention(mq, mk, mv, B, L):
    tq = ATTN_TQ
    nh = ATTN_HEADS_PER_STEP
    qw = nh * MLA_HEAD_PAD
    vw = nh * MLA_DV
    kidx = lax.broadcasted_iota(jnp.int32, (tq, tq), 0)
    qidx = lax.broadcasted_iota(jnp.int32, (tq, tq), 1)
    bias = jnp.where(kidx <= qidx, 0.0, -jnp.inf).astype(F32)
    return pl.pallas_call(
        _attn_kernel,
        grid=(B, MLA_HEADS // nh, L // tq),
        in_specs=[pl.BlockSpec((1, tq, qw), lambda b, h, i: (b, i, h)),
                  pl.BlockSpec((1, L, qw), lambda b, h, i: (b, 0, h)),
                  pl.BlockSpec((1, L, vw), lambda b, h, i: (b, 0, h)),
                  _const_spec(bias.shape)],
        out_specs=pl.BlockSpec((1, tq, vw), lambda b, h, i: (b, i, h)),
        out_shape=jax.ShapeDtypeStruct((B, L, MLA_HEADS * MLA_DV), BF16),
        scratch_shapes=[pltpu.VMEM((nh, tq, tq), F32), pltpu.VMEM((nh, 1, tq), F32),
                        pltpu.VMEM((nh, 1, tq), F32), pltpu.VMEM((nh, 1, tq), F32),
                        pltpu.VMEM((nh, MLA_DV, tq), F32), pltpu.VMEM((nh, MLA_DV, L), BF16),
                        pltpu.VMEM((nh, MLA_HEAD_PAD, tq), BF16)],
        compiler_params=_cparams(("parallel", "parallel", "arbitrary")),
        name="mla_attention",
    )(mq.reshape(B, L, -1), mk.reshape(B, L, -1), mv.reshape(B, L, -1), bias)


def _merge_kernel(x_ref, xn_ref, gin_ref, gout_ref, a_ref, y_ref, o_ref_in, wg_ref, wro_ref,
                  wga_ref, wgb_ref, wmo_ref, wout_ref, out_ref, ha_ref):
    pr = ha_ref.shape[0]
    n_pieces = x_ref.shape[0] // pr
    D = x_ref.shape[1]
    gin = gin_ref[...]
    gout = gout_ref[...]

    @pl.when(pl.program_id(0) == 0)
    def _():
        ha_ref[...] = _rms(x_ref[:pr, :], gin).astype(BF16)

    def branches(h, piece):
        rows = slice(piece * pr, (piece + 1) * pr)
        ys = y_ref[rows, :].astype(BF16)
        return (_dot(h, wg_ref[:, 0:D]), _dot(a_ref[rows, :], wro_ref[...]),
                _dot(h, wg_ref[:, D:2 * D]), _dot(ys, wga_ref[...]), _dot(ys, wgb_ref[...]),
                _dot(h, wg_ref[:, 2 * D:3 * D]), _dot(o_ref_in[rows, :], wmo_ref[...]))

    def combine(g0, y_ret, g1, ya, yb, g2, y_mla):
        merged = jax.nn.sigmoid(g0) * y_ret
        merged = merged + jax.nn.sigmoid(g1) * (ya * jax.nn.sigmoid(yb))
        merged = merged + jax.nn.sigmoid(g2) * y_mla
        return merged.astype(BF16)

    def finish(piece, o):
        rows = slice(piece * pr, (piece + 1) * pr)
        out_ref[rows, :] = x_ref[rows, :] + _rms(o, gout)

    b_prev = branches(ha_ref[...], 0)
    o_prev = None
    for piece in range(1, n_pieces):
        b = branches(_rms(x_ref[piece * pr:(piece + 1) * pr, :], gin).astype(BF16), piece)
        o = _dot(combine(*b_prev), wout_ref[...])
        if o_prev is not None:
            finish(piece - 2, o_prev)
        b_prev, o_prev = b, o
    o = _dot(combine(*b_prev), wout_ref[...])
    if o_prev is not None:
        finish(n_pieces - 2, o_prev)
    ha_ref[...] = _rms(xn_ref[...], gin).astype(BF16)
    finish(n_pieces - 1, o)


def _merge(x2, a_ret, y_s5, o_mla, p, l):
    T, D = x2.shape
    tm = MERGE_PIECE * MERGE_PIECES
    n = T // tm
    row = lambda w: pl.BlockSpec((tm, w), lambda i: (i, 0))
    nxt = pl.BlockSpec((MERGE_PIECE, D),
                       lambda i: (jnp.minimum(i + 1, n - 1) * MERGE_PIECES, 0))
    n_sub = p["gains"].shape[0] // p["wgates"].shape[0]
    names = ["wgates", "ret_wo", "glu_a", "glu_b", "mla_wo", "w_out"]
    return pl.pallas_call(
        _merge_kernel,
        grid=(n,),
        in_specs=[row(D), nxt, _slab_spec(p["gains"], (l * n_sub + 2,)),
                  _slab_spec(p["gains"], (l * n_sub + 3,)),
                  row(a_ret.shape[1]), row(y_s5.shape[1]), row(o_mla.shape[1])]
                 + [_slab_spec(p[n], (l,)) for n in names],
        out_specs=row(D),
        out_shape=jax.ShapeDtypeStruct((T, D), F32),
        scratch_shapes=[pltpu.VMEM((MERGE_PIECE, D), BF16)],
        compiler_params=_cparams(("arbitrary",)),
        name="gated_merge",
    )(x2, x2, p["gains"], p["gains"], a_ret, y_s5, o_mla, *[p[n] for n in names])


def _prepare(norm_gains, ffn_w_gate, ffn_w_up, ffn_w_down, w_in, ret_w_o,
             s5_a_re, s5_a_im, s5_log_dt, s5_b_re, s5_b_im, s5_c_re, s5_c_im, s5_d,
             s5_glu_a, s5_glu_b, mla_q_norm, mla_kv_norm, mla_w_uq, mla_w_ukv, mla_w_o, w_out):
    depth, n_sub, D = norm_gains.shape
    q_rank = mla_q_norm.shape[-1]
    kv_rank = mla_kv_norm.shape[-1]
    s5_w = s5_d.shape[-1]
    ret_w = 2 * RET_HEADS * RET_DK + 2 * RET_HEADS * RET_DV
    m_lo = ret_w + s5_w
    m_w = q_rank + kv_rank + MLA_ROPE
    assert w_in.shape[-1] == m_lo + m_w + 3 * D
    hd_w = MLA_NOPE + MLA_ROPE
    wuq = jnp.pad(mla_w_uq.reshape(depth, q_rank, MLA_HEADS, hd_w),
                  ((0, 0), (0, 0), (0, 0), (0, MLA_HEAD_PAD - hd_w)))
    wukv = mla_w_ukv.reshape(depth, kv_rank, MLA_HEADS, MLA_NOPE + MLA_DV)
    p = {
        "gains": norm_gains.reshape(depth * n_sub, 1, D),
        "ffn_wg": ffn_w_gate.astype(BF16), "ffn_wu": ffn_w_up.astype(BF16),
        "ffn_wd": ffn_w_down.astype(BF16),
        "w_in": w_in.astype(BF16),
        "wm": jnp.pad(w_in[:, :, m_lo:m_lo + m_w],
                      ((0, 0), (0, 0), (0, LANES - MLA_ROPE))).astype(BF16),
        "wgates": w_in[:, :, m_lo + m_w:].astype(BF16),
        "qn": mla_q_norm.reshape(depth, 1, q_rank), "kvn": mla_kv_norm.reshape(depth, 1, kv_rank),
        "wuq": wuq.reshape(depth, q_rank, MLA_HEADS * MLA_HEAD_PAD).astype(BF16),
        "wuk": wukv[..., :MLA_NOPE].reshape(depth, kv_rank, MLA_HEADS * MLA_NOPE).astype(BF16),
        "wuv": wukv[..., MLA_NOPE:].reshape(depth, kv_rank, MLA_HEADS * MLA_DV).astype(BF16),
        "ret_wo": ret_w_o.astype(BF16), "glu_a": s5_glu_a.astype(BF16),
        "glu_b": s5_glu_b.astype(BF16), "mla_wo": mla_w_o.astype(BF16),
        "w_out": w_out.astype(BF16),
    }
    p.update(_s5_params(s5_a_re, s5_a_im, s5_log_dt, s5_b_re, s5_b_im, s5_c_re, s5_c_im, s5_d))
    return p


def kernel(x, positions, norm_gains, ffn_w_gate, ffn_w_up, ffn_w_down, w_in, ret_w_o,
           s5_a_re, s5_a_im, s5_log_dt, s5_b_re, s5_b_im, s5_c_re, s5_c_im, s5_d,
           s5_glu_a, s5_glu_b, mla_q_norm, mla_kv_norm, mla_w_uq, mla_w_ukv, mla_w_o, w_out):
    B, L, D = x.shape
    depth = norm_gains.shape[0]
    assert L % RET_ROWS == 0 and L % ATTN_TQ == 0 and L % S5_TL == 0
    for tile in (ROW_TILE, FFN_PIECE * FFN_PIECES, MERGE_PIECE * MERGE_PIECES):
        assert (B * L) % tile == 0
    assert B == SUBLANES, "the S5 scan keeps the batch on the sublane axis"

    p = _prepare(norm_gains, ffn_w_gate, ffn_w_up, ffn_w_down, w_in, ret_w_o,
                 s5_a_re, s5_a_im, s5_log_dt, s5_b_re, s5_b_im, s5_c_re, s5_c_im, s5_d,
                 s5_glu_a, s5_glu_b, mla_q_norm, mla_kv_norm, mla_w_uq, mla_w_ukv, mla_w_o, w_out)
    tabs = _rope_tables(positions)
    ret_tabs = _ret_tables()
    x2 = x.reshape(B * L, D)
    for l in range(depth):
        x2 = _ffn(x2, p, l, 0)
        rq, rk, rv, rg, u, mq, mk, mv = _proj(x2, p, l, tabs)
        a_ret = _retention(rq, rk, rv, rg, ret_tabs, B, L)
        y_s5 = _s5(u, p, l, B, L)
        o_mla = _attention(mq, mk, mv, B, L)
        x2 = _merge(x2, a_ret.reshape(B * L, -1), y_s5.reshape(B * L, -1),
                    o_mla.reshape(B * L, -1), p, l)
        x2 = _ffn(x2, p, l, 1)
    return x2.reshape(B, L, D)
```
